```python
import math, functools
import jax, jax.numpy as jnp
from jax import lax
import numpy as np

D_MODEL = 2048
BATCH = 8
SEQ = 2048
DEPTH = 2
DEC_BATCH = 128
DEC_SEQ = 4
PAST_LEN = 8192
PAGE_SIZE = 128

DK_R = 256
DV_R = 256
H_R = D_MODEL // DK_R
RET_CHUNK = 128
ROPE_BASE = 10000.0
HEAD_DIM = 128
H_A = D_MODEL // HEAD_DIM
KV_HEADS = H_A // 4
GROUP = H_A // KV_HEADS
WINDOW = 128
ATTN_BLOCK = WINDOW
D_FF = 4 * D_MODEL
EPS = 1e-6
NEG_INF = -1e30
SPLIT_SIZES = (H_R * DK_R, H_R * DK_R, H_R * DV_R, H_R * DV_R,
               H_A * HEAD_DIM, KV_HEADS * HEAD_DIM, KV_HEADS * HEAD_DIM,
               D_MODEL, D_MODEL)
IN_WIDTH = sum(SPLIT_SIZES)

kernel_name = "hybrid_retention_swa_sink_adaln_decoder_step"


def _rms(x, w=None):
    xf = x.astype(jnp.float32)
    y = xf * lax.rsqrt(jnp.mean(xf * xf, axis=-1, keepdims=True) + EPS)
    if w is not None:
        y = y * w.astype(jnp.float32)
    return y


def _rotary(t, pos):
    half = t.shape[-1] // 2
    inv = 1.0 / (ROPE_BASE ** (jnp.arange(half, dtype=jnp.float32) / half))
    ang = pos.astype(jnp.float32)[:, None] * inv[None, :]
    cos = jnp.cos(ang)[None, :, None, :]
    sin = jnp.sin(ang)[None, :, None, :]
    t1, t2 = t[..., :half], t[..., half:]
    return jnp.concatenate([t1 * cos - t2 * sin, t1 * sin + t2 * cos], axis=-1)


def _retention(q, k, v, s0, chunk):
    b, L, h, _ = q.shape
    n = L // chunk
    log_g = jnp.log1p(-jnp.exp2(-5.0 - jnp.arange(h, dtype=jnp.float32)))
    idx = jnp.arange(chunk, dtype=jnp.float32)
    diff = idx[:, None] - idx[None, :]
    causal = diff >= 0
    d_intra = jnp.where(causal, jnp.exp(jnp.where(causal, diff, 0.0) * log_g[:, None, None]), 0.0)
    q_dec = jnp.exp((idx + 1.0) * log_g[:, None])[..., None]
    k_dec = jnp.exp((chunk - 1.0 - idx) * log_g[:, None])[..., None]
    s_dec = jnp.exp(chunk * log_g)[:, None, None]

    def to_chunks(t):
        return t.reshape(b, n, chunk, h, t.shape[-1]).transpose(1, 0, 3, 2, 4)

    def step(s, inp):
        qi, ki, vi = inp
        att = jnp.einsum('bhid,bhjd->bhij', qi, ki) * d_intra
        o = (jnp.einsum('bhij,bhjv->bhiv', att, vi)
             + jnp.einsum('bhid,bhdv->bhiv', qi * q_dec, s))
        s = s * s_dec + jnp.einsum('bhjd,bhjv->bhdv', ki * k_dec, vi)
        return s, o

    s_final, oc = lax.scan(step, s0, (to_chunks(q), to_chunks(k), to_chunks(v)))
    o = oc.transpose(1, 0, 3, 2, 4).reshape(b, L, h, v.shape[-1])
    return o, s_final


def _sink_attend(q, k, v, mask, sinks):
    s = jnp.einsum('...qhgd,...shd->...hgqs', q, k,
                   preferred_element_type=jnp.float32) * (HEAD_DIM ** -0.5)
    s = jnp.where(mask, s, NEG_INF)
    sink = sinks.astype(jnp.float32).reshape(KV_HEADS, GROUP)[:, :, None, None]
    m = jnp.maximum(jnp.max(s, axis=-1, keepdims=True), sink)
    p = jnp.exp(s - m)
    p = p / (jnp.sum(p, axis=-1, keepdims=True) + jnp.exp(sink - m))
    return jnp.einsum('...hgqs,...shd->...qhgd', p.astype(v.dtype), v,
                      preferred_element_type=jnp.float32)


def _swa_prompt(q, k, v, sinks):
    b, s = q.shape[:2]
    nb = s // ATTN_BLOCK
    qb = q.reshape(b, nb, ATTN_BLOCK, KV_HEADS, GROUP, HEAD_DIM)

    def band(t):
        pad = jnp.zeros((b, ATTN_BLOCK) + t.shape[2:], t.dtype)
        tp = jnp.concatenate([pad, t], axis=1).reshape((b, nb + 1, ATTN_BLOCK) + t.shape[2:])
        return jnp.concatenate([tp[:, :-1], tp[:, 1:]], axis=2)

    kb, vb = band(k), band(v)
    blk = jnp.arange(nb)[:, None]
    qpos = blk * ATTN_BLOCK + jnp.arange(ATTN_BLOCK)[None, :]
    kpos = (blk - 1) * ATTN_BLOCK + jnp.arange(2 * ATTN_BLOCK)[None, :]
    diff = qpos[:, :, None] - kpos[:, None, :]
    mask = (diff >= 0) & (diff <= WINDOW) & (kpos[:, None, :] >= 0)
    o = _sink_attend(qb, kb, vb, mask[:, None, None], sinks)
    return o.reshape(b, s, H_A * HEAD_DIM)


def _mixer_prep(h, pos, w_in_l, qn, kn):
    b, L = h.shape[:2]
    pts = np.cumsum(SPLIT_SIZES)[:-1].tolist()
    qr, kr, vr, gr, qa, ka, va, mr, ma = jnp.split(h @ w_in_l, pts, axis=-1)
    qr = _rotary(qr.reshape(b, L, H_R, DK_R).astype(jnp.float32), pos)
    kr = _rotary(kr.reshape(b, L, H_R, DK_R).astype(jnp.float32), pos) * (DK_R ** -0.5)
    vr = vr.reshape(b, L, H_R, DV_R).astype(jnp.float32)
    qa = _rms(qa.reshape(b, L, KV_HEADS, GROUP, HEAD_DIM), qn).astype(h.dtype)
    ka = _rms(ka.reshape(b, L, KV_HEADS, HEAD_DIM), kn).astype(h.dtype)
    va = va.reshape(b, L, KV_HEADS, HEAD_DIM)
    return qr, kr, vr, gr, qa, ka, va, mr, ma


def _mix_prompt(qr, kr, vr, qa, ka, va, sk):
    b, s = qr.shape[:2]
    s0 = jnp.zeros((b, H_R, DK_R, DV_R), jnp.float32)
    o_r, s_new = _retention(qr, kr, vr, s0, RET_CHUNK)
    o_a = _swa_prompt(qa, ka, va, sk)
    keep = min(WINDOW, s)
    return o_r, o_a, (s_new, ka[:, s - keep:], va[:, s - keep:])


def _mix_sample(qr, kr, vr, qa, ka, va, sk, *, s0, k_buf, v_buf):
    b, L = qr.shape[:2]
    o_r, s_new = _retention(qr, kr, vr, s0.astype(jnp.float32), L)
    wb = k_buf.shape[1]
    k_all = jnp.concatenate([k_buf.astype(ka.dtype), ka], axis=1)
    v_all = jnp.concatenate([v_buf.astype(va.dtype), va], axis=1)
    qpos = PAST_LEN + jnp.arange(L)
    kpos = PAST_LEN - wb + jnp.arange(wb + L)
    diff = qpos[:, None] - kpos[None, :]
    mask = (diff >= 0) & (diff <= WINDOW)
    o_a = _sink_attend(qa, k_all, v_all, mask, sk).reshape(b, L, H_A * HEAD_DIM)
    return o_r, o_a, (s_new, k_all[:, L:], v_all[:, L:])


def _layer(x, c, pos, mix, ln1, ln2, wa, ba, wi, qn, kn, sk, wo, wu, wd):
    dt = x.dtype
    sh1, sc1, g1, sh2, sc2, g2 = [m[:, None, :] for m in
                                  jnp.split(jax.nn.silu(c) @ wa + ba, 6, axis=-1)]
    h = (_rms(x, ln1) * (1.0 + sc1) + sh1).astype(dt)
    qr, kr, vr, gr, qa, ka, va, mr, ma = _mixer_prep(h, pos, wi, qn, kn)
    o_r, o_a, new_state = mix(qr, kr, vr, qa, ka, va, sk)
    b, L = x.shape[:2]
    o_r = _rms(o_r).reshape(b, L, H_R * DV_R) * jax.nn.silu(gr.astype(jnp.float32))
    merged = (jax.nn.sigmoid(mr.astype(jnp.float32)) * o_r
              + jax.nn.sigmoid(ma.astype(jnp.float32)) * o_a)
    x = x + (g1 * (merged.astype(dt) @ wo)).astype(dt)
    h = (_rms(x, ln2) * (1.0 + sc2) + sh2).astype(dt)
    x = x + (g2 * (jnp.square(jax.nn.relu(h @ wu)) @ wd)).astype(dt)
    s_new, k_new, v_new = new_state
    return x, (s_new.astype(dt), k_new, v_new)


def setup_inputs(seed: int = 0) -> dict:
    key = jax.random.key(seed)
    ks = jax.random.split(key, 20)
    f32 = jnp.float32
    win_buf = min(WINDOW, PAST_LEN)

    def nrm(k, shape, scale):
        return jax.random.normal(k, shape, f32) * scale

    return {
        "x_prompt": nrm(ks[0], (BATCH, SEQ, D_MODEL), 1.0),
        "x_sample": nrm(ks[1], (DEC_BATCH, DEC_SEQ, D_MODEL), 1.0),
        "c_prompt": nrm(ks[2], (BATCH, D_MODEL), 1.0),
        "c_sample": nrm(ks[3], (DEC_BATCH, D_MODEL), 1.0),
        "state_ret": nrm(ks[4], (DEPTH, DEC_BATCH, H_R, DK_R, DV_R), 0.5),
        "cache_k_win": nrm(ks[5], (DEPTH, DEC_BATCH, win_buf, KV_HEADS, HEAD_DIM), 1.0),
        "cache_v_win": nrm(ks[6], (DEPTH, DEC_BATCH, win_buf, KV_HEADS, HEAD_DIM), 1.0),
        "norm1_w": 1.0 + nrm(ks[7], (DEPTH, D_MODEL), 0.02),
        "norm2_w": 1.0 + nrm(ks[8], (DEPTH, D_MODEL), 0.02),
        "w_ada": nrm(ks[9], (DEPTH, D_MODEL, 6 * D_MODEL), D_MODEL ** -0.5),
        "b_ada": nrm(ks[10], (DEPTH, 6 * D_MODEL), 0.01),
        "w_in": nrm(ks[11], (DEPTH, D_MODEL, IN_WIDTH), D_MODEL ** -0.5),
        "q_norm_w": 1.0 + nrm(ks[12], (DEPTH, HEAD_DIM), 0.02),
        "k_norm_w": 1.0 + nrm(ks[13], (DEPTH, HEAD_DIM), 0.02),
        "sinks": nrm(ks[14], (DEPTH, H_A), 0.5),
        "w_out": nrm(ks[15], (DEPTH, D_MODEL, D_MODEL), D_MODEL ** -0.5),
        "w_up": nrm(ks[16], (DEPTH, D_MODEL, D_FF), D_MODEL ** -0.5),
        "w_down": nrm(ks[17], (DEPTH, D_FF, D_MODEL), D_FF ** -0.5),
    }


def reference(x_prompt, x_sample, c_prompt, c_sample, state_ret, cache_k_win, cache_v_win,
              norm1_w, norm2_w, w_ada, b_ada, w_in, q_norm_w, k_norm_w, sinks,
              w_out, w_up, w_down):
    pos_p = jnp.arange(x_prompt.shape[1], dtype=jnp.int32)
    pos_s = PAST_LEN + jnp.arange(x_sample.shape[1], dtype=jnp.int32)
    xp, xs = x_prompt, x_sample
    rp, kp, vp, rs, kss, vss = [], [], [], [], [], []
    for l in range(DEPTH):
        wts = (norm1_w[l], norm2_w[l], w_ada[l], b_ada[l], w_in[l], q_norm_w[l],
               k_norm_w[l], sinks[l], w_out[l], w_up[l], w_down[l])
        xp, (r, kw, vw) = _layer(xp, c_prompt, pos_p, _mix_prompt, *wts)
        rp.append(r); kp.append(kw); vp.append(vw)
        mix_s = functools.partial(_mix_sample, s0=state_ret[l],
                                  k_buf=cache_k_win[l], v_buf=cache_v_win[l])
        xs, (r, kw, vw) = _layer(xs, c_sample, pos_s, mix_s, *wts)
        rs.append(r); kss.append(kw); vss.append(vw)
    return (xp, xs, jnp.stack(rp), jnp.stack(kp), jnp.stack(vp),
            jnp.stack(rs), jnp.stack(kss), jnp.stack(vss))
```

```python
import functools
import math

import jax
import jax.numpy as jnp
from jax import lax
from jax.experimental import pallas as pl
from jax.experimental.pallas import tpu as pltpu

F32 = jnp.float32
BF16 = jnp.bfloat16

DK = 256
HD = 128
GROUP = 4
WINDOW = 128
ROPE_BASE = 10000.0
EPS = 1e-6
NEG_INF = -1e30
PAST_LEN = 8192
SAMPLE_BATCHES_PER_STEP = 4
VMEM_LIMIT = 56 * 1024 * 1024


def _cparams(n_axes):
    return pltpu.CompilerParams(dimension_semantics=("arbitrary",) * n_axes,
                                vmem_limit_bytes=VMEM_LIMIT)


def _sigmoid(x):
    return 1.0 / (1.0 + jnp.exp(-x))


def _modnorm(x, lnw, sc, sh):
    y = x * lax.rsqrt(jnp.mean(x * x, axis=-1, keepdims=True) + EPS)
    return (y * lnw) * (1.0 + sc) + sh


def _dot(a, b):
    return jnp.dot(a, b, preferred_element_type=F32)


def _dot_nt(a, b):
    return lax.dot_general(a, b, (((1,), (1,)), ((), ())), preferred_element_type=F32)


def _dot_tn(a, b):
    return lax.dot_general(a, b, (((0,), (0,)), ((), ())), preferred_element_type=F32)


def _ada_kernel(c_ref, w_ref, b_ref, o_ref):
    c = c_ref[...]
    s = (c * _sigmoid(c)).astype(BF16)
    o_ref[...] = _dot(s, w_ref[...].astype(BF16)) + b_ref[...]


def _ada(c_all, w_ada, b_ada):
    depth, d, n6 = w_ada.shape
    rows = c_all.shape[0]
    tn = min(1024, n6)
    return pl.pallas_call(
        _ada_kernel,
        grid=(depth, n6 // tn),
        in_specs=[pl.BlockSpec((rows, d), lambda l, n: (0, 0)),
                  pl.BlockSpec((None, d, tn), lambda l, n: (l, 0, n)),
                  pl.BlockSpec((None, 1, tn), lambda l, n: (l, 0, n))],
        out_specs=pl.BlockSpec((None, rows, tn), lambda l, n: (l, 0, n)),
        out_shape=jax.ShapeDtypeStruct((depth, rows, n6), F32),
        compiler_params=_cparams(2),
        name="ada",
    )(c_all, w_ada, b_ada.reshape(depth, 1, n6))


def _prenorm_kernel(x_ref, lnw_ref, sc_ref, sh_ref, h_ref):
    h_ref[...] = _modnorm(x_ref[...], lnw_ref[...], sc_ref[...], sh_ref[...]).astype(BF16)


def _mod_spec(mod, tm):
    d = mod.shape[-1]
    if mod.shape[1] == 1:
        return pl.BlockSpec((None, 1, d), lambda g, r, *_: (g, 0, 0))
    return pl.BlockSpec((None, tm, d), lambda g, r, *_: (g, r, 0))


def _prenorm(x, lnw, sc, sh, tm):
    g, r, d = x.shape
    return pl.pallas_call(
        _prenorm_kernel,
        grid=(g, r // tm),
        in_specs=[pl.BlockSpec((None, tm, d), lambda i, j: (i, j, 0)),
                  pl.BlockSpec((1, d), lambda i, j: (0, 0)),
                  _mod_spec(sc, tm), _mod_spec(sh, tm)],
        out_specs=pl.BlockSpec((None, tm, d), lambda i, j: (i, j, 0)),
        out_shape=jax.ShapeDtypeStruct((g, r, d), BF16),
        compiler_params=_cparams(2),
        name="prenorm",
    )(x, lnw, sc, sh)


def _inproj_kernel(h_ref, w_ref, cos_ref, sin_ref, qn_ref, kn_ref, o_ref, acc_ref, *, tn):
    n = pl.program_id(0)
    acc_ref[...] = _dot(h_ref[...], w_ref[...])
    half = DK // 2

    @pl.when(n < 4)
    def _rotary():
        scale = jnp.where(n < 2, 1.0, DK ** -0.5).astype(F32)
        cos = cos_ref[...]
        sin = sin_ref[...]
        for j in range(tn // DK):
            t1 = acc_ref[:, j * DK:j * DK + half]
            t2 = acc_ref[:, j * DK + half:(j + 1) * DK]
            o_ref[:, j * DK:j * DK + half] = ((t1 * cos - t2 * sin) * scale).astype(BF16)
            o_ref[:, j * DK + half:(j + 1) * DK] = ((t1 * sin + t2 * cos) * scale).astype(BF16)

    def _rms_cols(lo, hi, w):
        for j in range(lo // HD, hi // HD):
            t = acc_ref[:, j * HD:(j + 1) * HD]
            y = t * lax.rsqrt(jnp.mean(t * t, axis=-1, keepdims=True) + EPS)
            o_ref[:, j * HD:(j + 1) * HD] = (y * w).astype(BF16)

    @pl.when((n == 8) | (n == 9))
    def _qnorm():
        _rms_cols(0, tn, qn_ref[...])

    @pl.when(n == 10)
    def _kv():
        _rms_cols(0, tn // 2, kn_ref[...])
        o_ref[:, tn // 2:] = acc_ref[:, tn // 2:].astype(BF16)

    @pl.when(((n >= 4) & (n < 8)) | (n > 10))
    def _plain():
        o_ref[...] = acc_ref[...].astype(BF16)


def _inproj(h, w, cos, sin, qn, kn, tm):
    m, d = h.shape
    width = w.shape[1]
    tn = d // 2
    nblk = cos.shape[0] // tm
    return pl.pallas_call(
        functools.partial(_inproj_kernel, tn=tn),
        grid=(width // tn, m // tm),
        in_specs=[pl.BlockSpec((tm, d), lambda n, i: (i, 0)),
                  pl.BlockSpec((d, tn), lambda n, i: (0, n)),
                  pl.BlockSpec((tm, HD), lambda n, i: (i % nblk, 0)),
                  pl.BlockSpec((tm, HD), lambda n, i: (i % nblk, 0)),
                  pl.BlockSpec((1, HD), lambda n, i: (0, 0)),
                  pl.BlockSpec((1, HD), lambda n, i: (0, 0))],
        out_specs=pl.BlockSpec((tm, tn), lambda n, i: (i, n)),
        out_shape=jax.ShapeDtypeStruct((m, width), BF16),
        scratch_shapes=[pltpu.VMEM((tm, tn), F32)],
        compiler_params=_cparams(2),
        name="inproj",
    )(h, w, cos, sin, qn, kn)


def _segment_offsets(d):
    kvw = d // 4
    return dict(q=0, k=d, v=2 * d, g=3 * d, qa=4 * d, ka=5 * d, va=5 * d + kvw,
                mr=5 * d + 2 * kvw, ma=6 * d + 2 * kvw, width=7 * d + 2 * kvw)


def _retention_gate(o, g, mr):
    on = o * lax.rsqrt(jnp.mean(o * o, axis=-1, keepdims=True) + EPS)
    gf = g.astype(F32)
    return _sigmoid(mr.astype(F32)) * (on * (gf * _sigmoid(gf)))


def _sink_softmax_pv(s_parts, v_parts, sink_col):
    m = sink_col
    for s in s_parts:
        m = jnp.maximum(m, jnp.max(s, axis=-1, keepdims=True))
    denom = jnp.exp(sink_col - m)
    o = None
    for s, v in zip(s_parts, v_parts):
        p = jnp.exp(s - m)
        denom = denom + jnp.sum(p, axis=-1, keepdims=True)
        pv = _dot(p.astype(BF16), v)
        o = pv if o is None else o + pv
    return o / denom


def _mix_prompt_kernel(sink_ref, sdec_ref, blk_ref, prev_ref, dintra_ref, qdec_ref, kdec_ref,
                       out_ref, s_ref, macc_ref, *, d):
    c = pl.program_id(1)
    off = _segment_offsets(d)
    n_ret = d // DK
    n_kv = d // (HD * GROUP)
    kvw = n_kv * HD
    blk = WINDOW

    @pl.when(c == 0)
    def _init():
        s_ref[...] = jnp.zeros_like(s_ref)

    for h in range(n_ret):
        sl = slice(h * DK, (h + 1) * DK)
        q = blk_ref[:, off["q"] + h * DK:off["q"] + (h + 1) * DK]
        k = blk_ref[:, off["k"] + h * DK:off["k"] + (h + 1) * DK]
        v = blk_ref[:, off["v"] + h * DK:off["v"] + (h + 1) * DK]
        att = _dot_nt(q, k) * dintra_ref[h]
        qd = (q.astype(F32) * qdec_ref[h]).astype(BF16)
        kd = (k.astype(F32) * kdec_ref[h]).astype(BF16)
        state = s_ref[h]
        o = _dot(att.astype(BF16), v) + _dot(qd, state.astype(BF16))
        s_ref[h] = state * sdec_ref[h] + _dot_tn(kd, v)
        macc_ref[:, sl] = _retention_gate(
            o, blk_ref[:, off["g"] + h * DK:off["g"] + (h + 1) * DK],
            blk_ref[:, off["mr"] + h * DK:off["mr"] + (h + 1) * DK])

    rows = GROUP * blk
    qi = lax.broadcasted_iota(jnp.int32, (rows, 2 * blk), 0) & (blk - 1)
    kj = lax.broadcasted_iota(jnp.int32, (rows, 2 * blk), 1)
    diff = kj - qi
    jmin = jnp.where(c == 0, blk, 0)
    mask = (diff >= 0) & (diff <= WINDOW) & (kj >= jmin)
    scale = HD ** -0.5
    for kv in range(n_kv):
        qs = jnp.concatenate(
            [blk_ref[:, off["qa"] + (kv * GROUP + g) * HD:off["qa"] + (kv * GROUP + g + 1) * HD]
             for g in range(GROUP)], axis=0)
        k2 = jnp.concatenate([prev_ref[:, kv * HD:(kv + 1) * HD],
                              blk_ref[:, off["ka"] + kv * HD:off["ka"] + (kv + 1) * HD]], axis=0)
        v2 = jnp.concatenate([prev_ref[:, kvw + kv * HD:kvw + (kv + 1) * HD],
                              blk_ref[:, off["va"] + kv * HD:off["va"] + (kv + 1) * HD]], axis=0)
        s = jnp.where(mask, _dot_nt(qs, k2) * scale, NEG_INF)
        sink_col = jnp.concatenate(
            [jnp.full((blk, 1), sink_ref[kv * GROUP + g], F32) for g in range(GROUP)], axis=0)
        o = _sink_softmax_pv([s], [v2], sink_col)
        for g in range(GROUP):
            head = kv * GROUP + g
            hs = slice(head * HD, (head + 1) * HD)
            ma = blk_ref[:, off["ma"] + head * HD:off["ma"] + (head + 1) * HD].astype(F32)
            out_ref[:, hs] = (macc_ref[:, hs] + _sigmoid(ma) * o[g * blk:(g + 1) * blk]).astype(BF16)


def _mix_prompt(proj, sinks, tabs, b, seq, d):
    off = _segment_offsets(d)
    n_ret = d // DK
    kvw = d // 4
    nblk = seq // WINDOW
    proj3 = proj.reshape(b, seq, off["width"])
    kv_col_block = off["ka"] // (2 * kvw)
    smem = pl.BlockSpec(memory_space=pltpu.SMEM)
    const3 = lambda shape: pl.BlockSpec(shape, lambda i, c: (0, 0, 0))
    merged, state = pl.pallas_call(
        functools.partial(_mix_prompt_kernel, d=d),
        grid=(b, nblk),
        in_specs=[smem, smem,
                  pl.BlockSpec((None, WINDOW, off["width"]), lambda i, c: (i, c, 0)),
                  pl.BlockSpec((None, WINDOW, 2 * kvw),
                               lambda i, c: (i, jnp.maximum(c - 1, 0), kv_col_block)),
                  const3((n_ret, WINDOW, WINDOW)), const3((n_ret, WINDOW, DK)),
                  const3((n_ret, WINDOW, DK))],
        out_specs=[pl.BlockSpec((None, WINDOW, d), lambda i, c: (i, c, 0)),
                   pl.BlockSpec((None, n_ret, DK, DK), lambda i, c: (i, 0, 0, 0))],
        out_shape=[jax.ShapeDtypeStruct((b, seq, d), BF16),
                   jax.ShapeDtypeStruct((b, n_ret, DK, DK), F32)],
        scratch_shapes=[pltpu.VMEM((WINDOW, d), F32)],
        compiler_params=_cparams(2),
        name="mix_prompt",
    )(sinks, tabs["sdec"], proj3, proj3, tabs["dintra"], tabs["qdec"], tabs["kdec"])
    return merged.reshape(b * seq, d), state


def _mix_sample_kernel(sink_ref, sdec_ref, blk_ref, s0_ref, kc_ref, vc_ref, dmask_ref, qdec_ref,
                       kdec_ref, out_ref, s_out_ref, pad_ref, macc_ref, oacc_ref, *, d, dec_seq):
    off = _segment_offsets(d)
    n_ret = d // DK
    n_kv = d // (HD * GROUP)
    kvw = n_kv * HD
    nb = SAMPLE_BATCHES_PER_STEP
    rows = nb * dec_seq
    pad = pad_ref.shape[0]
    p_k, p_v, p_ka, p_va = 0, d, 2 * d, 2 * d + kvw

    @pl.when(pl.program_id(0) == 0)
    def _init():
        pad_ref[...] = jnp.zeros_like(pad_ref)

    pad_ref[0:rows, p_k:p_k + 2 * d] = blk_ref[:, off["k"]:off["k"] + 2 * d]
    pad_ref[0:rows, p_ka:p_ka + 2 * kvw] = blk_ref[:, off["ka"]:off["ka"] + 2 * kvw]

    tok_bits = dec_seq.bit_length() - 1
    row_batch = lax.broadcasted_iota(jnp.int32, (rows, 1), 0) >> tok_bits
    pad_batch = lax.broadcasted_iota(jnp.int32, (pad, 1), 0) >> tok_bits

    for h in range(n_ret):
        sl = slice(h * DK, (h + 1) * DK)
        q = blk_ref[:, off["q"] + h * DK:off["q"] + (h + 1) * DK]
        kpad = pad_ref[:, p_k + h * DK:p_k + (h + 1) * DK]
        vpad = pad_ref[:, p_v + h * DK:p_v + (h + 1) * DK]
        att = _dot_nt(q, kpad) * dmask_ref[h]
        o = _dot(att.astype(BF16), vpad)
        qd = (q.astype(F32) * qdec_ref[h]).astype(BF16)
        kd = (kpad.astype(F32) * kdec_ref[h]).astype(BF16)
        for bi in range(nb):
            state = s0_ref[bi, h]
            o = o + jnp.where(row_batch == bi, _dot(qd, state.astype(BF16)), 0.0)
            kd_b = jnp.where(pad_batch == bi, kd, jnp.zeros_like(kd))
            s_out_ref[bi, h] = state * sdec_ref[h] + _dot_tn(kd_b, vpad)
        macc_ref[:, sl] = _retention_gate(
            o, blk_ref[:, off["g"] + h * DK:off["g"] + (h + 1) * DK],
            blk_ref[:, off["mr"] + h * DK:off["mr"] + (h + 1) * DK])

    qrows = GROUP * rows
    q_tok = lax.broadcasted_iota(jnp.int32, (qrows, 1), 0) & (dec_seq - 1)
    q_batch = (lax.broadcasted_iota(jnp.int32, (qrows, 1), 0) & (rows - 1)) >> tok_bits
    cache_j = lax.broadcasted_iota(jnp.int32, (qrows, WINDOW), 1)
    new_j = lax.broadcasted_iota(jnp.int32, (qrows, pad), 1)
    mask_cache = cache_j >= q_tok
    scale = HD ** -0.5
    oacc_ref[...] = jnp.zeros_like(oacc_ref)
    for kv in range(n_kv):
        qs = jnp.concatenate(
            [blk_ref[:, off["qa"] + (kv * GROUP + g) * HD:off["qa"] + (kv * GROUP + g + 1) * HD]
             for g in range(GROUP)], axis=0)
        kn = pad_ref[:, p_ka + kv * HD:p_ka + (kv + 1) * HD]
        vn = pad_ref[:, p_va + kv * HD:p_va + (kv + 1) * HD]
        s_new = _dot_nt(qs, kn) * scale
        sink_col = jnp.concatenate(
            [jnp.full((rows, 1), sink_ref[kv * GROUP + g], F32) for g in range(GROUP)], axis=0)
        for bi in range(nb):
            kc = kc_ref[bi, :, kv * HD:(kv + 1) * HD].astype(BF16)
            vc = vc_ref[bi, :, kv * HD:(kv + 1) * HD].astype(BF16)
            s_c = jnp.where(mask_cache, _dot_nt(qs, kc) * scale, NEG_INF)
            mask_new = ((new_j >> tok_bits) == bi) & ((new_j & (dec_seq - 1)) <= q_tok)
            s_n = jnp.where(mask_new, s_new, NEG_INF)
            o = _sink_softmax_pv([s_c, s_n], [vc, vn], sink_col)
            oacc_ref[kv] = oacc_ref[kv] + jnp.where(q_batch == bi, o, 0.0)
        for g in range(GROUP):
            head = kv * GROUP + g
            hs = slice(head * HD, (head + 1) * HD)
            ma = blk_ref[:, off["ma"] + head * HD:off["ma"] + (head + 1) * HD].astype(F32)
            out_ref[:, hs] = (macc_ref[:, hs]
                              + _sigmoid(ma) * oacc_ref[kv, g * rows:(g + 1) * rows]).astype(BF16)


def _mix_sample(proj, sinks, tabs, state0, k_cache, v_cache, dec_seq, d):
    off = _segment_offsets(d)
    n_ret = d // DK
    n_kv = d // (HD * GROUP)
    kvw = n_kv * HD
    nb = SAMPLE_BATCHES_PER_STEP
    b = state0.shape[0]
    rows = nb * dec_seq
    smem = pl.BlockSpec(memory_space=pltpu.SMEM)
    const3 = lambda shape: pl.BlockSpec(shape, lambda i: (0, 0, 0))
    merged, state = pl.pallas_call(
        functools.partial(_mix_sample_kernel, d=d, dec_seq=dec_seq),
        grid=(b // nb,),
        in_specs=[smem, smem,
                  pl.BlockSpec((rows, off["width"]), lambda i: (i, 0)),
                  pl.BlockSpec((nb, n_ret, DK, DK), lambda i: (i, 0, 0, 0)),
                  pl.BlockSpec((nb, WINDOW, kvw), lambda i: (i, 0, 0)),
                  pl.BlockSpec((nb, WINDOW, kvw), lambda i: (i, 0, 0)),
                  const3((n_ret, rows, WINDOW)), const3((n_ret, rows, DK)),
                  const3((n_ret, WINDOW, DK))],
        out_specs=[pl.BlockSpec((rows, d), lambda i: (i, 0)),
                   pl.BlockSpec((nb, n_ret, DK, DK), lambda i: (i, 0, 0, 0))],
        out_shape=[jax.ShapeDtypeStruct((b * dec_seq, d), BF16),
                   jax.ShapeDtypeStruct((b, n_ret, DK, DK), F32)],
        scratch_shapes=[pltpu.VMEM((WINDOW, 2 * d + 2 * kvw), BF16),
                        pltpu.VMEM((rows, d), F32),
                        pltpu.VMEM((n_kv, GROUP * rows, HD), F32)],
        compiler_params=_cparams(1),
        name="mix_sample",
    )(sinks, tabs["sdec"], proj, state0, k_cache, v_cache, tabs["dmask"], tabs["qdec"], tabs["kdec"])
    return merged, state


def _outproj_kernel(a_ref, w_ref, x_ref, g_ref, lnw_ref, sc_ref, sh_ref, xo_ref, ho_ref):
    x = x_ref[...] + g_ref[...] * _dot(a_ref[...], w_ref[...])
    xo_ref[...] = x
    ho_ref[...] = _modnorm(x, lnw_ref[...], sc_ref[...], sh_ref[...]).astype(BF16)


def _outproj(a, w, x, gate, lnw, sc, sh, tm):
    g, r, d = x.shape
    tok = lambda: pl.BlockSpec((None, tm, d), lambda i, j: (i, j, 0))
    return pl.pallas_call(
        _outproj_kernel,
        grid=(g, r // tm),
        in_specs=[tok(), pl.BlockSpec((d, d), lambda i, j: (0, 0)), tok(), _mod_spec(gate, tm),
                  pl.BlockSpec((1, d), lambda i, j: (0, 0)), _mod_spec(sc, tm), _mod_spec(sh, tm)],
        out_specs=[tok(), tok()],
        out_shape=[jax.ShapeDtypeStruct((g, r, d), F32), jax.ShapeDtypeStruct((g, r, d), BF16)],
        compiler_params=_cparams(2),
        name="outproj",
    )(a.reshape(g, r, d), w, x, gate, lnw, sc, sh)


def _up_kernel(h_ref, w_ref, o_ref):
    r = jnp.maximum(_dot(h_ref[...], w_ref[...]), 0.0)
    o_ref[...] = (r * r).astype(BF16)


def _up(h, w, tm):
    m, d = h.shape
    f = w.shape[1]
    tn = min(1024, f)
    return pl.pallas_call(
        _up_kernel,
        grid=(f // tn, m // tm),
        in_specs=[pl.BlockSpec((tm, d), lambda n, i: (i, 0)),
                  pl.BlockSpec((d, tn), lambda n, i: (0, n))],
        out_specs=pl.BlockSpec((tm, tn), lambda n, i: (i, n)),
        out_shape=jax.ShapeDtypeStruct((m, f), BF16),
        compiler_params=_cparams(2),
        name="mlp_up",
    )(h, w)


def _down_kernel(u_ref, w_ref, x_ref, g_ref, *rest, with_norm):
    if with_norm:
        lnw_ref, sc_ref, sh_ref, xo_ref, ho_ref = rest
    else:
        (xo_ref,) = rest
    k = pl.program_id(2)

    @pl.when(k == 0)
    def _init():
        xo_ref[...] = jnp.zeros_like(xo_ref)

    xo_ref[...] += _dot(u_ref[...], w_ref[...])

    @pl.when(k == pl.num_programs(2) - 1)
    def _finish():
        x = x_ref[...] + g_ref[...] * xo_ref[...]
        xo_ref[...] = x
        if with_norm:
            ho_ref[...] = _modnorm(x, lnw_ref[...], sc_ref[...], sh_ref[...]).astype(BF16)


def _down(u, w, x, gate, norm, tm):
    g, r, d = x.shape
    f = w.shape[0]
    tk = min(1024, f)
    tok = lambda: pl.BlockSpec((None, tm, d), lambda i, j, k: (i, j, 0))
    in_specs = [pl.BlockSpec((None, tm, tk), lambda i, j, k: (i, j, k)),
                pl.BlockSpec((tk, d), lambda i, j, k: (k, 0)), tok(), _mod_spec(gate, tm)]
    args = [u.reshape(g, r, f), w, x, gate]
    out_specs = [tok()]
    out_shape = [jax.ShapeDtypeStruct((g, r, d), F32)]
    if norm is not None:
        lnw, sc, sh = norm
        in_specs += [pl.BlockSpec((1, d), lambda i, j, k: (0, 0)), _mod_spec(sc, tm), _mod_spec(sh, tm)]
        args += [lnw, sc, sh]
        out_specs.append(tok())
        out_shape.append(jax.ShapeDtypeStruct((g, r, d), BF16))
    out = pl.pallas_call(
        functools.partial(_down_kernel, with_norm=norm is not None),
        grid=(g, r // tm, f // tk),
        in_specs=in_specs, out_specs=out_specs, out_shape=out_shape,
        compiler_params=_cparams(3),
        name="mlp_down",
    )(*args)
    return (out[0], out[1]) if norm is not None else (out[0], None)


def _rope_tables(pos):
    half = DK // 2
    inv = 1.0 / (ROPE_BASE ** (jnp.arange(half, dtype=F32) / half))
    ang = pos.astype(F32)[:, None] * inv[None, :]
    return jnp.cos(ang), jnp.sin(ang)


def _decay_tables(n_ret, chunk, reps, pad_cols):
    log_g = jnp.log1p(-jnp.exp2(-5.0 - jnp.arange(n_ret, dtype=F32)))
    rows = reps * chunk
    r = jnp.arange(rows)
    cidx = jnp.arange(pad_cols)
    idx = (r % chunk).astype(F32)
    diff = idx[:, None] - (cidx % chunk).astype(F32)[None, :]
    ok = (diff >= 0) & ((r // chunk)[:, None] == (cidx // chunk)[None, :]) & (cidx < rows)[None, :]
    dintra = jnp.where(ok, jnp.exp(jnp.where(ok, diff, 0.0) * log_g[:, None, None]), 0.0)
    qdec = jnp.exp((idx + 1.0) * log_g[:, None])[..., None]
    kidx = (jnp.arange(max(rows, pad_cols)) % chunk).astype(F32)
    kdec = jnp.exp((chunk - 1.0 - kidx) * log_g[:, None])[..., None]
    sdec = jnp.exp(chunk * log_g)
    return dict(dintra=dintra, dmask=dintra,
                qdec=jnp.broadcast_to(qdec, (n_ret, rows, DK)),
                kdec=jnp.broadcast_to(kdec, (n_ret, kidx.shape[0], DK)), sdec=sdec)


def kernel(x_prompt, x_sample, c_prompt, c_sample, state_ret, cache_k_win, cache_v_win, norm1_w,
           norm2_w, w_ada, b_ada, w_in, q_norm_w, k_norm_w, sinks, w_out, w_up, w_down):
    bp, seq, d = x_prompt.shape
    bs, dec_seq, _ = x_sample.shape
    depth = w_in.shape[0]
    n_ret = d // DK
    kvw = d // 4
    n_kv = kvw // HD
    win = cache_k_win.shape[2]
    assert win == WINDOW and seq % WINDOW == 0 and bs % SAMPLE_BATCHES_PER_STEP == 0
    assert dec_seq & (dec_seq - 1) == 0

    n_c = bp + bs
    c_rows = -(-n_c // 16) * 16
    c_all = jnp.concatenate([c_prompt, c_sample, jnp.zeros((c_rows - n_c, d), F32)], axis=0)
    mods = _ada(c_all, w_ada, b_ada)

    def group_mods(l):
        parts = jnp.split(mods[l], 6, axis=-1)
        mp = [p[:bp].reshape(bp, 1, d) for p in parts]
        ms = [jnp.repeat(p[bp:n_c], dec_seq, axis=0).reshape(1, bs * dec_seq, d) for p in parts]
        return mp, ms

    wi = w_in.astype(BF16)
    wo = w_out.astype(BF16)
    wu = w_up.astype(BF16)
    wd = w_down.astype(BF16)

    cos_p, sin_p = _rope_tables(jnp.arange(seq, dtype=jnp.int32))
    cos_s, sin_s = _rope_tables(PAST_LEN + jnp.arange(dec_seq, dtype=jnp.int32))
    cos_s = jnp.tile(cos_s, (bs, 1))
    sin_s = jnp.tile(sin_s, (bs, 1))
    tabs_p = _decay_tables(n_ret, WINDOW, 1, WINDOW)
    tabs_s = _decay_tables(n_ret, dec_seq, SAMPLE_BATCHES_PER_STEP, WINDOW)

    tm_p = min(512, seq)
    ms_rows = bs * dec_seq
    xp = x_prompt
    xs = x_sample.reshape(1, ms_rows, d)

    mods_p, mods_s = group_mods(0)
    hp = _prenorm(xp, norm1_w[0:1], mods_p[1], mods_p[0], tm_p)
    hs = _prenorm(xs, norm1_w[0:1], mods_s[1], mods_s[0], ms_rows)

    outs = {k: [] for k in ("rp", "kp", "vp", "rs", "ks", "vs")}
    for l in range(depth):
        qn = q_norm_w[l:l + 1]
        kn = k_norm_w[l:l + 1]
        off = _segment_offsets(d)
        if l + 1 < depth:
            nxt_p, nxt_s = group_mods(l + 1)
            norm_p = (norm1_w[l + 1:l + 2], nxt_p[1], nxt_p[0])
            norm_s = (norm1_w[l + 1:l + 2], nxt_s[1], nxt_s[0])
        else:
            norm_p = norm_s = None

        proj = _inproj(hp.reshape(bp * seq, d), wi[l], cos_p, sin_p, qn, kn, tm_p)
        merged, r_new = _mix_prompt(proj, sinks[l], tabs_p, bp, seq, d)
        proj3 = proj.reshape(bp, seq, off["width"])
        outs["rp"].append(r_new)
        outs["kp"].append(proj3[:, seq - WINDOW:, off["ka"]:off["ka"] + kvw].astype(F32)
                          .reshape(bp, WINDOW, n_kv, HD))
        outs["vp"].append(proj3[:, seq - WINDOW:, off["va"]:off["va"] + kvw].astype(F32)
                          .reshape(bp, WINDOW, n_kv, HD))
        xp, h2 = _outproj(merged, wo[l], xp, mods_p[2], norm2_w[l:l + 1], mods_p[4], mods_p[3], tm_p)
        u = _up(h2.reshape(bp * seq, d), wu[l], tm_p)
        xp, hp = _down(u, wd[l], xp, mods_p[5], norm_p, tm_p)

        proj = _inproj(hs.reshape(ms_rows, d), wi[l], cos_s, sin_s, qn, kn, ms_rows)
        merged, r_new = _mix_sample(proj, sinks[l], tabs_s, state_ret[l],
                                    cache_k_win[l].reshape(bs, win, kvw),
                                    cache_v_win[l].reshape(bs, win, kvw), dec_seq, d)
        proj3 = proj.reshape(bs, dec_seq, off["width"])
        k_new = proj3[:, :, off["ka"]:off["ka"] + kvw].astype(F32).reshape(bs, dec_seq, n_kv, HD)
        v_new = proj3[:, :, off["va"]:off["va"] + kvw].astype(F32).reshape(bs, dec_seq, n_kv, HD)
        outs["rs"].append(r_new)
        outs["ks"].append(jnp.concatenate([cache_k_win[l][:, dec_seq:], k_new], axis=1))
        outs["vs"].append(jnp.concatenate([cache_v_win[l][:, dec_seq:], v_new], axis=1))
        xs, h2 = _outproj(merged, wo[l], xs, mods_s[2], norm2_w[l:l + 1], mods_s[4], mods_s[3], ms_rows)
        u = _up(h2.reshape(ms_rows, d), wu[l], ms_rows)
        xs, hs = _down(u, wd[l], xs, mods_s[5], norm_s, ms_rows)

        if l + 1 < depth:
            mods_p, mods_s = nxt_p, nxt_s

    return (xp, xs.reshape(bs, dec_seq, d), jnp.stack(outs["rp"]), jnp.stack(outs["kp"]),
            jnp.stack(outs["vp"]), jnp.stack(outs["rs"]), jnp.stack(outs["ks"]), jnp.stack(outs["vs"]))
```

```python
import functools
import math

import jax
import jax.numpy as jnp
from jax import lax
from jax.experimental import pallas as pl
from jax.experimental.pallas import tpu as pltpu

F32 = jnp.float32
BF16 = jnp.bfloat16

DK = 256
HD = 128
GROUP = 4
WINDOW = 128
ROPE_BASE = 10000.0
EPS = 1e-6
NEG_INF = -1e30
PAST_LEN = 8192
SAMPLE_BATCHES_PER_STEP = 4
VMEM_LIMIT = 56 * 1024 * 1024


def _cparams(n_axes):
    return pltpu.CompilerParams(dimension_semantics=("arbitrary",) * n_axes,
                                vmem_limit_bytes=VMEM_LIMIT)


def _sigmoid(x):
    return 1.0 / (1.0 + jnp.exp(-x))


def _modnorm(x, lnw, sc, sh):
    y = x * lax.rsqrt(jnp.mean(x * x, axis=-1, keepdims=True) + EPS)
    return (y * lnw) * (1.0 + sc) + sh


def _dot(a, b):
    return jnp.dot(a, b, preferred_element_type=F32)


def _dot_nt(a, b):
    return lax.dot_general(a, b, (((1,), (1,)), ((), ())), preferred_element_type=F32)


def _dot_tn(a, b):
    return lax.dot_general(a, b, (((0,), (0,)), ((), ())), preferred_element_type=F32)


def _ada_kernel(c_ref, w_ref, b_ref, o_ref):
    c = c_ref[...]
    s = (c * _sigmoid(c)).astype(BF16)
    o_ref[...] = _dot(s, w_ref[...].astype(BF16)) + b_ref[...]


def _ada(c_all, w_ada, b_ada):
    depth, d, n6 = w_ada.shape
    rows = c_all.shape[0]
    tn = min(1024, n6)
    return pl.pallas_call(
        _ada_kernel,
        grid=(depth, n6 // tn),
        in_specs=[pl.BlockSpec((rows, d), lambda l, n: (0, 0)),
                  pl.BlockSpec((None, d, tn), lambda l, n: (l, 0, n)),
                  pl.BlockSpec((None, 1, tn), lambda l, n: (l, 0, n))],
        out_specs=pl.BlockSpec((None, rows, tn), lambda l, n: (l, 0, n)),
        out_shape=jax.ShapeDtypeStruct((depth, rows, n6), F32),
        compiler_params=_cparams(2),
        name="ada",
    )(c_all, w_ada, b_ada.reshape(depth, 1, n6))


def _prenorm_kernel(x_ref, lnw_ref, sc_ref, sh_ref, h_ref):
    h_ref[...] = _modnorm(x_ref[...], lnw_ref[...], sc_ref[...], sh_ref[...]).astype(BF16)


def _mod_spec(mod, tm):
    d = mod.shape[-1]
    if mod.shape[1] == 1:
        return pl.BlockSpec((None, 1, d), lambda g, r, *_: (g, 0, 0))
    return pl.BlockSpec((None, tm, d), lambda g, r, *_: (g, r, 0))


def _prenorm(x, lnw, sc, sh, tm):
    g, r, d = x.shape
    return pl.pallas_call(
        _prenorm_kernel,
        grid=(g, r // tm),
        in_specs=[pl.BlockSpec((None, tm, d), lambda i, j: (i, j, 0)),
                  pl.BlockSpec((1, d), lambda i, j: (0, 0)),
                  _mod_spec(sc, tm), _mod_spec(sh, tm)],
        out_specs=pl.BlockSpec((None, tm, d), lambda i, j: (i, j, 0)),
        out_shape=jax.ShapeDtypeStruct((g, r, d), BF16),
        compiler_params=_cparams(2),
        name="prenorm",
    )(x, lnw, sc, sh)


def _rms_heads(a, w):
    outs = []
    for j in range(a.shape[1] // HD):
        t = a[:, j * HD:(j + 1) * HD]
        outs.append(t * lax.rsqrt(jnp.mean(t * t, axis=-1, keepdims=True) + EPS) * w)
    return outs


def _inproj_kernel(h_ref, w_ref, cos_ref, sin_ref, qn_ref, kn_ref, o_ref, *, tn):
    n = pl.program_id(0)
    half = DK // 2

    def slab(j):
        return _dot(h_ref[...], w_ref[:, j * DK:(j + 1) * DK])

    @pl.when(n < 4)
    def _rotary():
        scale = jnp.where(n < 2, 1.0, DK ** -0.5).astype(F32)
        cos = cos_ref[...]
        sin = sin_ref[...]
        for j in range(tn // DK):
            a = slab(j)
            t1 = a[:, :half]
            t2 = a[:, half:]
            o_ref[:, j * DK:j * DK + half] = ((t1 * cos - t2 * sin) * scale).astype(BF16)
            o_ref[:, j * DK + half:(j + 1) * DK] = ((t1 * sin + t2 * cos) * scale).astype(BF16)

    def _rms_slabs(lo, hi, w):
        for j in range(lo, hi):
            for i, y in enumerate(_rms_heads(slab(j), w)):
                o_ref[:, j * DK + i * HD:j * DK + (i + 1) * HD] = y.astype(BF16)

    @pl.when((n == 8) | (n == 9))
    def _qnorm():
        _rms_slabs(0, tn // DK, qn_ref[...])

    @pl.when(n == 10)
    def _kv():
        a = _dot(h_ref[...], w_ref[:, :tn // 2])
        for i, y in enumerate(_rms_heads(a, kn_ref[...])):
            o_ref[:, i * HD:(i + 1) * HD] = y.astype(BF16)
        o_ref[:, tn // 2:] = _dot(h_ref[...], w_ref[:, tn // 2:]).astype(BF16)

    @pl.when(((n >= 4) & (n < 8)) | (n > 10))
    def _plain():
        o_ref[...] = _dot(h_ref[...], w_ref[...]).astype(BF16)


def _inproj(h, w, layer, cos, sin, qn, kn, tm):
    m, d = h.shape
    width = w.shape[2]
    tn = d // 2
    nblk = cos.shape[0] // tm
    return pl.pallas_call(
        functools.partial(_inproj_kernel, tn=tn),
        grid=(width // tn, m // tm),
        in_specs=[pl.BlockSpec((tm, d), lambda n, i: (i, 0)),
                  pl.BlockSpec((None, d, tn), lambda n, i: (layer, 0, n)),
                  pl.BlockSpec((tm, HD), lambda n, i: (i % nblk, 0)),
                  pl.BlockSpec((tm, HD), lambda n, i: (i % nblk, 0)),
                  pl.BlockSpec((1, HD), lambda n, i: (0, 0)),
                  pl.BlockSpec((1, HD), lambda n, i: (0, 0))],
        out_specs=pl.BlockSpec((tm, tn), lambda n, i: (i, n)),
        out_shape=jax.ShapeDtypeStruct((m, width), BF16),
        compiler_params=_cparams(2),
        name="inproj",
    )(h, w, cos, sin, qn, kn)


def _kvproj_kernel(h_ref, w_ref, kn_ref, o_ref):
    kvw = o_ref.shape[1] // 2
    a = _dot(h_ref[...], w_ref[...])
    for i, y in enumerate(_rms_heads(a[:, :kvw], kn_ref[...])):
        o_ref[:, i * HD:(i + 1) * HD] = y
    o_ref[:, kvw:] = a[:, kvw:]


def _kvproj(h, w, layer, kn, tm, n_blocks, row_block):
    d = h.shape[1]
    kvw = d // 4
    col_block = (5 * d) // (2 * kvw)
    return pl.pallas_call(
        _kvproj_kernel,
        grid=(n_blocks,),
        in_specs=[pl.BlockSpec((tm, d), lambda i: (row_block(i), 0)),
                  pl.BlockSpec((None, d, 2 * kvw), lambda i: (layer, 0, col_block)),
                  pl.BlockSpec((1, HD), lambda i: (0, 0))],
        out_specs=pl.BlockSpec((tm, 2 * kvw), lambda i: (i, 0)),
        out_shape=jax.ShapeDtypeStruct((n_blocks * tm, 2 * kvw), F32),
        compiler_params=_cparams(1),
        name="kvproj",
    )(h, w, kn)


def _segment_offsets(d):
    kvw = d // 4
    return dict(q=0, k=d, v=2 * d, g=3 * d, qa=4 * d, ka=5 * d, va=5 * d + kvw,
                mr=5 * d + 2 * kvw, ma=6 * d + 2 * kvw, width=7 * d + 2 * kvw)


def _retention_gate(o, g, mr):
    on = o * lax.rsqrt(jnp.mean(o * o, axis=-1, keepdims=True) + EPS)
    gf = g.astype(F32)
    return _sigmoid(mr.astype(F32)) * (on * (gf * _sigmoid(gf)))


def _sink_softmax_pv(s_parts, v_parts, sink_col):
    m = sink_col
    for s in s_parts:
        m = jnp.maximum(m, jnp.max(s, axis=-1, keepdims=True))
    denom = jnp.exp(sink_col - m)
    o = None
    for s, v in zip(s_parts, v_parts):
        p = jnp.exp(s - m)
        denom = denom + jnp.sum(p, axis=-1, keepdims=True)
        pv = _dot(p.astype(BF16), v)
        o = pv if o is None else o + pv
    return o / denom


def _mix_prompt_kernel(sink_ref, sdec_ref, blk_ref, prev_ref, dintra_ref, qdec_ref, kdec_ref,
                       out_ref, s_ref, macc_ref, *, d):
    c = pl.program_id(1)
    off = _segment_offsets(d)
    n_ret = d // DK
    n_kv = d // (HD * GROUP)
    kvw = n_kv * HD
    blk = WINDOW

    @pl.when(c == 0)
    def _init():
        s_ref[...] = jnp.zeros_like(s_ref)

    for h in range(n_ret):
        sl = slice(h * DK, (h + 1) * DK)
        q = blk_ref[:, off["q"] + h * DK:off["q"] + (h + 1) * DK]
        k = blk_ref[:, off["k"] + h * DK:off["k"] + (h + 1) * DK]
        v = blk_ref[:, off["v"] + h * DK:off["v"] + (h + 1) * DK]
        att = _dot_nt(q, k) * dintra_ref[h]
        qd = (q.astype(F32) * qdec_ref[h]).astype(BF16)
        kd = (k.astype(F32) * kdec_ref[h]).astype(BF16)
        state = s_ref[h]
        o = _dot(att.astype(BF16), v) + _dot(qd, state.astype(BF16))
        s_ref[h] = state * sdec_ref[h] + _dot_tn(kd, v)
        macc_ref[:, sl] = _retention_gate(
            o, blk_ref[:, off["g"] + h * DK:off["g"] + (h + 1) * DK],
            blk_ref[:, off["mr"] + h * DK:off["mr"] + (h + 1) * DK])

    rows = GROUP * blk
    qi = lax.broadcasted_iota(jnp.int32, (rows, 2 * blk), 0) & (blk - 1)
    kj = lax.broadcasted_iota(jnp.int32, (rows, 2 * blk), 1)
    diff = kj - qi
    jmin = jnp.where(c == 0, blk, 0)
    mask = (diff >= 0) & (diff <= WINDOW) & (kj >= jmin)
    scale = HD ** -0.5
    for kv in range(n_kv):
        qs = jnp.concatenate(
            [blk_ref[:, off["qa"] + (kv * GROUP + g) * HD:off["qa"] + (kv * GROUP + g + 1) * HD]
             for g in range(GROUP)], axis=0)
        k2 = jnp.concatenate([prev_ref[:, kv * HD:(kv + 1) * HD],
                              blk_ref[:, off["ka"] + kv * HD:off["ka"] + (kv + 1) * HD]], axis=0)
        v2 = jnp.concatenate([prev_ref[:, kvw + kv * HD:kvw + (kv + 1) * HD],
                              blk_ref[:, off["va"] + kv * HD:off["va"] + (kv + 1) * HD]], axis=0)
        s = jnp.where(mask, _dot_nt(qs, k2) * scale, NEG_INF)
        sink_col = jnp.concatenate(
            [jnp.full((blk, 1), sink_ref[kv * GROUP + g], F32) for g in range(GROUP)], axis=0)
        o = _sink_softmax_pv([s], [v2], sink_col)
        for g in range(GROUP):
            head = kv * GROUP + g
            hs = slice(head * HD, (head + 1) * HD)
            ma = blk_ref[:, off["ma"] + head * HD:off["ma"] + (head + 1) * HD].astype(F32)
            out_ref[:, hs] = (macc_ref[:, hs] + _sigmoid(ma) * o[g * blk:(g + 1) * blk]).astype(BF16)


def _mix_prompt(proj, sinks, tabs, b, seq, d):
    off = _segment_offsets(d)
    n_ret = d // DK
    kvw = d // 4
    nblk = seq // WINDOW
    proj3 = proj.reshape(b, seq, off["width"])
    kv_col_block = off["ka"] // (2 * kvw)
    smem = pl.BlockSpec(memory_space=pltpu.SMEM)
    const3 = lambda shape: pl.BlockSpec(shape, lambda i, c: (0, 0, 0))
    merged, state = pl.pallas_call(
        functools.partial(_mix_prompt_kernel, d=d),
        grid=(b, nblk),
        in_specs=[smem, smem,
                  pl.BlockSpec((None, WINDOW, off["width"]), lambda i, c: (i, c, 0)),
                  pl.BlockSpec((None, WINDOW, 2 * kvw),
                               lambda i, c: (i, jnp.maximum(c - 1, 0), kv_col_block)),
                  const3((n_ret, WINDOW, WINDOW)), const3((n_ret, WINDOW, DK)),
                  const3((n_ret, WINDOW, DK))],
        out_specs=[pl.BlockSpec((None, WINDOW, d), lambda i, c: (i, c, 0)),
                   pl.BlockSpec((None, n_ret, DK, DK), lambda i, c: (i, 0, 0, 0))],
        out_shape=[jax.ShapeDtypeStruct((b, seq, d), BF16),
                   jax.ShapeDtypeStruct((b, n_ret, DK, DK), F32)],
        scratch_shapes=[pltpu.VMEM((WINDOW, d), F32)],
        compiler_params=_cparams(2),
        name="mix_prompt",
    )(sinks, tabs["sdec"], proj3, proj3, tabs["dintra"], tabs["qdec"], tabs["kdec"])
    return merged.reshape(b * seq, d), state


def _mix_sample_kernel(sink_ref, sdec_ref, blk_ref, kvnew_ref, s0_ref, kc_ref, vc_ref, dmask_ref,
                       qdec_ref, kdec_ref, *rest, d, dec_seq, n_alias):
    out_ref, s_out_ref, kout_ref, vout_ref, pad_ref, macc_ref, oacc_ref = rest[n_alias:]
    off = _segment_offsets(d)
    n_ret = d // DK
    n_kv = d // (HD * GROUP)
    kvw = n_kv * HD
    nb = SAMPLE_BATCHES_PER_STEP
    rows = nb * dec_seq
    pad = pad_ref.shape[0]
    p_k, p_v, p_ka, p_va = 0, d, 2 * d, 2 * d + kvw

    @pl.when(pl.program_id(0) == 0)
    def _init():
        pad_ref[...] = jnp.zeros_like(pad_ref)

    pad_ref[0:rows, p_k:p_k + 2 * d] = blk_ref[:, off["k"]:off["k"] + 2 * d]
    pad_ref[0:rows, p_ka:p_ka + 2 * kvw] = blk_ref[:, off["ka"]:off["ka"] + 2 * kvw]

    tok_bits = dec_seq.bit_length() - 1
    row_batch = lax.broadcasted_iota(jnp.int32, (rows, 1), 0) >> tok_bits
    pad_batch = lax.broadcasted_iota(jnp.int32, (pad, 1), 0) >> tok_bits

    for h in range(n_ret):
        sl = slice(h * DK, (h + 1) * DK)
        q = blk_ref[:, off["q"] + h * DK:off["q"] + (h + 1) * DK]
        kpad = pad_ref[:, p_k + h * DK:p_k + (h + 1) * DK]
        vpad = pad_ref[:, p_v + h * DK:p_v + (h + 1) * DK]
        att = _dot_nt(q, kpad) * dmask_ref[h]
        o = _dot(att.astype(BF16), vpad)
        qd = (q.astype(F32) * qdec_ref[h]).astype(BF16)
        kd = (kpad.astype(F32) * kdec_ref[h]).astype(BF16)
        for bi in range(nb):
            state = s0_ref[bi, h]
            o = o + jnp.where(row_batch == bi, _dot(qd, state.astype(BF16)), 0.0)
            kd_b = jnp.where(pad_batch == bi, kd, jnp.zeros_like(kd))
            s_out_ref[bi, h] = state * sdec_ref[h] + _dot_tn(kd_b, vpad)
        macc_ref[:, sl] = _retention_gate(
            o, blk_ref[:, off["g"] + h * DK:off["g"] + (h + 1) * DK],
            blk_ref[:, off["mr"] + h * DK:off["mr"] + (h + 1) * DK])

    qrows = GROUP * rows
    q_tok = lax.broadcasted_iota(jnp.int32, (qrows, 1), 0) & (dec_seq - 1)
    q_batch = (lax.broadcasted_iota(jnp.int32, (qrows, 1), 0) & (rows - 1)) >> tok_bits
    cache_j = lax.broadcasted_iota(jnp.int32, (qrows, WINDOW), 1)
    new_j = lax.broadcasted_iota(jnp.int32, (qrows, pad), 1)
    mask_cache = cache_j >= q_tok
    scale = HD ** -0.5
    oacc_ref[...] = jnp.zeros_like(oacc_ref)
    for kv in range(n_kv):
        qs = jnp.concatenate(
            [blk_ref[:, off["qa"] + (kv * GROUP + g) * HD:off["qa"] + (kv * GROUP + g + 1) * HD]
             for g in range(GROUP)], axis=0)
        kn = pad_ref[:, p_ka + kv * HD:p_ka + (kv + 1) * HD]
        vn = pad_ref[:, p_va + kv * HD:p_va + (kv + 1) * HD]
        s_new = _dot_nt(qs, kn) * scale
        sink_col = jnp.concatenate(
            [jnp.full((rows, 1), sink_ref[kv * GROUP + g], F32) for g in range(GROUP)], axis=0)
        for bi in range(nb):
            kc = kc_ref[bi, :, kv, :].astype(BF16)
            vc = vc_ref[bi, :, kv, :].astype(BF16)
            s_c = jnp.where(mask_cache, _dot_nt(qs, kc) * scale, NEG_INF)
            mask_new = ((new_j >> tok_bits) == bi) & ((new_j & (dec_seq - 1)) <= q_tok)
            s_n = jnp.where(mask_new, s_new, NEG_INF)
            o = _sink_softmax_pv([s_c, s_n], [vc, vn], sink_col)
            oacc_ref[kv] = oacc_ref[kv] + jnp.where(q_batch == bi, o, 0.0)
        for g in range(GROUP):
            head = kv * GROUP + g
            hs = slice(head * HD, (head + 1) * HD)
            ma = blk_ref[:, off["ma"] + head * HD:off["ma"] + (head + 1) * HD].astype(F32)
            out_ref[:, hs] = (macc_ref[:, hs]
                              + _sigmoid(ma) * oacc_ref[kv, g * rows:(g + 1) * rows]).astype(BF16)

    keep = WINDOW - dec_seq
    for bi in range(nb):
        kout_ref[bi, 0:keep] = kc_ref[bi, dec_seq:WINDOW]
        vout_ref[bi, 0:keep] = vc_ref[bi, dec_seq:WINDOW]
        for t in range(dec_seq):
            r = bi * dec_seq + t
            for kv in range(n_kv):
                kout_ref[bi, keep + t, kv:kv + 1, :] = kvnew_ref[r:r + 1, kv * HD:(kv + 1) * HD]
                vout_ref[bi, keep + t, kv:kv + 1, :] = kvnew_ref[r:r + 1, kvw + kv * HD:kvw + (kv + 1) * HD]


def _mix_sample(proj, kvnew, sinks, tabs, layer, state_all, k_all, v_all, prev_outs, dec_seq, d):
    off = _segment_offsets(d)
    n_ret = d // DK
    n_kv = d // (HD * GROUP)
    kvw = n_kv * HD
    nb = SAMPLE_BATCHES_PER_STEP
    b = state_all.shape[1]
    rows = nb * dec_seq
    smem = pl.BlockSpec(memory_space=pltpu.SMEM)
    const3 = lambda shape: pl.BlockSpec(shape, lambda i: (0, 0, 0))
    state_spec = lambda: pl.BlockSpec((None, nb, n_ret, DK, DK), lambda i: (layer, i, 0, 0, 0))
    win_spec = lambda: pl.BlockSpec((None, nb, WINDOW, n_kv, HD), lambda i: (layer, i, 0, 0, 0))
    in_specs = [smem, smem,
                pl.BlockSpec((rows, off["width"]), lambda i: (i, 0)),
                pl.BlockSpec((rows, 2 * kvw), lambda i: (i, 0)),
                state_spec(), win_spec(), win_spec(),
                const3((n_ret, rows, WINDOW)), const3((n_ret, rows, DK)), const3((n_ret, WINDOW, DK))]
    args = [sinks, tabs["sdec"], proj, kvnew, state_all, k_all, v_all,
            tabs["dmask"], tabs["qdec"], tabs["kdec"]]
    aliases = {}
    if prev_outs is not None:
        for j, arr in enumerate(prev_outs):
            aliases[len(args)] = 1 + j
            in_specs.append(pl.BlockSpec(memory_space=pl.ANY))
            args.append(arr)
    n_alias = len(aliases)
    merged, state, kwin, vwin = pl.pallas_call(
        functools.partial(_mix_sample_kernel, d=d, dec_seq=dec_seq, n_alias=n_alias),
        grid=(b // nb,),
        in_specs=in_specs,
        out_specs=[pl.BlockSpec((rows, d), lambda i: (i, 0)), state_spec(), win_spec(), win_spec()],
        out_shape=[jax.ShapeDtypeStruct((b * dec_seq, d), BF16),
                   jax.ShapeDtypeStruct(state_all.shape, F32),
                   jax.ShapeDtypeStruct(k_all.shape, F32),
                   jax.ShapeDtypeStruct(v_all.shape, F32)],
        scratch_shapes=[pltpu.VMEM((WINDOW, 2 * d + 2 * kvw), BF16),
                        pltpu.VMEM((rows, d), F32),
                        pltpu.VMEM((n_kv, GROUP * rows, HD), F32)],
        input_output_aliases=aliases,
        compiler_params=_cparams(1),
        name="mix_sample",
    )(*args)
    return merged, (state, kwin, vwin)


def _outproj_kernel(a_ref, w_ref, x_ref, g_ref, lnw_ref, sc_ref, sh_ref, xo_ref, ho_ref):
    x = x_ref[...] + g_ref[...] * _dot(a_ref[...], w_ref[...])
    xo_ref[...] = x
    ho_ref[...] = _modnorm(x, lnw_ref[...], sc_ref[...], sh_ref[...]).astype(BF16)


def _outproj(a, w, layer, x, gate, lnw, sc, sh, tm):
    g, r, d = x.shape
    tok = lambda: pl.BlockSpec((None, tm, d), lambda i, j: (i, j, 0))
    return pl.pallas_call(
        _outproj_kernel,
        grid=(g, r // tm),
        in_specs=[tok(), pl.BlockSpec((None, d, d), lambda i, j: (layer, 0, 0)), tok(), _mod_spec(gate, tm),
                  pl.BlockSpec((1, d), lambda i, j: (0, 0)), _mod_spec(sc, tm), _mod_spec(sh, tm)],
        out_specs=[tok(), tok()],
        out_shape=[jax.ShapeDtypeStruct((g, r, d), F32), jax.ShapeDtypeStruct((g, r, d), BF16)],
        compiler_params=_cparams(2),
        name="outproj",
    )(a.reshape(g, r, d), w, x, gate, lnw, sc, sh)


def _up_kernel(h_ref, w_ref, o_ref):
    r = jnp.maximum(_dot(h_ref[...], w_ref[...]), 0.0)
    o_ref[...] = (r * r).astype(BF16)


def _up(h, w, layer, tm):
    m, d = h.shape
    f = w.shape[2]
    tn = min(1024, f)
    return pl.pallas_call(
        _up_kernel,
        grid=(f // tn, m // tm),
        in_specs=[pl.BlockSpec((tm, d), lambda n, i: (i, 0)),
                  pl.BlockSpec((None, d, tn), lambda n, i: (layer, 0, n))],
        out_specs=pl.BlockSpec((tm, tn), lambda n, i: (i, n)),
        out_shape=jax.ShapeDtypeStruct((m, f), BF16),
        compiler_params=_cparams(2),
        name="mlp_up",
    )(h, w)


def _down_kernel(u_ref, w_ref, x_ref, g_ref, *rest, with_norm):
    if with_norm:
        lnw_ref, sc_ref, sh_ref, xo_ref, ho_ref = rest
    else:
        (xo_ref,) = rest
    k = pl.program_id(2)

    @pl.when(k == 0)
    def _init():
        xo_ref[...] = jnp.zeros_like(xo_ref)

    xo_ref[...] += _dot(u_ref[...], w_ref[...])

    @pl.when(k == pl.num_programs(2) - 1)
    def _finish():
        x = x_ref[...] + g_ref[...] * xo_ref[...]
        xo_ref[...] = x
        if with_norm:
            ho_ref[...] = _modnorm(x, lnw_ref[...], sc_ref[...], sh_ref[...]).astype(BF16)


def _down(u, w, layer, x, gate, norm, tm):
    g, r, d = x.shape
    f = w.shape[1]
    tk = min(512, f)
    tok = lambda: pl.BlockSpec((None, tm, d), lambda i, j, k: (i, j, 0))
    in_specs = [pl.BlockSpec((None, tm, tk), lambda i, j, k: (i, j, k)),
                pl.BlockSpec((None, tk, d), lambda i, j, k: (layer, k, 0)), tok(), _mod_spec(gate, tm)]
    args = [u.reshape(g, r, f), w, x, gate]
    out_specs = [tok()]
    out_shape = [jax.ShapeDtypeStruct((g, r, d), F32)]
    if norm is not None:
        lnw, sc, sh = norm
        in_specs += [pl.BlockSpec((1, d), lambda i, j, k: (0, 0)), _mod_spec(sc, tm), _mod_spec(sh, tm)]
        args += [lnw, sc, sh]
        out_specs.append(tok())
        out_shape.append(jax.ShapeDtypeStruct((g, r, d), BF16))
    out = pl.pallas_call(
        functools.partial(_down_kernel, with_norm=norm is not None),
        grid=(g, r // tm, f // tk),
        in_specs=in_specs, out_specs=out_specs, out_shape=out_shape,
        compiler_params=_cparams(3),
        name="mlp_down",
    )(*args)
    return (out[0], out[1]) if norm is not None else (out[0], None)


def _rope_tables(pos):
    half = DK // 2
    inv = 1.0 / (ROPE_BASE ** (jnp.arange(half, dtype=F32) / half))
    ang = pos.astype(F32)[:, None] * inv[None, :]
    return jnp.cos(ang), jnp.sin(ang)


def _decay_tables(n_ret, chunk, reps, pad_cols):
    log_g = jnp.log1p(-jnp.exp2(-5.0 - jnp.arange(n_ret, dtype=F32)))
    rows = reps * chunk
    r = jnp.arange(rows)
    cidx = jnp.arange(pad_cols)
    idx = (r % chunk).astype(F32)
    diff = idx[:, None] - (cidx % chunk).astype(F32)[None, :]
    ok = (diff >= 0) & ((r // chunk)[:, None] == (cidx // chunk)[None, :]) & (cidx < rows)[None, :]
    dintra = jnp.where(ok, jnp.exp(jnp.where(ok, diff, 0.0) * log_g[:, None, None]), 0.0)
    qdec = jnp.exp((idx + 1.0) * log_g[:, None])[..., None]
    kidx = (jnp.arange(max(rows, pad_cols)) % chunk).astype(F32)
    kdec = jnp.exp((chunk - 1.0 - kidx) * log_g[:, None])[..., None]
    sdec = jnp.exp(chunk * log_g)
    return dict(dintra=dintra, dmask=dintra,
                qdec=jnp.broadcast_to(qdec, (n_ret, rows, DK)),
                kdec=jnp.broadcast_to(kdec, (n_ret, kidx.shape[0], DK)), sdec=sdec)


def kernel(x_prompt, x_sample, c_prompt, c_sample, state_ret, cache_k_win, cache_v_win, norm1_w,
           norm2_w, w_ada, b_ada, w_in, q_norm_w, k_norm_w, sinks, w_out, w_up, w_down):
    bp, seq, d = x_prompt.shape
    bs, dec_seq, _ = x_sample.shape
    depth = w_in.shape[0]
    n_ret = d // DK
    kvw = d // 4
    n_kv = kvw // HD
    win = cache_k_win.shape[2]
    assert win == WINDOW and seq % WINDOW == 0 and bs % SAMPLE_BATCHES_PER_STEP == 0
    assert dec_seq & (dec_seq - 1) == 0

    n_c = bp + bs
    c_rows = -(-n_c // 16) * 16
    c_all = jnp.concatenate([c_prompt, c_sample, jnp.zeros((c_rows - n_c, d), F32)], axis=0)
    mods = _ada(c_all, w_ada, b_ada)

    def group_mods(l):
        parts = jnp.split(mods[l], 6, axis=-1)
        mp = [p[:bp].reshape(bp, 1, d) for p in parts]
        ms = [jnp.repeat(p[bp:n_c], dec_seq, axis=0).reshape(1, bs * dec_seq, d) for p in parts]
        return mp, ms

    wi = w_in.astype(BF16)
    wo = w_out.astype(BF16)
    wu = w_up.astype(BF16)
    wd = w_down.astype(BF16)

    cos_p, sin_p = _rope_tables(jnp.arange(seq, dtype=jnp.int32))
    cos_s, sin_s = _rope_tables(PAST_LEN + jnp.arange(dec_seq, dtype=jnp.int32))
    cos_s = jnp.tile(cos_s, (bs, 1))
    sin_s = jnp.tile(sin_s, (bs, 1))
    tabs_p = _decay_tables(n_ret, WINDOW, 1, WINDOW)
    tabs_s = _decay_tables(n_ret, dec_seq, SAMPLE_BATCHES_PER_STEP, WINDOW)

    tm_p = min(512, seq)
    tm_down = min(1024, seq)
    ms_rows = bs * dec_seq
    nblk = seq // WINDOW
    xp = x_prompt
    xs = x_sample.reshape(1, ms_rows, d)

    mods_p, mods_s = group_mods(0)
    hp = _prenorm(xp, norm1_w[0:1], mods_p[1], mods_p[0], tm_p)
    hs = _prenorm(xs, norm1_w[0:1], mods_s[1], mods_s[0], ms_rows)

    outs = {k: [] for k in ("rp", "kp", "vp")}
    sample_state = None
    for l in range(depth):
        qn = q_norm_w[l:l + 1]
        kn = k_norm_w[l:l + 1]
        if l + 1 < depth:
            nxt_p, nxt_s = group_mods(l + 1)
            norm_p = (norm1_w[l + 1:l + 2], nxt_p[1], nxt_p[0])
            norm_s = (norm1_w[l + 1:l + 2], nxt_s[1], nxt_s[0])
        else:
            norm_p = norm_s = None

        hp2 = hp.reshape(bp * seq, d)
        proj = _inproj(hp2, wi, l, cos_p, sin_p, qn, kn, tm_p)
        kv_tail = _kvproj(hp2, wi, l, kn, WINDOW, bp, lambda i: i * nblk + nblk - 1)
        merged, r_new = _mix_prompt(proj, sinks[l], tabs_p, bp, seq, d)
        outs["rp"].append(r_new)
        outs["kp"].append(kv_tail[:, :kvw].reshape(bp, WINDOW, n_kv, HD))
        outs["vp"].append(kv_tail[:, kvw:].reshape(bp, WINDOW, n_kv, HD))
        xp, h2 = _outproj(merged, wo, l, xp, mods_p[2], norm2_w[l:l + 1], mods_p[4], mods_p[3], tm_p)
        u = _up(h2.reshape(bp * seq, d), wu, l, tm_p)
        xp, hp = _down(u, wd, l, xp, mods_p[5], norm_p, tm_down)

        hs2 = hs.reshape(ms_rows, d)
        proj = _inproj(hs2, wi, l, cos_s, sin_s, qn, kn, ms_rows)
        kv_new = _kvproj(hs2, wi, l, kn, ms_rows, 1, lambda i: i)
        merged, sample_state = _mix_sample(proj, kv_new, sinks[l], tabs_s, l, state_ret, cache_k_win,
                                           cache_v_win, sample_state, dec_seq, d)
        xs, h2 = _outproj(merged, wo, l, xs, mods_s[2], norm2_w[l:l + 1], mods_s[4], mods_s[3], ms_rows)
        u = _up(h2.reshape(ms_rows, d), wu, l, ms_rows)
        xs, hs = _down(u, wd, l, xs, mods_s[5], norm_s, ms_rows)

        if l + 1 < depth:
            mods_p, mods_s = nxt_p, nxt_s

    return (xp, xs.reshape(bs, dec_seq, d), jnp.stack(outs["rp"]), jnp.stack(outs["kp"]),
            jnp.stack(outs["vp"])) + tuple(sample_state)
```

```python
import functools
import math

import jax
import jax.numpy as jnp
from jax import lax
from jax.experimental import pallas as pl
from jax.experimental.pallas import tpu as pltpu

F32 = jnp.float32
BF16 = jnp.bfloat16

DK = 256
HD = 128
GROUP = 4
WINDOW = 128
ROPE_BASE = 10000.0
EPS = 1e-6
NEG_INF = -1e30
PAST_LEN = 8192
SAMPLE_BATCHES_PER_STEP = 4
VMEM_LIMIT = 56 * 1024 * 1024


def _cparams(n_axes):
    return pltpu.CompilerParams(dimension_semantics=("arbitrary",) * n_axes,
                                vmem_limit_bytes=VMEM_LIMIT)


def _sigmoid(x):
    return 1.0 / (1.0 + jnp.exp(-x))


def _modnorm(x, lnw, sc, sh):
    y = x * lax.rsqrt(jnp.mean(x * x, axis=-1, keepdims=True) + EPS)
    return (y * lnw) * (1.0 + sc) + sh


def _dot(a, b):
    return jnp.dot(a, b, preferred_element_type=F32)


def _dot_nt(a, b):
    return lax.dot_general(a, b, (((1,), (1,)), ((), ())), preferred_element_type=F32)


def _dot_tn(a, b):
    return lax.dot_general(a, b, (((0,), (0,)), ((), ())), preferred_element_type=F32)


def _ada_kernel(c_ref, w_ref, b_ref, o_ref):
    c = c_ref[...]
    s = (c * _sigmoid(c)).astype(BF16)
    o_ref[...] = _dot(s, w_ref[...].astype(BF16)) + b_ref[...]


def _ada(c_all, w_ada, b_ada):
    depth, d, n6 = w_ada.shape
    rows = c_all.shape[0]
    tn = min(1024, n6)
    return pl.pallas_call(
        _ada_kernel,
        grid=(depth, n6 // tn),
        in_specs=[pl.BlockSpec((rows, d), lambda l, n: (0, 0)),
                  pl.BlockSpec((None, d, tn), lambda l, n: (l, 0, n)),
                  pl.BlockSpec((None, 1, tn), lambda l, n: (l, 0, n))],
        out_specs=pl.BlockSpec((None, rows, tn), lambda l, n: (l, 0, n)),
        out_shape=jax.ShapeDtypeStruct((depth, rows, n6), F32),
        compiler_params=_cparams(2),
        name="ada",
    )(c_all, w_ada, b_ada.reshape(depth, 1, n6))


def _prenorm_kernel(x_ref, lnw_ref, sc_ref, sh_ref, h_ref):
    h_ref[...] = _modnorm(x_ref[...], lnw_ref[...], sc_ref[...], sh_ref[...]).astype(BF16)


def _mod_spec(mod, tm):
    d = mod.shape[-1]
    if mod.shape[1] == 1:
        return pl.BlockSpec((None, 1, d), lambda g, r, *_: (g, 0, 0))
    return pl.BlockSpec((None, tm, d), lambda g, r, *_: (g, r, 0))


def _prenorm(x, lnw, sc, sh, tm):
    g, r, d = x.shape
    return pl.pallas_call(
        _prenorm_kernel,
        grid=(g, r // tm),
        in_specs=[pl.BlockSpec((None, tm, d), lambda i, j: (i, j, 0)),
                  pl.BlockSpec((1, d), lambda i, j: (0, 0)),
                  _mod_spec(sc, tm), _mod_spec(sh, tm)],
        out_specs=pl.BlockSpec((None, tm, d), lambda i, j: (i, j, 0)),
        out_shape=jax.ShapeDtypeStruct((g, r, d), BF16),
        compiler_params=_cparams(2),
        name="prenorm",
    )(x, lnw, sc, sh)


def _rms_heads(a, w):
    outs = []
    for j in range(a.shape[1] // HD):
        t = a[:, j * HD:(j + 1) * HD]
        outs.append(t * lax.rsqrt(jnp.mean(t * t, axis=-1, keepdims=True) + EPS) * w)
    return outs


def _inproj_kernel(h_ref, w_ref, cos_ref, sin_ref, qn_ref, kn_ref, o_ref, *, tn):
    n = pl.program_id(0)
    half = DK // 2

    def slab(j):
        return _dot(h_ref[...], w_ref[:, j * DK:(j + 1) * DK])

    @pl.when(n < 4)
    def _rotary():
        scale = jnp.where(n < 2, 1.0, DK ** -0.5).astype(F32)
        cos = cos_ref[...]
        sin = sin_ref[...]
        for j in range(tn // DK):
            a = slab(j)
            t1 = a[:, :half]
            t2 = a[:, half:]
            o_ref[:, j * DK:j * DK + half] = ((t1 * cos - t2 * sin) * scale).astype(BF16)
            o_ref[:, j * DK + half:(j + 1) * DK] = ((t1 * sin + t2 * cos) * scale).astype(BF16)

    def _rms_slabs(lo, hi, w):
        for j in range(lo, hi):
            for i, y in enumerate(_rms_heads(slab(j), w)):
                o_ref[:, j * DK + i * HD:j * DK + (i + 1) * HD] = y.astype(BF16)

    @pl.when((n == 8) | (n == 9))
    def _qnorm():
        _rms_slabs(0, tn // DK, qn_ref[...])

    @pl.when(n == 10)
    def _kv():
        a = _dot(h_ref[...], w_ref[:, :tn // 2])
        for i, y in enumerate(_rms_heads(a, kn_ref[...])):
            o_ref[:, i * HD:(i + 1) * HD] = y.astype(BF16)
        o_ref[:, tn // 2:] = _dot(h_ref[...], w_ref[:, tn // 2:]).astype(BF16)

    @pl.when((n == 4) | (n == 5))
    def _plain():
        o_ref[...] = _dot(h_ref[...], w_ref[...]).astype(BF16)

    @pl.when((n == 6) | (n == 7))
    def _silu_gate():
        for j in range(tn // DK):
            a = slab(j)
            o_ref[:, j * DK:(j + 1) * DK] = (a * _sigmoid(a)).astype(BF16)

    @pl.when(n > 10)
    def _sigmoid_gate():
        for j in range(tn // DK):
            o_ref[:, j * DK:(j + 1) * DK] = _sigmoid(slab(j)).astype(BF16)


def _inproj(h, w, layer, cos, sin, qn, kn, tm):
    m, d = h.shape
    width = w.shape[2]
    tn = d // 2
    nblk = cos.shape[0] // tm
    return pl.pallas_call(
        functools.partial(_inproj_kernel, tn=tn),
        grid=(width // tn, m // tm),
        in_specs=[pl.BlockSpec((tm, d), lambda n, i: (i, 0)),
                  pl.BlockSpec((None, d, tn), lambda n, i: (layer, 0, n)),
                  pl.BlockSpec((tm, HD), lambda n, i: (i % nblk, 0)),
                  pl.BlockSpec((tm, HD), lambda n, i: (i % nblk, 0)),
                  pl.BlockSpec((1, HD), lambda n, i: (0, 0)),
                  pl.BlockSpec((1, HD), lambda n, i: (0, 0))],
        out_specs=pl.BlockSpec((tm, tn), lambda n, i: (i, n)),
        out_shape=jax.ShapeDtypeStruct((m, width), BF16),
        compiler_params=_cparams(2),
        name="inproj",
    )(h, w, cos, sin, qn, kn)


def _kvproj_kernel(h_ref, w_ref, kn_ref, o_ref):
    kvw = o_ref.shape[1] // 2
    a = _dot(h_ref[...], w_ref[...])
    for i, y in enumerate(_rms_heads(a[:, :kvw], kn_ref[...])):
        o_ref[:, i * HD:(i + 1) * HD] = y
    o_ref[:, kvw:] = a[:, kvw:]


def _kvproj(h, w, layer, kn, tm, n_blocks, row_block):
    d = h.shape[1]
    kvw = d // 4
    col_block = (5 * d) // (2 * kvw)
    return pl.pallas_call(
        _kvproj_kernel,
        grid=(n_blocks,),
        in_specs=[pl.BlockSpec((tm, d), lambda i: (row_block(i), 0)),
                  pl.BlockSpec((None, d, 2 * kvw), lambda i: (layer, 0, col_block)),
                  pl.BlockSpec((1, HD), lambda i: (0, 0))],
        out_specs=pl.BlockSpec((tm, 2 * kvw), lambda i: (i, 0)),
        out_shape=jax.ShapeDtypeStruct((n_blocks * tm, 2 * kvw), F32),
        compiler_params=_cparams(1),
        name="kvproj",
    )(h, w, kn)


def _segment_offsets(d):
    kvw = d // 4
    return dict(q=0, k=d, v=2 * d, g=3 * d, qa=4 * d, ka=5 * d, va=5 * d + kvw,
                mr=5 * d + 2 * kvw, ma=6 * d + 2 * kvw, width=7 * d + 2 * kvw)


def _retention_gate(o, g_act, mr_act):
    on = o * lax.rsqrt(jnp.mean(o * o, axis=-1, keepdims=True) + EPS)
    return mr_act.astype(F32) * (on * g_act.astype(F32))


def _mix_prompt_kernel(sink_ref, sdec_ref, blk_ref, prev_ref, dintra_ref, qdec_ref, kdec_ref,
                       out_ref, s_ref, macc_ref, *, d):
    c = pl.program_id(1)
    off = _segment_offsets(d)
    n_ret = d // DK
    n_kv = d // (HD * GROUP)
    kvw = n_kv * HD
    blk = WINDOW

    @pl.when(c == 0)
    def _init():
        s_ref[...] = jnp.zeros_like(s_ref)

    def seg(name, h, width=DK):
        return blk_ref[:, off[name] + h * width:off[name] + (h + 1) * width]

    heads = range(n_ret)
    att = [_dot_nt(seg("q", h), seg("k", h)) for h in heads]
    inter = [_dot(seg("q", h), s_ref[h].astype(BF16)) for h in heads]
    upd = [_dot_tn((seg("k", h).astype(F32) * kdec_ref[h]).astype(BF16), seg("v", h)) for h in heads]
    for h in heads:
        s_ref[h] = s_ref[h] * sdec_ref[h] + upd[h]
    att = [(att[h] * dintra_ref[h]).astype(BF16) for h in heads]
    o_ret = [_dot(att[h], seg("v", h)) + inter[h] * qdec_ref[h] for h in heads]
    for h in heads:
        macc_ref[:, h * DK:(h + 1) * DK] = _retention_gate(o_ret[h], seg("g", h), seg("mr", h))

    qi = lax.broadcasted_iota(jnp.int32, (blk, 2 * blk), 0)
    kj = lax.broadcasted_iota(jnp.int32, (blk, 2 * blk), 1)
    diff = kj - qi
    jmin = jnp.where(c == 0, blk, 0)
    mask = (diff >= 0) & (diff <= WINDOW) & (kj >= jmin)
    scale = HD ** -0.5
    kvs = range(n_kv)
    ones = jnp.ones((2 * blk, HD), BF16)
    scores, values = [], []
    for kv in kvs:
        qs = jnp.concatenate([seg("qa", kv * GROUP + g, HD) for g in range(GROUP)], axis=0)
        k2 = jnp.concatenate([prev_ref[:, kv * HD:(kv + 1) * HD], seg("ka", kv, HD)], axis=0)
        v2 = jnp.concatenate([prev_ref[:, kvw + kv * HD:kvw + (kv + 1) * HD], seg("va", kv, HD)], axis=0)
        scores.append(_dot_nt(qs, k2))
        values.append(jnp.concatenate([v2, ones], axis=1))
    probs, sink_terms = [], []
    for kv in kvs:
        p_rows = []
        for g in range(GROUP):
            sink = sink_ref[kv * GROUP + g]
            s = jnp.where(mask, scores[kv][g * blk:(g + 1) * blk] * scale, NEG_INF)
            m = jnp.maximum(jnp.max(s, axis=-1, keepdims=True), sink)
            p_rows.append(jnp.exp(s - m).astype(BF16))
            sink_terms.append(jnp.exp(sink - m))
        probs.append(jnp.concatenate(p_rows, axis=0))
    pv = [_dot(probs[kv], values[kv]) for kv in kvs]
    for kv in kvs:
        for g in range(GROUP):
            head = kv * GROUP + g
            hs = slice(head * HD, (head + 1) * HD)
            o = pv[kv][g * blk:(g + 1) * blk, :HD]
            denom = pv[kv][g * blk:(g + 1) * blk, HD:] + sink_terms[head]
            out_ref[:, hs] = (macc_ref[:, hs] + seg("ma", head, HD).astype(F32) * (o / denom)).astype(BF16)


def _mix_prompt(proj, sinks, tabs, b, seq, d):
    off = _segment_offsets(d)
    n_ret = d // DK
    kvw = d // 4
    nblk = seq // WINDOW
    proj3 = proj.reshape(b, seq, off["width"])
    kv_col_block = off["ka"] // (2 * kvw)
    smem = pl.BlockSpec(memory_space=pltpu.SMEM)
    const3 = lambda shape: pl.BlockSpec(shape, lambda i, c: (0, 0, 0))
    merged, state = pl.pallas_call(
        functools.partial(_mix_prompt_kernel, d=d),
        grid=(b, nblk),
        in_specs=[smem, smem,
                  pl.BlockSpec((None, WINDOW, off["width"]), lambda i, c: (i, c, 0)),
                  pl.BlockSpec((None, WINDOW, 2 * kvw),
                               lambda i, c: (i, jnp.maximum(c - 1, 0), kv_col_block)),
                  const3((n_ret, WINDOW, WINDOW)), const3((n_ret, WINDOW, DK)),
                  const3((n_ret, WINDOW, DK))],
        out_specs=[pl.BlockSpec((None, WINDOW, d), lambda i, c: (i, c, 0)),
                   pl.BlockSpec((None, n_ret, DK, DK), lambda i, c: (i, 0, 0, 0))],
        out_shape=[jax.ShapeDtypeStruct((b, seq, d), BF16),
                   jax.ShapeDtypeStruct((b, n_ret, DK, DK), F32)],
        scratch_shapes=[pltpu.VMEM((WINDOW, d), F32)],
        compiler_params=_cparams(2),
        name="mix_prompt",
    )(sinks, tabs["sdec"], proj3, proj3, tabs["dintra"], tabs["qdec"], tabs["kdec"])
    return merged.reshape(b * seq, d), state


def _mix_sample_kernel(sink_ref, sdec_ref, blk_ref, kvnew_ref, s0_ref, kc_ref, vc_ref, dmask_ref,
                       qdec_ref, kdec_ref, *rest, d, dec_seq, n_alias):
    out_ref, s_out_ref, kout_ref, vout_ref, pad_ref, macc_ref = rest[n_alias:]
    off = _segment_offsets(d)
    n_ret = d // DK
    n_kv = d // (HD * GROUP)
    kvw = n_kv * HD
    nb = SAMPLE_BATCHES_PER_STEP
    rows = nb * dec_seq
    pad = pad_ref.shape[0]
    p_k, p_v, p_ka, p_va = 0, d, 2 * d, 2 * d + kvw

    @pl.when(pl.program_id(0) == 0)
    def _init():
        pad_ref[...] = jnp.zeros_like(pad_ref)

    pad_ref[0:rows, p_k:p_k + 2 * d] = blk_ref[:, off["k"]:off["k"] + 2 * d]
    pad_ref[0:rows, p_ka:p_ka + 2 * kvw] = blk_ref[:, off["ka"]:off["ka"] + 2 * kvw]

    tok_bits = dec_seq.bit_length() - 1
    row_batch = lax.broadcasted_iota(jnp.int32, (rows, 1), 0) >> tok_bits
    pad_batch = lax.broadcasted_iota(jnp.int32, (pad, 1), 0) >> tok_bits

    def seg(name, h, width=DK):
        return blk_ref[:, off[name] + h * width:off[name] + (h + 1) * width]

    def padded(col0, h, width=DK):
        return pad_ref[:, col0 + h * width:col0 + (h + 1) * width]

    heads = range(n_ret)
    bis = range(nb)
    att = [_dot_nt(seg("q", h), padded(p_k, h)) for h in heads]
    inter = [[_dot(seg("q", h), s0_ref[bi, h].astype(BF16)) for bi in bis] for h in heads]
    kd = [(padded(p_k, h).astype(F32) * kdec_ref[h]).astype(BF16) for h in heads]
    v_b = [[jnp.where(pad_batch == bi, padded(p_v, h), jnp.zeros((pad, DK), BF16)) for bi in bis]
           for h in heads]
    upd = [[_dot_tn(kd[h], v_b[h][bi]) for bi in bis] for h in heads]
    for h in heads:
        for bi in bis:
            s_out_ref[bi, h] = s0_ref[bi, h] * sdec_ref[h] + upd[h][bi]
    att = [(att[h] * dmask_ref[h]).astype(BF16) for h in heads]
    for h in heads:
        own = inter[h][0]
        for bi in bis[1:]:
            own = jnp.where(row_batch == bi, inter[h][bi], own)
        o = _dot(att[h], padded(p_v, h)) + own * qdec_ref[h]
        macc_ref[:, h * DK:(h + 1) * DK] = _retention_gate(o, seg("g", h), seg("mr", h))

    qrows = GROUP * rows
    q_tok = lax.broadcasted_iota(jnp.int32, (qrows, 1), 0) & (dec_seq - 1)
    q_batch = (lax.broadcasted_iota(jnp.int32, (qrows, 1), 0) & (rows - 1)) >> tok_bits
    cache_j = lax.broadcasted_iota(jnp.int32, (qrows, WINDOW), 1)
    new_j = lax.broadcasted_iota(jnp.int32, (qrows, pad), 1)
    mask_cache = cache_j >= q_tok
    mask_new = [((new_j >> tok_bits) == bi) & ((new_j & (dec_seq - 1)) <= q_tok) for bi in bis]
    scale = HD ** -0.5
    kvs = range(n_kv)
    ones = jnp.ones((WINDOW, HD), BF16)
    qs = [jnp.concatenate([seg("qa", kv * GROUP + g, HD) for g in range(GROUP)], axis=0) for kv in kvs]
    s_new = [_dot_nt(qs[kv], padded(p_ka, kv, HD)) for kv in kvs]
    s_cache = [[_dot_nt(qs[kv], kc_ref[bi, :, kv, :].astype(BF16)) for bi in bis] for kv in kvs]
    v_new = [jnp.concatenate([padded(p_va, kv, HD), ones], axis=1) for kv in kvs]
    v_cache = [[jnp.concatenate([vc_ref[bi, :, kv, :].astype(BF16), ones], axis=1) for bi in bis]
               for kv in kvs]
    for kv in kvs:
        sink_col = jnp.concatenate(
            [jnp.full((rows, 1), sink_ref[kv * GROUP + g], F32) for g in range(GROUP)], axis=0)
        o_kv = None
        for bi in bis:
            s_c = jnp.where(mask_cache, s_cache[kv][bi] * scale, NEG_INF)
            s_n = jnp.where(mask_new[bi], s_new[kv] * scale, NEG_INF)
            m = jnp.maximum(jnp.maximum(jnp.max(s_c, axis=-1, keepdims=True),
                                        jnp.max(s_n, axis=-1, keepdims=True)), sink_col)
            pv = (_dot(jnp.exp(s_c - m).astype(BF16), v_cache[kv][bi])
                  + _dot(jnp.exp(s_n - m).astype(BF16), v_new[kv]))
            o = pv[:, :HD] / (pv[:, HD:] + jnp.exp(sink_col - m))
            o_kv = o if o_kv is None else jnp.where(q_batch == bi, o, o_kv)
        for g in range(GROUP):
            head = kv * GROUP + g
            hs = slice(head * HD, (head + 1) * HD)
            out_ref[:, hs] = (macc_ref[:, hs] + seg("ma", head, HD).astype(F32)
                              * o_kv[g * rows:(g + 1) * rows]).astype(BF16)

    keep = WINDOW - dec_seq
    for bi in range(nb):
        kout_ref[bi, 0:keep] = kc_ref[bi, dec_seq:WINDOW]
        vout_ref[bi, 0:keep] = vc_ref[bi, dec_seq:WINDOW]
        for t in range(dec_seq):
            r = bi * dec_seq + t
            for kv in range(n_kv):
                kout_ref[bi, keep + t, kv:kv + 1, :] = kvnew_ref[r:r + 1, kv * HD:(kv + 1) * HD]
                vout_ref[bi, keep + t, kv:kv + 1, :] = kvnew_ref[r:r + 1, kvw + kv * HD:kvw + (kv + 1) * HD]


def _mix_sample(proj, kvnew, sinks, tabs, layer, state_all, k_all, v_all, prev_outs, dec_seq, d):
    off = _segment_offsets(d)
    n_ret = d // DK
    n_kv = d // (HD * GROUP)
    kvw = n_kv * HD
    nb = SAMPLE_BATCHES_PER_STEP
    b = state_all.shape[1]
    rows = nb * dec_seq
    smem = pl.BlockSpec(memory_space=pltpu.SMEM)
    const3 = lambda shape: pl.BlockSpec(shape, lambda i: (0, 0, 0))
    state_spec = lambda: pl.BlockSpec((None, nb, n_ret, DK, DK), lambda i: (layer, i, 0, 0, 0))
    win_spec = lambda: pl.BlockSpec((None, nb, WINDOW, n_kv, HD), lambda i: (layer, i, 0, 0, 0))
    in_specs = [smem, smem,
                pl.BlockSpec((rows, off["width"]), lambda i: (i, 0)),
                pl.BlockSpec((rows, 2 * kvw), lambda i: (i, 0)),
                state_spec(), win_spec(), win_spec(),
                const3((n_ret, rows, WINDOW)), const3((n_ret, rows, DK)), const3((n_ret, WINDOW, DK))]
    args = [sinks, tabs["sdec"], proj, kvnew, state_all, k_all, v_all,
            tabs["dmask"], tabs["qdec"], tabs["kdec"]]
    aliases = {}
    if prev_outs is not None:
        for j, arr in enumerate(prev_outs):
            aliases[len(args)] = 1 + j
            in_specs.append(pl.BlockSpec(memory_space=pl.ANY))
            args.append(arr)
    n_alias = len(aliases)
    merged, state, kwin, vwin = pl.pallas_call(
        functools.partial(_mix_sample_kernel, d=d, dec_seq=dec_seq, n_alias=n_alias),
        grid=(b // nb,),
        in_specs=in_specs,
        out_specs=[pl.BlockSpec((rows, d), lambda i: (i, 0)), state_spec(), win_spec(), win_spec()],
        out_shape=[jax.ShapeDtypeStruct((b * dec_seq, d), BF16),
                   jax.ShapeDtypeStruct(state_all.shape, F32),
                   jax.ShapeDtypeStruct(k_all.shape, F32),
                   jax.ShapeDtypeStruct(v_all.shape, F32)],
        scratch_shapes=[pltpu.VMEM((WINDOW, 2 * d + 2 * kvw), BF16),
                        pltpu.VMEM((rows, d), F32)],
        input_output_aliases=aliases,
        compiler_params=_cparams(1),
        name="mix_sample",
    )(*args)
    return merged, (state, kwin, vwin)


def _outproj_kernel(a_ref, w_ref, x_ref, g_ref, lnw_ref, sc_ref, sh_ref, xo_ref, ho_ref):
    x = x_ref[...] + g_ref[...] * _dot(a_ref[...], w_ref[...])
    xo_ref[...] = x
    ho_ref[...] = _modnorm(x, lnw_ref[...], sc_ref[...], sh_ref[...]).astype(BF16)


def _outproj(a, w, layer, x, gate, lnw, sc, sh, tm):
    g, r, d = x.shape
    tok = lambda: pl.BlockSpec((None, tm, d), lambda i, j: (i, j, 0))
    return pl.pallas_call(
        _outproj_kernel,
        grid=(g, r // tm),
        in_specs=[tok(), pl.BlockSpec((None, d, d), lambda i, j: (layer, 0, 0)), tok(), _mod_spec(gate, tm),
                  pl.BlockSpec((1, d), lambda i, j: (0, 0)), _mod_spec(sc, tm), _mod_spec(sh, tm)],
        out_specs=[tok(), tok()],
        out_shape=[jax.ShapeDtypeStruct((g, r, d), F32), jax.ShapeDtypeStruct((g, r, d), BF16)],
        compiler_params=_cparams(2),
        name="outproj",
    )(a.reshape(g, r, d), w, x, gate, lnw, sc, sh)


def _up_kernel(h_ref, w_ref, o_ref):
    r = jnp.maximum(_dot(h_ref[...], w_ref[...]), 0.0)
    o_ref[...] = (r * r).astype(BF16)


def _up(h, w, layer, tm):
    m, d = h.shape
    f = w.shape[2]
    tn = min(1024, f)
    return pl.pallas_call(
        _up_kernel,
        grid=(f // tn, m // tm),
        in_specs=[pl.BlockSpec((tm, d), lambda n, i: (i, 0)),
                  pl.BlockSpec((None, d, tn), lambda n, i: (layer, 0, n))],
        out_specs=pl.BlockSpec((tm, tn), lambda n, i: (i, n)),
        out_shape=jax.ShapeDtypeStruct((m, f), BF16),
        compiler_params=_cparams(2),
        name="mlp_up",
    )(h, w)


def _down_kernel(u_ref, w_ref, x_ref, g_ref, *rest, with_norm):
    if with_norm:
        lnw_ref, sc_ref, sh_ref, xo_ref, ho_ref = rest
    else:
        (xo_ref,) = rest
    k = pl.program_id(2)

    @pl.when(k == 0)
    def _init():
        xo_ref[...] = jnp.zeros_like(xo_ref)

    xo_ref[...] += _dot(u_ref[...], w_ref[...])

    @pl.when(k == pl.num_programs(2) - 1)
    def _finish():
        x = x_ref[...] + g_ref[...] * xo_ref[...]
        xo_ref[...] = x
        if with_norm:
            ho_ref[...] = _modnorm(x, lnw_ref[...], sc_ref[...], sh_ref[...]).astype(BF16)


def _down(u, w, layer, x, gate, norm, tm):
    g, r, d = x.shape
    f = w.shape[1]
    tk = min(512, f)
    tok = lambda: pl.BlockSpec((None, tm, d), lambda i, j, k: (i, j, 0))
    in_specs = [pl.BlockSpec((None, tm, tk), lambda i, j, k: (i, j, k)),
                pl.BlockSpec((None, tk, d), lambda i, j, k: (layer, k, 0)), tok(), _mod_spec(gate, tm)]
    args = [u.reshape(g, r, f), w, x, gate]
    out_specs = [tok()]
    out_shape = [jax.ShapeDtypeStruct((g, r, d), F32)]
    if norm is not None:
        lnw, sc, sh = norm
        in_specs += [pl.BlockSpec((1, d), lambda i, j, k: (0, 0)), _mod_spec(sc, tm), _mod_spec(sh, tm)]
        args += [lnw, sc, sh]
        out_specs.append(tok())
        out_shape.append(jax.ShapeDtypeStruct((g, r, d), BF16))
    out = pl.pallas_call(
        functools.partial(_down_kernel, with_norm=norm is not None),
        grid=(g, r // tm, f // tk),
        in_specs=in_specs, out_specs=out_specs, out_shape=out_shape,
        compiler_params=_cparams(3),
        name="mlp_down",
    )(*args)
    return (out[0], out[1]) if norm is not None else (out[0], None)


def _rope_tables(pos):
    half = DK // 2
    inv = 1.0 / (ROPE_BASE ** (jnp.arange(half, dtype=F32) / half))
    ang = pos.astype(F32)[:, None] * inv[None, :]
    return jnp.cos(ang), jnp.sin(ang)


def _decay_tables(n_ret, chunk, reps, pad_cols):
    log_g = jnp.log1p(-jnp.exp2(-5.0 - jnp.arange(n_ret, dtype=F32)))
    rows = reps * chunk
    r = jnp.arange(rows)
    cidx = jnp.arange(pad_cols)
    idx = (r % chunk).astype(F32)
    diff = idx[:, None] - (cidx % chunk).astype(F32)[None, :]
    ok = (diff >= 0) & ((r // chunk)[:, None] == (cidx // chunk)[None, :]) & (cidx < rows)[None, :]
    dintra = jnp.where(ok, jnp.exp(jnp.where(ok, diff, 0.0) * log_g[:, None, None]), 0.0)
    qdec = jnp.exp((idx + 1.0) * log_g[:, None])[..., None]
    kidx = (jnp.arange(max(rows, pad_cols)) % chunk).astype(F32)
    kdec = jnp.exp((chunk - 1.0 - kidx) * log_g[:, None])[..., None]
    sdec = jnp.exp(chunk * log_g)
    return dict(dintra=dintra, dmask=dintra,
                qdec=jnp.broadcast_to(qdec, (n_ret, rows, DK)),
                kdec=jnp.broadcast_to(kdec, (n_ret, kidx.shape[0], DK)), sdec=sdec)


def kernel(x_prompt, x_sample, c_prompt, c_sample, state_ret, cache_k_win, cache_v_win, norm1_w,
           norm2_w, w_ada, b_ada, w_in, q_norm_w, k_norm_w, sinks, w_out, w_up, w_down):
    bp, seq, d = x_prompt.shape
    bs, dec_seq, _ = x_sample.shape
    depth = w_in.shape[0]
    n_ret = d // DK
    kvw = d // 4
    n_kv = kvw // HD
    win = cache_k_win.shape[2]
    assert win == WINDOW and seq % WINDOW == 0 and bs % SAMPLE_BATCHES_PER_STEP == 0
    assert dec_seq & (dec_seq - 1) == 0

    n_c = bp + bs
    c_rows = -(-n_c // 16) * 16
    c_all = jnp.concatenate([c_prompt, c_sample, jnp.zeros((c_rows - n_c, d), F32)], axis=0)
    mods = _ada(c_all, w_ada, b_ada)

    def group_mods(l):
        parts = jnp.split(mods[l], 6, axis=-1)
        mp = [p[:bp].reshape(bp, 1, d) for p in parts]
        ms = [jnp.repeat(p[bp:n_c], dec_seq, axis=0).reshape(1, bs * dec_seq, d) for p in parts]
        return mp, ms

    wi = w_in.astype(BF16)
    wo = w_out.astype(BF16)
    wu = w_up.astype(BF16)
    wd = w_down.astype(BF16)

    cos_p, sin_p = _rope_tables(jnp.arange(seq, dtype=jnp.int32))
    cos_s, sin_s = _rope_tables(PAST_LEN + jnp.arange(dec_seq, dtype=jnp.int32))
    cos_s = jnp.tile(cos_s, (bs, 1))
    sin_s = jnp.tile(sin_s, (bs, 1))
    tabs_p = _decay_tables(n_ret, WINDOW, 1, WINDOW)
    tabs_s = _decay_tables(n_ret, dec_seq, SAMPLE_BATCHES_PER_STEP, WINDOW)

    tm_p = min(512, seq)
    tm_down = min(1024, seq)
    ms_rows = bs * dec_seq
    nblk = seq // WINDOW
    xp = x_prompt
    xs = x_sample.reshape(1, ms_rows, d)

    mods_p, mods_s = group_mods(0)
    hp = _prenorm(xp, norm1_w[0:1], mods_p[1], mods_p[0], tm_p)
    hs = _prenorm(xs, norm1_w[0:1], mods_s[1], mods_s[0], ms_rows)

    outs = {k: [] for k in ("rp", "kp", "vp")}
    sample_state = None
    for l in range(depth):
        qn = q_norm_w[l:l + 1]
        kn = k_norm_w[l:l + 1]
        if l + 1 < depth:
            nxt_p, nxt_s = group_mods(l + 1)
            norm_p = (norm1_w[l + 1:l + 2], nxt_p[1], nxt_p[0])
            norm_s = (norm1_w[l + 1:l + 2], nxt_s[1], nxt_s[0])
        else:
            norm_p = norm_s = None

        hp2 = hp.reshape(bp * seq, d)
        proj = _inproj(hp2, wi, l, cos_p, sin_p, qn, kn, tm_p)
        kv_tail = _kvproj(hp2, wi, l, kn, WINDOW, bp, lambda i: i * nblk + nblk - 1)
        merged, r_new = _mix_prompt(proj, sinks[l], tabs_p, bp, seq, d)
        outs["rp"].append(r_new)
        outs["kp"].append(kv_tail[:, :kvw].reshape(bp, WINDOW, n_kv, HD))
        outs["vp"].append(kv_tail[:, kvw:].reshape(bp, WINDOW, n_kv, HD))
        xp, h2 = _outproj(merged, wo, l, xp, mods_p[2], norm2_w[l:l + 1], mods_p[4], mods_p[3], tm_p)
        u = _up(h2.reshape(bp * seq, d), wu, l, tm_p)
        xp, hp = _down(u, wd, l, xp, mods_p[5], norm_p, tm_down)

        hs2 = hs.reshape(ms_rows, d)
        proj = _inproj(hs2, wi, l, cos_s, sin_s, qn, kn, ms_rows)
        kv_new = _kvproj(hs2, wi, l, kn, ms_rows, 1, lambda i: i)
        merged, sample_state = _mix_sample(proj, kv_new, sinks[l], tabs_s, l, state_ret, cache_k_win,
                                           cache_v_win, sample_state, dec_seq, d)
        xs, h2 = _outproj(merged, wo, l, xs, mods_s[2], norm2_w[l:l + 1], mods_s[4], mods_s[3], ms_rows)
        u = _up(h2.reshape(ms_rows, d), wu, l, ms_rows)
        xs, hs = _down(u, wd, l, xs, mods_s[5], norm_s, ms_rows)

        if l + 1 < depth:
            mods_p, mods_s = nxt_p, nxt_s

    return (xp, xs.reshape(bs, dec_seq, d), jnp.stack(outs["rp"]), jnp.stack(outs["kp"]),
            jnp.stack(outs["vp"])) + tuple(sample_state)
```

```python
import functools
import math

import jax
import jax.numpy as jnp
from jax import lax
from jax.experimental import pallas as pl
from jax.experimental.pallas import tpu as pltpu

F32 = jnp.float32
BF16 = jnp.bfloat16

DK = 256
HD = 128
GROUP = 4
WINDOW = 128
ROPE_BASE = 10000.0
EPS = 1e-6
NEG_INF = -1e30
PAST_LEN = 8192
SAMPLE_BATCHES_PER_STEP = 4
VMEM_LIMIT = 56 * 1024 * 1024


def _cparams(n_axes):
    return pltpu.CompilerParams(dimension_semantics=("arbitrary",) * n_axes,
                                vmem_limit_bytes=VMEM_LIMIT)


def _sigmoid(x):
    return 1.0 / (1.0 + jnp.exp(-x))


def _modnorm(x, lnw, sc, sh):
    y = x * lax.rsqrt(jnp.mean(x * x, axis=-1, keepdims=True) + EPS)
    return (y * lnw) * (1.0 + sc) + sh


def _dot(a, b):
    return jnp.dot(a, b, preferred_element_type=F32)


def _dot_nt(a, b):
    return lax.dot_general(a, b, (((1,), (1,)), ((), ())), preferred_element_type=F32)


def _dot_tn(a, b):
    return lax.dot_general(a, b, (((0,), (0,)), ((), ())), preferred_element_type=F32)


def _ada_kernel(c_ref, w_ref, b_ref, o_ref):
    c = c_ref[...]
    s = (c * _sigmoid(c)).astype(BF16)
    o_ref[...] = _dot(s, w_ref[...].astype(BF16)) + b_ref[...]


def _ada(c_all, w_ada, b_ada):
    depth, d, n6 = w_ada.shape
    rows = c_all.shape[0]
    tn = min(1024, n6)
    return pl.pallas_call(
        _ada_kernel,
        grid=(depth, n6 // tn),
        in_specs=[pl.BlockSpec((rows, d), lambda l, n: (0, 0)),
                  pl.BlockSpec((None, d, tn), lambda l, n: (l, 0, n)),
                  pl.BlockSpec((None, 1, tn), lambda l, n: (l, 0, n))],
        out_specs=pl.BlockSpec((None, rows, tn), lambda l, n: (l, 0, n)),
        out_shape=jax.ShapeDtypeStruct((depth, rows, n6), F32),
        compiler_params=_cparams(2),
        name="ada",
    )(c_all, w_ada, b_ada.reshape(depth, 1, n6))


def _prenorm_kernel(x_ref, lnw_ref, sc_ref, sh_ref, h_ref):
    h_ref[...] = _modnorm(x_ref[...], lnw_ref[...], sc_ref[...], sh_ref[...]).astype(BF16)


def _mod_spec(mod, tm):
    d = mod.shape[-1]
    if mod.shape[1] == 1:
        return pl.BlockSpec((None, 1, d), lambda g, r, *_: (g, 0, 0))
    return pl.BlockSpec((None, tm, d), lambda g, r, *_: (g, r, 0))


def _prenorm(x, lnw, sc, sh, tm):
    g, r, d = x.shape
    return pl.pallas_call(
        _prenorm_kernel,
        grid=(g, r // tm),
        in_specs=[pl.BlockSpec((None, tm, d), lambda i, j: (i, j, 0)),
                  pl.BlockSpec((1, d), lambda i, j: (0, 0)),
                  _mod_spec(sc, tm), _mod_spec(sh, tm)],
        out_specs=pl.BlockSpec((None, tm, d), lambda i, j: (i, j, 0)),
        out_shape=jax.ShapeDtypeStruct((g, r, d), BF16),
        compiler_params=_cparams(2),
        name="prenorm",
    )(x, lnw, sc, sh)


def _rms_heads(a, w):
    outs = []
    for j in range(a.shape[1] // HD):
        t = a[:, j * HD:(j + 1) * HD]
        outs.append(t * lax.rsqrt(jnp.mean(t * t, axis=-1, keepdims=True) + EPS) * w)
    return outs


def _inproj_kernel(h_ref, w_ref, cos_ref, sin_ref, qn_ref, kn_ref, o_ref, *, tn):
    n = pl.program_id(0)
    half = DK // 2

    def slab(j):
        return _dot(h_ref[...], w_ref[:, j * DK:(j + 1) * DK])

    @pl.when(n < 4)
    def _rotary():
        scale = jnp.where(n < 2, 1.0, DK ** -0.5).astype(F32)
        cos = cos_ref[...]
        sin = sin_ref[...]
        for j in range(tn // DK):
            a = slab(j)
            t1 = a[:, :half]
            t2 = a[:, half:]
            o_ref[:, j * DK:j * DK + half] = ((t1 * cos - t2 * sin) * scale).astype(BF16)
            o_ref[:, j * DK + half:(j + 1) * DK] = ((t1 * sin + t2 * cos) * scale).astype(BF16)

    def _rms_slabs(lo, hi, w):
        for j in range(lo, hi):
            for i, y in enumerate(_rms_heads(slab(j), w)):
                o_ref[:, j * DK + i * HD:j * DK + (i + 1) * HD] = y.astype(BF16)

    @pl.when((n == 8) | (n == 9))
    def _qnorm():
        _rms_slabs(0, tn // DK, qn_ref[...])

    @pl.when(n == 10)
    def _kv():
        a = _dot(h_ref[...], w_ref[:, :tn // 2])
        for i, y in enumerate(_rms_heads(a, kn_ref[...])):
            o_ref[:, i * HD:(i + 1) * HD] = y.astype(BF16)
        o_ref[:, tn // 2:] = _dot(h_ref[...], w_ref[:, tn // 2:]).astype(BF16)

    @pl.when((n == 4) | (n == 5))
    def _plain():
        o_ref[...] = _dot(h_ref[...], w_ref[...]).astype(BF16)

    @pl.when((n == 6) | (n == 7))
    def _silu_gate():
        for j in range(tn // DK):
            a = slab(j)
            o_ref[:, j * DK:(j + 1) * DK] = (a * _sigmoid(a)).astype(BF16)

    @pl.when(n > 10)
    def _sigmoid_gate():
        for j in range(tn // DK):
            o_ref[:, j * DK:(j + 1) * DK] = _sigmoid(slab(j)).astype(BF16)


def _inproj(h, w, layer, cos, sin, qn, kn, tm):
    m, d = h.shape
    width = w.shape[2]
    tn = d // 2
    nblk = cos.shape[0] // tm
    return pl.pallas_call(
        functools.partial(_inproj_kernel, tn=tn),
        grid=(width // tn, m // tm),
        in_specs=[pl.BlockSpec((tm, d), lambda n, i: (i, 0)),
                  pl.BlockSpec((None, d, tn), lambda n, i: (layer, 0, n)),
                  pl.BlockSpec((tm, HD), lambda n, i: (i % nblk, 0)),
                  pl.BlockSpec((tm, HD), lambda n, i: (i % nblk, 0)),
                  pl.BlockSpec((1, HD), lambda n, i: (0, 0)),
                  pl.BlockSpec((1, HD), lambda n, i: (0, 0))],
        out_specs=pl.BlockSpec((tm, tn), lambda n, i: (i, n)),
        out_shape=jax.ShapeDtypeStruct((m, width), BF16),
        compiler_params=_cparams(2),
        name="inproj",
    )(h, w, cos, sin, qn, kn)


def _kvproj_kernel(h_ref, w_ref, kn_ref, o_ref):
    kvw = o_ref.shape[1] // 2
    a = _dot(h_ref[...], w_ref[...])
    for i, y in enumerate(_rms_heads(a[:, :kvw], kn_ref[...])):
        o_ref[:, i * HD:(i + 1) * HD] = y
    o_ref[:, kvw:] = a[:, kvw:]


def _kvproj(h, w, layer, kn, tm, n_blocks, row_block):
    d = h.shape[1]
    kvw = d // 4
    col_block = (5 * d) // (2 * kvw)
    return pl.pallas_call(
        _kvproj_kernel,
        grid=(n_blocks,),
        in_specs=[pl.BlockSpec((tm, d), lambda i: (row_block(i), 0)),
                  pl.BlockSpec((None, d, 2 * kvw), lambda i: (layer, 0, col_block)),
                  pl.BlockSpec((1, HD), lambda i: (0, 0))],
        out_specs=pl.BlockSpec((tm, 2 * kvw), lambda i: (i, 0)),
        out_shape=jax.ShapeDtypeStruct((n_blocks * tm, 2 * kvw), F32),
        compiler_params=_cparams(1),
        name="kvproj",
    )(h, w, kn)


def _segment_offsets(d):
    kvw = d // 4
    return dict(q=0, k=d, v=2 * d, g=3 * d, qa=4 * d, ka=5 * d, va=5 * d + kvw,
                mr=5 * d + 2 * kvw, ma=6 * d + 2 * kvw, width=7 * d + 2 * kvw)


def _retention_gate(o, g_act, mr_act):
    on = o * lax.rsqrt(jnp.mean(o * o, axis=-1, keepdims=True) + EPS)
    return mr_act.astype(F32) * (on * g_act.astype(F32))


def _mix_prompt_kernel(sink_ref, sdec_ref, blk_ref, prev_ref, dintra_ref, qdec_ref, kdec_ref,
                       out_ref, s_ref, macc_ref, *, d):
    c = pl.program_id(1)
    off = _segment_offsets(d)
    n_ret = d // DK
    n_kv = d // (HD * GROUP)
    kvw = n_kv * HD
    blk = WINDOW

    @pl.when(c == 0)
    def _init():
        s_ref[...] = jnp.zeros_like(s_ref)

    def seg(name, h, width=DK):
        return blk_ref[:, off[name] + h * width:off[name] + (h + 1) * width]

    heads = range(n_ret)
    att = [_dot_nt(seg("q", h), seg("k", h)) for h in heads]
    inter = [_dot(seg("q", h), s_ref[h].astype(BF16)) for h in heads]
    upd = [_dot_tn((seg("k", h).astype(F32) * kdec_ref[h]).astype(BF16), seg("v", h)) for h in heads]
    for h in heads:
        s_ref[h] = s_ref[h] * sdec_ref[h] + upd[h]
    att = [(att[h] * dintra_ref[h]).astype(BF16) for h in heads]
    o_ret = [_dot(att[h], seg("v", h)) + inter[h] * qdec_ref[h] for h in heads]
    for h in heads:
        macc_ref[:, h * DK:(h + 1) * DK] = _retention_gate(o_ret[h], seg("g", h), seg("mr", h))

    qi = lax.broadcasted_iota(jnp.int32, (blk, 2 * blk), 0)
    kj = lax.broadcasted_iota(jnp.int32, (blk, 2 * blk), 1)
    diff = kj - qi
    jmin = jnp.where(c == 0, blk, 0)
    mask = (diff >= 0) & (diff <= WINDOW) & (kj >= jmin)
    scale = HD ** -0.5
    kvs = range(n_kv)
    ones = jnp.ones((2 * blk, HD), BF16)
    scores, values = [], []
    for kv in kvs:
        qs = jnp.concatenate([seg("qa", kv * GROUP + g, HD) for g in range(GROUP)], axis=0)
        k2 = jnp.concatenate([prev_ref[:, kv * HD:(kv + 1) * HD], seg("ka", kv, HD)], axis=0)
        v2 = jnp.concatenate([prev_ref[:, kvw + kv * HD:kvw + (kv + 1) * HD], seg("va", kv, HD)], axis=0)
        scores.append(_dot_nt(qs, k2))
        values.append(jnp.concatenate([v2, ones], axis=1))
    probs, sink_terms = [], []
    for kv in kvs:
        p_rows = []
        for g in range(GROUP):
            sink = sink_ref[kv * GROUP + g]
            s = jnp.where(mask, scores[kv][g * blk:(g + 1) * blk] * scale, NEG_INF)
            m = jnp.maximum(jnp.max(s, axis=-1, keepdims=True), sink)
            p_rows.append(jnp.exp(s - m).astype(BF16))
            sink_terms.append(jnp.exp(sink - m))
        probs.append(jnp.concatenate(p_rows, axis=0))
    pv = [_dot(probs[kv], values[kv]) for kv in kvs]
    for kv in kvs:
        for g in range(GROUP):
            head = kv * GROUP + g
            hs = slice(head * HD, (head + 1) * HD)
            o = pv[kv][g * blk:(g + 1) * blk, :HD]
            denom = pv[kv][g * blk:(g + 1) * blk, HD:] + sink_terms[head]
            out_ref[:, hs] = (macc_ref[:, hs] + seg("ma", head, HD).astype(F32) * (o / denom)).astype(BF16)


def _mix_prompt(proj, sinks, tabs, b, seq, d):
    off = _segment_offsets(d)
    n_ret = d // DK
    kvw = d // 4
    nblk = seq // WINDOW
    proj3 = proj.reshape(b, seq, off["width"])
    kv_col_block = off["ka"] // (2 * kvw)
    smem = pl.BlockSpec(memory_space=pltpu.SMEM)
    const3 = lambda shape: pl.BlockSpec(shape, lambda i, c: (0, 0, 0))
    merged, state = pl.pallas_call(
        functools.partial(_mix_prompt_kernel, d=d),
        grid=(b, nblk),
        in_specs=[smem, smem,
                  pl.BlockSpec((None, WINDOW, off["width"]), lambda i, c: (i, c, 0)),
                  pl.BlockSpec((None, WINDOW, 2 * kvw),
                               lambda i, c: (i, jnp.maximum(c - 1, 0), kv_col_block)),
                  const3((n_ret, WINDOW, WINDOW)), const3((n_ret, WINDOW, DK)),
                  const3((n_ret, WINDOW, DK))],
        out_specs=[pl.BlockSpec((None, WINDOW, d), lambda i, c: (i, c, 0)),
                   pl.BlockSpec((None, n_ret, DK, DK), lambda i, c: (i, 0, 0, 0))],
        out_shape=[jax.ShapeDtypeStruct((b, seq, d), BF16),
                   jax.ShapeDtypeStruct((b, n_ret, DK, DK), F32)],
        scratch_shapes=[pltpu.VMEM((WINDOW, d), F32)],
        compiler_params=_cparams(2),
        name="mix_prompt",
    )(sinks, tabs["sdec"], proj3, proj3, tabs["dintra"], tabs["qdec"], tabs["kdec"])
    return merged.reshape(b * seq, d), state


def _mix_sample_kernel(sink_ref, sdec_ref, blk_ref, kvnew_ref, s0_ref, kc_ref, vc_ref, dmask_ref,
                       qdec_ref, kdec_ref, *rest, d, dec_seq, n_alias):
    out_ref, s_out_ref, kout_ref, vout_ref, pad_ref, macc_ref = rest[n_alias:]
    off = _segment_offsets(d)
    n_ret = d // DK
    n_kv = d // (HD * GROUP)
    kvw = n_kv * HD
    nb = SAMPLE_BATCHES_PER_STEP
    rows = nb * dec_seq
    pad = pad_ref.shape[0]
    p_k, p_v, p_ka, p_va = 0, d, 2 * d, 2 * d + kvw

    @pl.when(pl.program_id(0) == 0)
    def _init():
        pad_ref[...] = jnp.zeros_like(pad_ref)

    pad_ref[0:rows, p_k:p_k + 2 * d] = blk_ref[:, off["k"]:off["k"] + 2 * d]
    pad_ref[0:rows, p_ka:p_ka + 2 * kvw] = blk_ref[:, off["ka"]:off["ka"] + 2 * kvw]

    tok_bits = dec_seq.bit_length() - 1
    row_batch = lax.broadcasted_iota(jnp.int32, (rows, 1), 0) >> tok_bits
    pad_batch = lax.broadcasted_iota(jnp.int32, (pad, 1), 0) >> tok_bits

    def seg(name, h, width=DK):
        return blk_ref[:, off[name] + h * width:off[name] + (h + 1) * width]

    def padded(col0, h, width=DK):
        return pad_ref[:, col0 + h * width:col0 + (h + 1) * width]

    heads = range(n_ret)
    bis = range(nb)
    att = [_dot_nt(seg("q", h), padded(p_k, h)) for h in heads]
    inter = [[_dot(seg("q", h), s0_ref[bi, h].astype(BF16)) for bi in bis] for h in heads]
    kd = [(padded(p_k, h).astype(F32) * kdec_ref[h]).astype(BF16) for h in heads]
    v_b = [[jnp.where(pad_batch == bi, padded(p_v, h), jnp.zeros((pad, DK), BF16)) for bi in bis]
           for h in heads]
    upd = [[_dot_tn(kd[h], v_b[h][bi]) for bi in bis] for h in heads]
    for h in heads:
        for bi in bis:
            s_out_ref[bi, h] = s0_ref[bi, h] * sdec_ref[h] + upd[h][bi]
    att = [(att[h] * dmask_ref[h]).astype(BF16) for h in heads]
    for h in heads:
        own = inter[h][0]
        for bi in bis[1:]:
            own = jnp.where(row_batch == bi, inter[h][bi], own)
        o = _dot(att[h], padded(p_v, h)) + own * qdec_ref[h]
        macc_ref[:, h * DK:(h + 1) * DK] = _retention_gate(o, seg("g", h), seg("mr", h))

    qrows = GROUP * rows
    q_tok = lax.broadcasted_iota(jnp.int32, (qrows, 1), 0) & (dec_seq - 1)
    q_batch = (lax.broadcasted_iota(jnp.int32, (qrows, 1), 0) & (rows - 1)) >> tok_bits
    cache_j = lax.broadcasted_iota(jnp.int32, (qrows, WINDOW), 1)
    new_j = lax.broadcasted_iota(jnp.int32, (qrows, pad), 1)
    mask_cache = cache_j >= q_tok
    mask_new = [((new_j >> tok_bits) == bi) & ((new_j & (dec_seq - 1)) <= q_tok) for bi in bis]
    scale = HD ** -0.5
    kvs = range(n_kv)
    ones = jnp.ones((WINDOW, HD), BF16)
    qs = [jnp.concatenate([seg("qa", kv * GROUP + g, HD) for g in range(GROUP)], axis=0) for kv in kvs]
    s_new = [_dot_nt(qs[kv], padded(p_ka, kv, HD)) for kv in kvs]
    s_cache = [[_dot_nt(qs[kv], kc_ref[bi, :, kv, :].astype(BF16)) for bi in bis] for kv in kvs]
    v_new = [jnp.concatenate([padded(p_va, kv, HD), ones], axis=1) for kv in kvs]
    v_cache = [[jnp.concatenate([vc_ref[bi, :, kv, :].astype(BF16), ones], axis=1) for bi in bis]
               for kv in kvs]
    for kv in kvs:
        sink_col = jnp.concatenate(
            [jnp.full((rows, 1), sink_ref[kv * GROUP + g], F32) for g in range(GROUP)], axis=0)
        o_kv = None
        for bi in bis:
            s_c = jnp.where(mask_cache, s_cache[kv][bi] * scale, NEG_INF)
            s_n = jnp.where(mask_new[bi], s_new[kv] * scale, NEG_INF)
            m = jnp.maximum(jnp.maximum(jnp.max(s_c, axis=-1, keepdims=True),
                                        jnp.max(s_n, axis=-1, keepdims=True)), sink_col)
            pv = (_dot(jnp.exp(s_c - m).astype(BF16), v_cache[kv][bi])
                  + _dot(jnp.exp(s_n - m).astype(BF16), v_new[kv]))
            o = pv[:, :HD] / (pv[:, HD:] + jnp.exp(sink_col - m))
            o_kv = o if o_kv is None else jnp.where(q_batch == bi, o, o_kv)
        for g in range(GROUP):
            head = kv * GROUP + g
            hs = slice(head * HD, (head + 1) * HD)
            out_ref[:, hs] = (macc_ref[:, hs] + seg("ma", head, HD).astype(F32)
                              * o_kv[g * rows:(g + 1) * rows]).astype(BF16)

    keep = WINDOW - dec_seq
    for bi in range(nb):
        kout_ref[bi, 0:keep] = kc_ref[bi, dec_seq:WINDOW]
        vout_ref[bi, 0:keep] = vc_ref[bi, dec_seq:WINDOW]
        for t in range(dec_seq):
            r = bi * dec_seq + t
            for kv in range(n_kv):
                kout_ref[bi, keep + t, kv:kv + 1, :] = kvnew_ref[r:r + 1, kv * HD:(kv + 1) * HD]
                vout_ref[bi, keep + t, kv:kv + 1, :] = kvnew_ref[r:r + 1, kvw + kv * HD:kvw + (kv + 1) * HD]


def _mix_sample(proj, kvnew, sinks, tabs, layer, state_all, k_all, v_all, prev_outs, dec_seq, d):
    off = _segment_offsets(d)
    n_ret = d // DK
    n_kv = d // (HD * GROUP)
    kvw = n_kv * HD
    nb = SAMPLE_BATCHES_PER_STEP
    b = state_all.shape[1]
    rows = nb * dec_seq
    smem = pl.BlockSpec(memory_space=pltpu.SMEM)
    const3 = lambda shape: pl.BlockSpec(shape, lambda i: (0, 0, 0))
    state_spec = lambda: pl.BlockSpec((None, nb, n_ret, DK, DK), lambda i: (layer, i, 0, 0, 0))
    win_spec = lambda: pl.BlockSpec((None, nb, WINDOW, n_kv, HD), lambda i: (layer, i, 0, 0, 0))
    in_specs = [smem, smem,
                pl.BlockSpec((rows, off["width"]), lambda i: (i, 0)),
                pl.BlockSpec((rows, 2 * kvw), lambda i: (i, 0)),
                state_spec(), win_spec(), win_spec(),
                const3((n_ret, rows, WINDOW)), const3((n_ret, rows, DK)), const3((n_ret, WINDOW, DK))]
    args = [sinks, tabs["sdec"], proj, kvnew, state_all, k_all, v_all,
            tabs["dmask"], tabs["qdec"], tabs["kdec"]]
    aliases = {}
    if prev_outs is not None:
        for j, arr in enumerate(prev_outs):
            aliases[len(args)] = 1 + j
            in_specs.append(pl.BlockSpec(memory_space=pl.ANY))
            args.append(arr)
    n_alias = len(aliases)
    merged, state, kwin, vwin = pl.pallas_call(
        functools.partial(_mix_sample_kernel, d=d, dec_seq=dec_seq, n_alias=n_alias),
        grid=(b // nb,),
        in_specs=in_specs,
        out_specs=[pl.BlockSpec((rows, d), lambda i: (i, 0)), state_spec(), win_spec(), win_spec()],
        out_shape=[jax.ShapeDtypeStruct((b * dec_seq, d), BF16),
                   jax.ShapeDtypeStruct(state_all.shape, F32),
                   jax.ShapeDtypeStruct(k_all.shape, F32),
                   jax.ShapeDtypeStruct(v_all.shape, F32)],
        scratch_shapes=[pltpu.VMEM((WINDOW, 2 * d + 2 * kvw), BF16),
                        pltpu.VMEM((rows, d), F32)],
        input_output_aliases=aliases,
        compiler_params=_cparams(1),
        name="mix_sample",
    )(*args)
    return merged, (state, kwin, vwin)


def _proj_residual_kernel(a_ref, w_ref, x_ref, g_ref, *rest, with_norm):
    if with_norm:
        lnw_ref, sc_ref, sh_ref, xo_ref, ho_ref = rest
    else:
        (xo_ref,) = rest
    d = xo_ref.shape[1]
    half = DK // 2
    slabs = [slice(j * DK, (j + 1) * DK) for j in range(d // DK)]
    sq = None
    for cs in slabs:
        x = x_ref[:, cs] + g_ref[:, cs] * _dot(a_ref[...], w_ref[:, cs])
        xo_ref[:, cs] = x
        if with_norm:
            x2 = x * x
            part = x2[:, :half] + x2[:, half:]
            sq = part if sq is None else sq + part
    if with_norm:
        r = lax.rsqrt(jnp.sum(sq, axis=-1, keepdims=True) / d + EPS)
        for cs in slabs:
            y = (xo_ref[:, cs] * r) * lnw_ref[:, cs]
            ho_ref[:, cs] = (y * (1.0 + sc_ref[:, cs]) + sh_ref[:, cs]).astype(BF16)


def _proj_residual(a, w, layer, x, gate, norm, tm, name):
    g, r, d = x.shape
    kdim = w.shape[1]
    tok = lambda: pl.BlockSpec((None, tm, d), lambda i, j: (i, j, 0))
    in_specs = [pl.BlockSpec((None, tm, kdim), lambda i, j: (i, j, 0)),
                pl.BlockSpec((None, kdim, d), lambda i, j: (layer, 0, 0), pipeline_mode=pl.Buffered(1)),
                tok(), _mod_spec(gate, tm)]
    args = [a.reshape(g, r, kdim), w, x, gate]
    out_specs = [tok()]
    out_shape = [jax.ShapeDtypeStruct((g, r, d), F32)]
    if norm is not None:
        lnw, sc, sh = norm
        in_specs += [pl.BlockSpec((1, d), lambda i, j: (0, 0)), _mod_spec(sc, tm), _mod_spec(sh, tm)]
        args += [lnw, sc, sh]
        out_specs.append(tok())
        out_shape.append(jax.ShapeDtypeStruct((g, r, d), BF16))
    out = pl.pallas_call(
        functools.partial(_proj_residual_kernel, with_norm=norm is not None),
        grid=(g, r // tm),
        in_specs=in_specs, out_specs=out_specs, out_shape=out_shape,
        compiler_params=_cparams(2),
        name=name,
    )(*args)
    return (out[0], out[1]) if norm is not None else (out[0], None)


def _up_kernel(h_ref, w_ref, o_ref):
    r = jnp.maximum(_dot(h_ref[...], w_ref[...]), 0.0)
    o_ref[...] = (r * r).astype(BF16)


def _up(h, w, layer, tm):
    m, d = h.shape
    f = w.shape[2]
    tn = min(1024, f)
    return pl.pallas_call(
        _up_kernel,
        grid=(f // tn, m // tm),
        in_specs=[pl.BlockSpec((tm, d), lambda n, i: (i, 0)),
                  pl.BlockSpec((None, d, tn), lambda n, i: (layer, 0, n))],
        out_specs=pl.BlockSpec((tm, tn), lambda n, i: (i, n)),
        out_shape=jax.ShapeDtypeStruct((m, f), BF16),
        compiler_params=_cparams(2),
        name="mlp_up",
    )(h, w)


def _rope_tables(pos):
    half = DK // 2
    inv = 1.0 / (ROPE_BASE ** (jnp.arange(half, dtype=F32) / half))
    ang = pos.astype(F32)[:, None] * inv[None, :]
    return jnp.cos(ang), jnp.sin(ang)


def _decay_tables(n_ret, chunk, reps, pad_cols):
    log_g = jnp.log1p(-jnp.exp2(-5.0 - jnp.arange(n_ret, dtype=F32)))
    rows = reps * chunk
    r = jnp.arange(rows)
    cidx = jnp.arange(pad_cols)
    idx = (r % chunk).astype(F32)
    diff = idx[:, None] - (cidx % chunk).astype(F32)[None, :]
    ok = (diff >= 0) & ((r // chunk)[:, None] == (cidx // chunk)[None, :]) & (cidx < rows)[None, :]
    dintra = jnp.where(ok, jnp.exp(jnp.where(ok, diff, 0.0) * log_g[:, None, None]), 0.0)
    qdec = jnp.exp((idx + 1.0) * log_g[:, None])[..., None]
    kidx = (jnp.arange(max(rows, pad_cols)) % chunk).astype(F32)
    kdec = jnp.exp((chunk - 1.0 - kidx) * log_g[:, None])[..., None]
    sdec = jnp.exp(chunk * log_g)
    return dict(dintra=dintra, dmask=dintra,
                qdec=jnp.broadcast_to(qdec, (n_ret, rows, DK)),
                kdec=jnp.broadcast_to(kdec, (n_ret, kidx.shape[0], DK)), sdec=sdec)


def kernel(x_prompt, x_sample, c_prompt, c_sample, state_ret, cache_k_win, cache_v_win, norm1_w,
           norm2_w, w_ada, b_ada, w_in, q_norm_w, k_norm_w, sinks, w_out, w_up, w_down):
    bp, seq, d = x_prompt.shape
    bs, dec_seq, _ = x_sample.shape
    depth = w_in.shape[0]
    n_ret = d // DK
    kvw = d // 4
    n_kv = kvw // HD
    win = cache_k_win.shape[2]
    assert win == WINDOW and seq % WINDOW == 0 and bs % SAMPLE_BATCHES_PER_STEP == 0
    assert dec_seq & (dec_seq - 1) == 0

    n_c = bp + bs
    c_rows = -(-n_c // 16) * 16
    c_all = jnp.concatenate([c_prompt, c_sample, jnp.zeros((c_rows - n_c, d), F32)], axis=0)
    mods = _ada(c_all, w_ada, b_ada)

    def group_mods(l):
        parts = jnp.split(mods[l], 6, axis=-1)
        mp = [p[:bp].reshape(bp, 1, d) for p in parts]
        ms = [jnp.repeat(p[bp:n_c], dec_seq, axis=0).reshape(1, bs * dec_seq, d) for p in parts]
        return mp, ms

    wi = w_in.astype(BF16)
    wo = w_out.astype(BF16)
    wu = w_up.astype(BF16)
    wd = w_down.astype(BF16)

    cos_p, sin_p = _rope_tables(jnp.arange(seq, dtype=jnp.int32))
    cos_s, sin_s = _rope_tables(PAST_LEN + jnp.arange(dec_seq, dtype=jnp.int32))
    cos_s = jnp.tile(cos_s, (bs, 1))
    sin_s = jnp.tile(sin_s, (bs, 1))
    tabs_p = _decay_tables(n_ret, WINDOW, 1, WINDOW)
    tabs_s = _decay_tables(n_ret, dec_seq, SAMPLE_BATCHES_PER_STEP, WINDOW)

    tm_p = min(512, seq)
    tm_down = min(256, seq)
    ms_rows = bs * dec_seq
    nblk = seq // WINDOW
    xp = x_prompt
    xs = x_sample.reshape(1, ms_rows, d)

    mods_p, mods_s = group_mods(0)
    hp = _prenorm(xp, norm1_w[0:1], mods_p[1], mods_p[0], tm_p)
    hs = _prenorm(xs, norm1_w[0:1], mods_s[1], mods_s[0], ms_rows)

    outs = {k: [] for k in ("rp", "kp", "vp")}
    sample_state = None
    for l in range(depth):
        qn = q_norm_w[l:l + 1]
        kn = k_norm_w[l:l + 1]
        if l + 1 < depth:
            nxt_p, nxt_s = group_mods(l + 1)
            norm_p = (norm1_w[l + 1:l + 2], nxt_p[1], nxt_p[0])
            norm_s = (norm1_w[l + 1:l + 2], nxt_s[1], nxt_s[0])
        else:
            norm_p = norm_s = None

        hp2 = hp.reshape(bp * seq, d)
        proj = _inproj(hp2, wi, l, cos_p, sin_p, qn, kn, min(1024, seq))
        kv_tail = _kvproj(hp2, wi, l, kn, WINDOW, bp, lambda i: i * nblk + nblk - 1)
        merged, r_new = _mix_prompt(proj, sinks[l], tabs_p, bp, seq, d)
        outs["rp"].append(r_new)
        outs["kp"].append(kv_tail[:, :kvw].reshape(bp, WINDOW, n_kv, HD))
        outs["vp"].append(kv_tail[:, kvw:].reshape(bp, WINDOW, n_kv, HD))
        xp, h2 = _proj_residual(merged, wo, l, xp, mods_p[2], (norm2_w[l:l + 1], mods_p[4], mods_p[3]),
                                tm_p, "outproj")
        u = _up(h2.reshape(bp * seq, d), wu, l, min(1024, seq))
        xp, hp = _proj_residual(u, wd, l, xp, mods_p[5], norm_p, tm_down, "mlp_down")

        hs2 = hs.reshape(ms_rows, d)
        proj = _inproj(hs2, wi, l, cos_s, sin_s, qn, kn, ms_rows)
        kv_new = _kvproj(hs2, wi, l, kn, ms_rows, 1, lambda i: i)
        merged, sample_state = _mix_sample(proj, kv_new, sinks[l], tabs_s, l, state_ret, cache_k_win,
                                           cache_v_win, sample_state, dec_seq, d)
        xs, h2 = _proj_residual(merged, wo, l, xs, mods_s[2], (norm2_w[l:l + 1], mods_s[4], mods_s[3]),
                                ms_rows, "outproj")
        u = _up(h2.reshape(ms_rows, d), wu, l, ms_rows)
        xs, hs = _proj_residual(u, wd, l, xs, mods_s[5], norm_s, min(tm_down // 2, ms_rows), "mlp_down")

        if l + 1 < depth:
            mods_p, mods_s = nxt_p, nxt_s

    return (xp, xs.reshape(bs, dec_seq, d), jnp.stack(outs["rp"]), jnp.stack(outs["kp"]),
            jnp.stack(outs["vp"])) + tuple(sample_state)
```

```python
import functools
import math

import jax
import jax.numpy as jnp
from jax import lax
from jax.experimental import pallas as pl
from jax.experimental.pallas import tpu as pltpu

F32 = jnp.float32
BF16 = jnp.bfloat16

DK = 256
HD = 128
GROUP = 4
WINDOW = 128
ROPE_BASE = 10000.0
EPS = 1e-6
NEG_INF = -1e30
PAST_LEN = 8192
SAMPLE_BATCHES_PER_STEP = 4
INPROJ_SLAB_ROWS = 256
VMEM_LIMIT = 56 * 1024 * 1024


def _cparams(n_axes):
    return pltpu.CompilerParams(dimension_semantics=("arbitrary",) * n_axes,
                                vmem_limit_bytes=VMEM_LIMIT)


def _sigmoid(x):
    return 1.0 / (1.0 + jnp.exp(-x))


def _modnorm(x, lnw, sc, sh):
    y = x * lax.rsqrt(jnp.mean(x * x, axis=-1, keepdims=True) + EPS)
    return (y * lnw) * (1.0 + sc) + sh


def _dot(a, b):
    return jnp.dot(a, b, preferred_element_type=F32)


def _dot_nt(a, b):
    return lax.dot_general(a, b, (((1,), (1,)), ((), ())), preferred_element_type=F32)


def _dot_tn(a, b):
    return lax.dot_general(a, b, (((0,), (0,)), ((), ())), preferred_element_type=F32)


def _ada_kernel(c_ref, w_ref, b_ref, o_ref):
    c = c_ref[...]
    s = (c * _sigmoid(c)).astype(BF16)
    o_ref[...] = _dot(s, w_ref[...].astype(BF16)) + b_ref[...]


def _ada(c_all, w_ada, b_ada):
    depth, d, n6 = w_ada.shape
    rows = c_all.shape[0]
    tn = min(1024, n6)
    return pl.pallas_call(
        _ada_kernel,
        grid=(depth, n6 // tn),
        in_specs=[pl.BlockSpec((rows, d), lambda l, n: (0, 0)),
                  pl.BlockSpec((None, d, tn), lambda l, n: (l, 0, n)),
                  pl.BlockSpec((None, 1, tn), lambda l, n: (l, 0, n))],
        out_specs=pl.BlockSpec((None, rows, tn), lambda l, n: (l, 0, n)),
        out_shape=jax.ShapeDtypeStruct((depth, rows, n6), F32),
        compiler_params=_cparams(2),
        name="ada",
    )(c_all, w_ada, b_ada.reshape(depth, 1, n6))


def _prenorm_kernel(x_ref, lnw_ref, sc_ref, sh_ref, h_ref):
    h_ref[...] = _modnorm(x_ref[...], lnw_ref[...], sc_ref[...], sh_ref[...]).astype(BF16)


def _mod_spec(mod, tm):
    d = mod.shape[-1]
    if mod.shape[1] == 1:
        return pl.BlockSpec((None, 1, d), lambda g, r, *_: (g, 0, 0))
    return pl.BlockSpec((None, tm, d), lambda g, r, *_: (g, r, 0))


def _prenorm(x, lnw, sc, sh, tm):
    g, r, d = x.shape
    return pl.pallas_call(
        _prenorm_kernel,
        grid=(g, r // tm),
        in_specs=[pl.BlockSpec((None, tm, d), lambda i, j: (i, j, 0)),
                  pl.BlockSpec((1, d), lambda i, j: (0, 0)),
                  _mod_spec(sc, tm), _mod_spec(sh, tm)],
        out_specs=pl.BlockSpec((None, tm, d), lambda i, j: (i, j, 0)),
        out_shape=jax.ShapeDtypeStruct((g, r, d), BF16),
        compiler_params=_cparams(2),
        name="prenorm",
    )(x, lnw, sc, sh)


def _rms_heads(a, w):
    outs = []
    for j in range(a.shape[1] // HD):
        t = a[:, j * HD:(j + 1) * HD]
        outs.append(t * lax.rsqrt(jnp.mean(t * t, axis=-1, keepdims=True) + EPS) * w)
    return outs


def _cast_weight_tile(w_ref, rest, first_step):
    wout_ref, wbf_ref = rest

    @pl.when(first_step)
    def _cast():
        wb = w_ref[...].astype(BF16)
        wbf_ref[...] = wb
        wout_ref[...] = wb

    return wbf_ref


def _inproj_kernel(h_ref, w_ref, cos_ref, sin_ref, qn_ref, kn_ref, o_ref, *rest, tn, cast):
    n = pl.program_id(0)
    if cast:
        w_ref = _cast_weight_tile(w_ref, rest, pl.program_id(1) == 0)
    half = DK // 2
    tm = h_ref.shape[0]
    tr = min(tm, INPROJ_SLAB_ROWS)
    slabs = [(slice(r, r + tr), j) for j in range(tn // DK) for r in range(0, tm, tr)]
    full_slabs = [(slice(0, tm), j) for j in range(tn // DK)]

    def slab(rows, j):
        return _dot(h_ref[rows, :], w_ref[:, j * DK:(j + 1) * DK])

    @pl.when(n < 4)
    def _rotary():
        scale = jnp.where(n < 2, 1.0, DK ** -0.5).astype(F32)
        for rows, j in full_slabs:
            a = slab(rows, j)
            cos = cos_ref[rows, :]
            sin = sin_ref[rows, :]
            t1 = a[:, :half]
            t2 = a[:, half:]
            o_ref[rows, j * DK:j * DK + half] = ((t1 * cos - t2 * sin) * scale).astype(BF16)
            o_ref[rows, j * DK + half:(j + 1) * DK] = ((t1 * sin + t2 * cos) * scale).astype(BF16)

    def _rms_slabs(w, n_chunks):
        for rows, j in slabs:
            if j < n_chunks:
                for i, y in enumerate(_rms_heads(slab(rows, j), w)):
                    o_ref[rows, j * DK + i * HD:j * DK + (i + 1) * HD] = y.astype(BF16)

    @pl.when((n == 8) | (n == 9))
    def _qnorm():
        _rms_slabs(qn_ref[...], tn // DK)

    @pl.when(n == 10)
    def _kv():
        if tn // 2 >= DK:
            _rms_slabs(kn_ref[...], tn // (2 * DK))
        else:
            a = _dot(h_ref[...], w_ref[:, :tn // 2])
            for i, y in enumerate(_rms_heads(a, kn_ref[...])):
                o_ref[:, i * HD:(i + 1) * HD] = y.astype(BF16)
        o_ref[:, tn // 2:] = _dot(h_ref[...], w_ref[:, tn // 2:]).astype(BF16)

    @pl.when((n == 4) | (n == 5))
    def _plain():
        o_ref[...] = _dot(h_ref[...], w_ref[...]).astype(BF16)

    @pl.when((n == 6) | (n == 7))
    def _silu_gate():
        for rows, j in slabs:
            a = slab(rows, j)
            o_ref[rows, j * DK:(j + 1) * DK] = (a * _sigmoid(a)).astype(BF16)

    @pl.when(n > 10)
    def _sigmoid_gate():
        for rows, j in slabs:
            o_ref[rows, j * DK:(j + 1) * DK] = _sigmoid(slab(rows, j)).astype(BF16)


def _weight_specs(w, layer, k, tn, cast):
    if not cast:
        return pl.BlockSpec((k, tn), lambda n, i: (0, n)), [], [], []
    spec = pl.BlockSpec((None, k, tn), lambda n, i: (layer, 0, n))
    return (spec, [pl.BlockSpec((k, tn), lambda n, i: (0, n))],
            [jax.ShapeDtypeStruct((k, w.shape[2]), BF16)], [pltpu.VMEM((k, tn), BF16)])


def _inproj(h, w, layer, cos, sin, qn, kn, tm):
    m, d = h.shape
    cast = w.ndim == 3
    width = w.shape[-1]
    tn = d // 2
    nblk = cos.shape[0] // tm
    w_spec, extra_out_specs, extra_out_shape, scratch = _weight_specs(w, layer, d, tn, cast)
    out = pl.pallas_call(
        functools.partial(_inproj_kernel, tn=tn, cast=cast),
        grid=(width // tn, m // tm),
        in_specs=[pl.BlockSpec((tm, d), lambda n, i: (i, 0)),
                  w_spec,
                  pl.BlockSpec((tm, HD), lambda n, i: (i % nblk, 0)),
                  pl.BlockSpec((tm, HD), lambda n, i: (i % nblk, 0)),
                  pl.BlockSpec((1, HD), lambda n, i: (0, 0)),
                  pl.BlockSpec((1, HD), lambda n, i: (0, 0))],
        out_specs=[pl.BlockSpec((tm, tn), lambda n, i: (i, n))] + extra_out_specs,
        out_shape=[jax.ShapeDtypeStruct((m, width), BF16)] + extra_out_shape,
        scratch_shapes=scratch,
        compiler_params=_cparams(2),
        name="inproj",
    )(h, w, cos, sin, qn, kn)
    return (out[0], out[1]) if cast else (out[0], None)


def _kvproj_kernel(h_ref, w_ref, kn_ref, o_ref):
    kvw = o_ref.shape[1] // 2
    a = _dot(h_ref[...], w_ref[...])
    for i, y in enumerate(_rms_heads(a[:, :kvw], kn_ref[...])):
        o_ref[:, i * HD:(i + 1) * HD] = y
    o_ref[:, kvw:] = a[:, kvw:]


def _kvproj(h, w, kn, tm, n_blocks, row_block):
    d = h.shape[1]
    kvw = d // 4
    col_block = (5 * d) // (2 * kvw)
    return pl.pallas_call(
        _kvproj_kernel,
        grid=(n_blocks,),
        in_specs=[pl.BlockSpec((tm, d), lambda i: (row_block(i), 0)),
                  pl.BlockSpec((d, 2 * kvw), lambda i: (0, col_block)),
                  pl.BlockSpec((1, HD), lambda i: (0, 0))],
        out_specs=pl.BlockSpec((tm, 2 * kvw), lambda i: (i, 0)),
        out_shape=jax.ShapeDtypeStruct((n_blocks * tm, 2 * kvw), F32),
        compiler_params=_cparams(1),
        name="kvproj",
    )(h, w, kn)


def _segment_offsets(d):
    kvw = d // 4
    return dict(q=0, k=d, v=2 * d, g=3 * d, qa=4 * d, ka=5 * d, va=5 * d + kvw,
                mr=5 * d + 2 * kvw, ma=6 * d + 2 * kvw, width=7 * d + 2 * kvw)


def _retention_gate(o, g_act, mr_act):
    on = o * lax.rsqrt(jnp.mean(o * o, axis=-1, keepdims=True) + EPS)
    return mr_act.astype(F32) * (on * g_act.astype(F32))


def _mix_prompt_kernel(sink_ref, sdec_ref, blk_ref, prev_ref, dintra_ref, qdec_ref, kdec_ref,
                       out_ref, s_ref, macc_ref, *, d):
    c = pl.program_id(1)
    off = _segment_offsets(d)
    n_ret = d // DK
    n_kv = d // (HD * GROUP)
    kvw = n_kv * HD
    blk = WINDOW

    @pl.when(c == 0)
    def _init():
        s_ref[...] = jnp.zeros_like(s_ref)

    def seg(name, h, width=DK):
        return blk_ref[:, off[name] + h * width:off[name] + (h + 1) * width]

    heads = range(n_ret)
    att = [_dot_nt(seg("q", h), seg("k", h)) for h in heads]
    inter = [_dot(seg("q", h), s_ref[h].astype(BF16)) for h in heads]
    upd = [_dot_tn((seg("k", h).astype(F32) * kdec_ref[h]).astype(BF16), seg("v", h)) for h in heads]
    for h in heads:
        s_ref[h] = s_ref[h] * sdec_ref[h] + upd[h]
    att = [(att[h] * dintra_ref[h]).astype(BF16) for h in heads]
    o_ret = [_dot(att[h], seg("v", h)) + inter[h] * qdec_ref[h] for h in heads]
    for h in heads:
        macc_ref[:, h * DK:(h + 1) * DK] = _retention_gate(o_ret[h], seg("g", h), seg("mr", h))

    qi = lax.broadcasted_iota(jnp.int32, (blk, 2 * blk), 0)
    kj = lax.broadcasted_iota(jnp.int32, (blk, 2 * blk), 1)
    diff = kj - qi
    jmin = jnp.where(c == 0, blk, 0)
    mask = (diff >= 0) & (diff <= WINDOW) & (kj >= jmin)
    scale = HD ** -0.5
    kvs = range(n_kv)
    ones = jnp.ones((2 * blk, HD), BF16)
    scores, values = [], []
    for kv in kvs:
        qs = jnp.concatenate([seg("qa", kv * GROUP + g, HD) for g in range(GROUP)], axis=0)
        k2 = jnp.concatenate([prev_ref[:, kv * HD:(kv + 1) * HD], seg("ka", kv, HD)], axis=0)
        v2 = jnp.concatenate([prev_ref[:, kvw + kv * HD:kvw + (kv + 1) * HD], seg("va", kv, HD)], axis=0)
        scores.append(_dot_nt(qs, k2))
        values.append(jnp.concatenate([v2, ones], axis=1))
    probs, sink_terms = [], []
    for kv in kvs:
        p_rows = []
        for g in range(GROUP):
            sink = sink_ref[kv * GROUP + g]
            s = jnp.where(mask, scores[kv][g * blk:(g + 1) * blk] * scale, NEG_INF)
            m = jnp.maximum(jnp.max(s, axis=-1, keepdims=True), sink)
            p_rows.append(jnp.exp(s - m).astype(BF16))
            sink_terms.append(jnp.exp(sink - m))
        probs.append(jnp.concatenate(p_rows, axis=0))
    pv = [_dot(probs[kv], values[kv]) for kv in kvs]
    for kv in kvs:
        for g in range(GROUP):
            head = kv * GROUP + g
            hs = slice(head * HD, (head + 1) * HD)
            o = pv[kv][g * blk:(g + 1) * blk, :HD]
            denom = pv[kv][g * blk:(g + 1) * blk, HD:] + sink_terms[head]
            out_ref[:, hs] = (macc_ref[:, hs] + seg("ma", head, HD).astype(F32) * (o / denom)).astype(BF16)


def _mix_prompt(proj, sinks, tabs, b, seq, d):
    off = _segment_offsets(d)
    n_ret = d // DK
    kvw = d // 4
    nblk = seq // WINDOW
    proj3 = proj.reshape(b, seq, off["width"])
    kv_col_block = off["ka"] // (2 * kvw)
    smem = pl.BlockSpec(memory_space=pltpu.SMEM)
    const3 = lambda shape: pl.BlockSpec(shape, lambda i, c: (0, 0, 0))
    merged, state = pl.pallas_call(
        functools.partial(_mix_prompt_kernel, d=d),
        grid=(b, nblk),
        in_specs=[smem, smem,
                  pl.BlockSpec((None, WINDOW, off["width"]), lambda i, c: (i, c, 0)),
                  pl.BlockSpec((None, WINDOW, 2 * kvw),
                               lambda i, c: (i, jnp.maximum(c - 1, 0), kv_col_block)),
                  const3((n_ret, WINDOW, WINDOW)), const3((n_ret, WINDOW, DK)),
                  const3((n_ret, WINDOW, DK))],
        out_specs=[pl.BlockSpec((None, WINDOW, d), lambda i, c: (i, c, 0)),
                   pl.BlockSpec((None, n_ret, DK, DK), lambda i, c: (i, 0, 0, 0))],
        out_shape=[jax.ShapeDtypeStruct((b, seq, d), BF16),
                   jax.ShapeDtypeStruct((b, n_ret, DK, DK), F32)],
        scratch_shapes=[pltpu.VMEM((WINDOW, d), F32)],
        compiler_params=_cparams(2),
        name="mix_prompt",
    )(sinks, tabs["sdec"], proj3, proj3, tabs["dintra"], tabs["qdec"], tabs["kdec"])
    return merged.reshape(b * seq, d), state


def _mix_sample_kernel(sink_ref, sdec_ref, blk_ref, kvnew_ref, s0_ref, kc_ref, vc_ref, dmask_ref,
                       qdec_ref, kdec_ref, *rest, d, dec_seq, n_alias):
    out_ref, s_out_ref, kout_ref, vout_ref, pad_ref, macc_ref = rest[n_alias:]
    off = _segment_offsets(d)
    n_ret = d // DK
    n_kv = d // (HD * GROUP)
    kvw = n_kv * HD
    nb = SAMPLE_BATCHES_PER_STEP
    rows = nb * dec_seq
    pad = pad_ref.shape[0]
    p_k, p_v, p_ka, p_va = 0, d, 2 * d, 2 * d + kvw

    @pl.when(pl.program_id(0) == 0)
    def _init():
        pad_ref[...] = jnp.zeros_like(pad_ref)

    pad_ref[0:rows, p_k:p_k + 2 * d] = blk_ref[:, off["k"]:off["k"] + 2 * d]
    pad_ref[0:rows, p_ka:p_ka + 2 * kvw] = blk_ref[:, off["ka"]:off["ka"] + 2 * kvw]

    tok_bits = dec_seq.bit_length() - 1
    row_batch = lax.broadcasted_iota(jnp.int32, (rows, 1), 0) >> tok_bits
    pad_batch = lax.broadcasted_iota(jnp.int32, (pad, 1), 0) >> tok_bits

    def seg(name, h, width=DK):
        return blk_ref[:, off[name] + h * width:off[name] + (h + 1) * width]

    def padded(col0, h, width=DK):
        return pad_ref[:, col0 + h * width:col0 + (h + 1) * width]

    heads = range(n_ret)
    bis = range(nb)
    att = [_dot_nt(seg("q", h), padded(p_k, h)) for h in heads]
    inter = [[_dot(seg("q", h), s0_ref[bi, h].astype(BF16)) for bi in bis] for h in heads]
    kd = [(padded(p_k, h).astype(F32) * kdec_ref[h]).astype(BF16) for h in heads]
    v_b = [[jnp.where(pad_batch == bi, padded(p_v, h), jnp.zeros((pad, DK), BF16)) for bi in bis]
           for h in heads]
    upd = [[_dot_tn(kd[h], v_b[h][bi]) for bi in bis] for h in heads]
    for h in heads:
        for bi in bis:
            s_out_ref[bi, h] = s0_ref[bi, h] * sdec_ref[h] + upd[h][bi]
    att = [(att[h] * dmask_ref[h]).astype(BF16) for h in heads]
    for h in heads:
        own = inter[h][0]
        for bi in bis[1:]:
            own = jnp.where(row_batch == bi, inter[h][bi], own)
        o = _dot(att[h], padded(p_v, h)) + own * qdec_ref[h]
        macc_ref[:, h * DK:(h + 1) * DK] = _retention_gate(o, seg("g", h), seg("mr", h))

    qrows = GROUP * rows
    q_tok = lax.broadcasted_iota(jnp.int32, (qrows, 1), 0) & (dec_seq - 1)
    q_batch = (lax.broadcasted_iota(jnp.int32, (qrows, 1), 0) & (rows - 1)) >> tok_bits
    cache_j = lax.broadcasted_iota(jnp.int32, (qrows, WINDOW), 1)
    new_j = lax.broadcasted_iota(jnp.int32, (qrows, pad), 1)
    mask_cache = cache_j >= q_tok
    mask_new = [((new_j >> tok_bits) == bi) & ((new_j & (dec_seq - 1)) <= q_tok) for bi in bis]
    scale = HD ** -0.5
    kvs = range(n_kv)
    ones = jnp.ones((WINDOW, HD), BF16)
    qs = [jnp.concatenate([seg("qa", kv * GROUP + g, HD) for g in range(GROUP)], axis=0) for kv in kvs]
    s_new = [_dot_nt(qs[kv], padded(p_ka, kv, HD)) for kv in kvs]
    s_cache = [[_dot_nt(qs[kv], kc_ref[bi, :, kv, :].astype(BF16)) for bi in bis] for kv in kvs]
    v_new = [jnp.concatenate([padded(p_va, kv, HD), ones], axis=1) for kv in kvs]
    v_cache = [[jnp.concatenate([vc_ref[bi, :, kv, :].astype(BF16), ones], axis=1) for bi in bis]
               for kv in kvs]
    for kv in kvs:
        sink_col = jnp.concatenate(
            [jnp.full((rows, 1), sink_ref[kv * GROUP + g], F32) for g in range(GROUP)], axis=0)
        o_kv = None
        for bi in bis:
            s_c = jnp.where(mask_cache, s_cache[kv][bi] * scale, NEG_INF)
            s_n = jnp.where(mask_new[bi], s_new[kv] * scale, NEG_INF)
            m = jnp.maximum(jnp.maximum(jnp.max(s_c, axis=-1, keepdims=True),
                                        jnp.max(s_n, axis=-1, keepdims=True)), sink_col)
            pv = (_dot(jnp.exp(s_c - m).astype(BF16), v_cache[kv][bi])
                  + _dot(jnp.exp(s_n - m).astype(BF16), v_new[kv]))
            o = pv[:, :HD] / (pv[:, HD:] + jnp.exp(sink_col - m))
            o_kv = o if o_kv is None else jnp.where(q_batch == bi, o, o_kv)
        for g in range(GROUP):
            head = kv * GROUP + g
            hs = slice(head * HD, (head + 1) * HD)
            out_ref[:, hs] = (macc_ref[:, hs] + seg("ma", head, HD).astype(F32)
                              * o_kv[g * rows:(g + 1) * rows]).astype(BF16)

    keep = WINDOW - dec_seq
    for bi in range(nb):
        kout_ref[bi, 0:keep] = kc_ref[bi, dec_seq:WINDOW]
        vout_ref[bi, 0:keep] = vc_ref[bi, dec_seq:WINDOW]
        for t in range(dec_seq):
            r = bi * dec_seq + t
            for kv in range(n_kv):
                kout_ref[bi, keep + t, kv:kv + 1, :] = kvnew_ref[r:r + 1, kv * HD:(kv + 1) * HD]
                vout_ref[bi, keep + t, kv:kv + 1, :] = kvnew_ref[r:r + 1, kvw + kv * HD:kvw + (kv + 1) * HD]


def _mix_sample(proj, kvnew, sinks, tabs, layer, state_all, k_all, v_all, prev_outs, dec_seq, d):
    off = _segment_offsets(d)
    n_ret = d // DK
    n_kv = d // (HD * GROUP)
    kvw = n_kv * HD
    nb = SAMPLE_BATCHES_PER_STEP
    b = state_all.shape[1]
    rows = nb * dec_seq
    smem = pl.BlockSpec(memory_space=pltpu.SMEM)
    const3 = lambda shape: pl.BlockSpec(shape, lambda i: (0, 0, 0))
    state_spec = lambda: pl.BlockSpec((None, nb, n_ret, DK, DK), lambda i: (layer, i, 0, 0, 0))
    win_spec = lambda: pl.BlockSpec((None, nb, WINDOW, n_kv, HD), lambda i: (layer, i, 0, 0, 0))
    in_specs = [smem, smem,
                pl.BlockSpec((rows, off["width"]), lambda i: (i, 0)),
                pl.BlockSpec((rows, 2 * kvw), lambda i: (i, 0)),
                state_spec(), win_spec(), win_spec(),
                const3((n_ret, rows, WINDOW)), const3((n_ret, rows, DK)), const3((n_ret, WINDOW, DK))]
    args = [sinks, tabs["sdec"], proj, kvnew, state_all, k_all, v_all,
            tabs["dmask"], tabs["qdec"], tabs["kdec"]]
    aliases = {}
    if prev_outs is not None:
        for j, arr in enumerate(prev_outs):
            aliases[len(args)] = 1 + j
            in_specs.append(pl.BlockSpec(memory_space=pl.ANY))
            args.append(arr)
    n_alias = len(aliases)
    merged, state, kwin, vwin = pl.pallas_call(
        functools.partial(_mix_sample_kernel, d=d, dec_seq=dec_seq, n_alias=n_alias),
        grid=(b // nb,),
        in_specs=in_specs,
        out_specs=[pl.BlockSpec((rows, d), lambda i: (i, 0)), state_spec(), win_spec(), win_spec()],
        out_shape=[jax.ShapeDtypeStruct((b * dec_seq, d), BF16),
                   jax.ShapeDtypeStruct(state_all.shape, F32),
                   jax.ShapeDtypeStruct(k_all.shape, F32),
                   jax.ShapeDtypeStruct(v_all.shape, F32)],
        scratch_shapes=[pltpu.VMEM((WINDOW, 2 * d + 2 * kvw), BF16),
                        pltpu.VMEM((rows, d), F32)],
        input_output_aliases=aliases,
        compiler_params=_cparams(1),
        name="mix_sample",
    )(*args)
    return merged, (state, kwin, vwin)


def _proj_residual_kernel(a_ref, w_ref, x_ref, g_ref, *rest, with_norm):
    if with_norm:
        lnw_ref, sc_ref, sh_ref, xo_ref, ho_ref = rest
    else:
        (xo_ref,) = rest
    d = xo_ref.shape[1]
    half = DK // 2
    slabs = [slice(j * DK, (j + 1) * DK) for j in range(d // DK)]
    sq = None
    for cs in slabs:
        x = x_ref[:, cs] + g_ref[:, cs] * _dot(a_ref[...], w_ref[:, cs])
        xo_ref[:, cs] = x
        if with_norm:
            x2 = x * x
            part = x2[:, :half] + x2[:, half:]
            sq = part if sq is None else sq + part
    if with_norm:
        r = lax.rsqrt(jnp.sum(sq, axis=-1, keepdims=True) / d + EPS)
        for cs in slabs:
            y = (xo_ref[:, cs] * r) * lnw_ref[:, cs]
            ho_ref[:, cs] = (y * (1.0 + sc_ref[:, cs]) + sh_ref[:, cs]).astype(BF16)


def _proj_residual(a, w, layer, x, gate, norm, tm, name):
    g, r, d = x.shape
    kdim = w.shape[1]
    tok = lambda: pl.BlockSpec((None, tm, d), lambda i, j: (i, j, 0))
    in_specs = [pl.BlockSpec((None, tm, kdim), lambda i, j: (i, j, 0)),
                pl.BlockSpec((None, kdim, d), lambda i, j: (layer, 0, 0), pipeline_mode=pl.Buffered(1)),
                tok(), _mod_spec(gate, tm)]
    args = [a.reshape(g, r, kdim), w, x, gate]
    out_specs = [tok()]
    out_shape = [jax.ShapeDtypeStruct((g, r, d), F32)]
    if norm is not None:
        lnw, sc, sh = norm
        in_specs += [pl.BlockSpec((1, d), lambda i, j: (0, 0)), _mod_spec(sc, tm), _mod_spec(sh, tm)]
        args += [lnw, sc, sh]
        out_specs.append(tok())
        out_shape.append(jax.ShapeDtypeStruct((g, r, d), BF16))
    out = pl.pallas_call(
        functools.partial(_proj_residual_kernel, with_norm=norm is not None),
        grid=(g, r // tm),
        in_specs=in_specs, out_specs=out_specs, out_shape=out_shape,
        compiler_params=_cparams(2),
        name=name,
    )(*args)
    return (out[0], out[1]) if norm is not None else (out[0], None)


def _up_kernel(h_ref, w_ref, o_ref, *rest, cast):
    if cast:
        w_ref = _cast_weight_tile(w_ref, rest, pl.program_id(1) == 0)
    r = jnp.maximum(_dot(h_ref[...], w_ref[...]), 0.0)
    o_ref[...] = (r * r).astype(BF16)


def _up(h, w, layer, tm):
    m, d = h.shape
    cast = w.ndim == 3
    f = w.shape[-1]
    tn = min(1024, f)
    w_spec, extra_out_specs, extra_out_shape, scratch = _weight_specs(w, layer, d, tn, cast)
    out = pl.pallas_call(
        functools.partial(_up_kernel, cast=cast),
        grid=(f // tn, m // tm),
        in_specs=[pl.BlockSpec((tm, d), lambda n, i: (i, 0)), w_spec],
        out_specs=[pl.BlockSpec((tm, tn), lambda n, i: (i, n))] + extra_out_specs,
        out_shape=[jax.ShapeDtypeStruct((m, f), BF16)] + extra_out_shape,
        scratch_shapes=scratch,
        compiler_params=_cparams(2),
        name="mlp_up",
    )(h, w)
    return (out[0], out[1]) if cast else (out[0], None)


def _rope_tables(pos):
    half = DK // 2
    inv = 1.0 / (ROPE_BASE ** (jnp.arange(half, dtype=F32) / half))
    ang = pos.astype(F32)[:, None] * inv[None, :]
    return jnp.cos(ang), jnp.sin(ang)


def _decay_tables(n_ret, chunk, reps, pad_cols):
    log_g = jnp.log1p(-jnp.exp2(-5.0 - jnp.arange(n_ret, dtype=F32)))
    rows = reps * chunk
    r = jnp.arange(rows)
    cidx = jnp.arange(pad_cols)
    idx = (r % chunk).astype(F32)
    diff = idx[:, None] - (cidx % chunk).astype(F32)[None, :]
    ok = (diff >= 0) & ((r // chunk)[:, None] == (cidx // chunk)[None, :]) & (cidx < rows)[None, :]
    dintra = jnp.where(ok, jnp.exp(jnp.where(ok, diff, 0.0) * log_g[:, None, None]), 0.0)
    qdec = jnp.exp((idx + 1.0) * log_g[:, None])[..., None]
    kidx = (jnp.arange(max(rows, pad_cols)) % chunk).astype(F32)
    kdec = jnp.exp((chunk - 1.0 - kidx) * log_g[:, None])[..., None]
    sdec = jnp.exp(chunk * log_g)
    return dict(dintra=dintra, dmask=dintra,
                qdec=jnp.broadcast_to(qdec, (n_ret, rows, DK)),
                kdec=jnp.broadcast_to(kdec, (n_ret, kidx.shape[0], DK)), sdec=sdec)


def kernel(x_prompt, x_sample, c_prompt, c_sample, state_ret, cache_k_win, cache_v_win, norm1_w,
           norm2_w, w_ada, b_ada, w_in, q_norm_w, k_norm_w, sinks, w_out, w_up, w_down):
    bp, seq, d = x_prompt.shape
    bs, dec_seq, _ = x_sample.shape
    depth = w_in.shape[0]
    n_ret = d // DK
    kvw = d // 4
    n_kv = kvw // HD
    win = cache_k_win.shape[2]
    assert win == WINDOW and seq % WINDOW == 0 and bs % SAMPLE_BATCHES_PER_STEP == 0
    assert dec_seq & (dec_seq - 1) == 0

    n_c = bp + bs
    c_rows = -(-n_c // 16) * 16
    c_all = jnp.concatenate([c_prompt, c_sample, jnp.zeros((c_rows - n_c, d), F32)], axis=0)
    mods = _ada(c_all, w_ada, b_ada)

    def group_mods(l):
        parts = jnp.split(mods[l], 6, axis=-1)
        mp = [p[:bp].reshape(bp, 1, d) for p in parts]
        ms = [jnp.repeat(p[bp:n_c], dec_seq, axis=0).reshape(1, bs * dec_seq, d) for p in parts]
        return mp, ms

    wo = w_out.astype(BF16)
    wd = w_down.astype(BF16)

    cos_p, sin_p = _rope_tables(jnp.arange(seq, dtype=jnp.int32))
    cos_s, sin_s = _rope_tables(PAST_LEN + jnp.arange(dec_seq, dtype=jnp.int32))
    cos_s = jnp.tile(cos_s, (bs, 1))
    sin_s = jnp.tile(sin_s, (bs, 1))
    tabs_p = _decay_tables(n_ret, WINDOW, 1, WINDOW)
    tabs_s = _decay_tables(n_ret, dec_seq, SAMPLE_BATCHES_PER_STEP, WINDOW)

    tm_p = min(512, seq)
    tm_down = min(256, seq)
    ms_rows = bs * dec_seq
    nblk = seq // WINDOW
    xp = x_prompt
    xs = x_sample.reshape(1, ms_rows, d)

    mods_p, mods_s = group_mods(0)
    hp = _prenorm(xp, norm1_w[0:1], mods_p[1], mods_p[0], tm_p)
    hs = _prenorm(xs, norm1_w[0:1], mods_s[1], mods_s[0], ms_rows)

    outs = {k: [] for k in ("rp", "kp", "vp")}
    sample_state = None
    for l in range(depth):
        qn = q_norm_w[l:l + 1]
        kn = k_norm_w[l:l + 1]
        if l + 1 < depth:
            nxt_p, nxt_s = group_mods(l + 1)
            norm_p = (norm1_w[l + 1:l + 2], nxt_p[1], nxt_p[0])
            norm_s = (norm1_w[l + 1:l + 2], nxt_s[1], nxt_s[0])
        else:
            norm_p = norm_s = None

        hp2 = hp.reshape(bp * seq, d)
        proj, wi = _inproj(hp2, w_in, l, cos_p, sin_p, qn, kn, min(1024, seq))
        kv_tail = _kvproj(hp2, wi, kn, WINDOW, bp, lambda i: i * nblk + nblk - 1)
        merged, r_new = _mix_prompt(proj, sinks[l], tabs_p, bp, seq, d)
        outs["rp"].append(r_new)
        outs["kp"].append(kv_tail[:, :kvw].reshape(bp, WINDOW, n_kv, HD))
        outs["vp"].append(kv_tail[:, kvw:].reshape(bp, WINDOW, n_kv, HD))
        xp, h2 = _proj_residual(merged, wo, l, xp, mods_p[2], (norm2_w[l:l + 1], mods_p[4], mods_p[3]),
                                tm_p, "outproj")
        u, wu = _up(h2.reshape(bp * seq, d), w_up, l, min(1024, seq))
        xp, hp = _proj_residual(u, wd, l, xp, mods_p[5], norm_p, tm_down, "mlp_down")

        hs2 = hs.reshape(ms_rows, d)
        proj, _ = _inproj(hs2, wi, l, cos_s, sin_s, qn, kn, ms_rows)
        kv_new = _kvproj(hs2, wi, kn, ms_rows, 1, lambda i: i)
        merged, sample_state = _mix_sample(proj, kv_new, sinks[l], tabs_s, l, state_ret, cache_k_win,
                                           cache_v_win, sample_state, dec_seq, d)
        xs, h2 = _proj_residual(merged, wo, l, xs, mods_s[2], (norm2_w[l:l + 1], mods_s[4], mods_s[3]),
                                ms_rows, "outproj")
        u, _ = _up(h2.reshape(ms_rows, d), wu, l, ms_rows)
        xs, hs = _proj_residual(u, wd, l, xs, mods_s[5], norm_s, min(tm_down // 2, ms_rows), "mlp_down")

        if l + 1 < depth:
            mods_p, mods_s = nxt_p, nxt_s

    return (xp, xs.reshape(bs, dec_seq, d), jnp.stack(outs["rp"]), jnp.stack(outs["kp"]),
            jnp.stack(outs["vp"])) + tuple(sample_state)
```

```python
import functools
import math

import jax
import jax.numpy as jnp
from jax import lax
from jax.experimental import pallas as pl
from jax.experimental.pallas import tpu as pltpu

F32 = jnp.float32
BF16 = jnp.bfloat16

DK = 256
HD = 128
GROUP = 4
WINDOW = 128
ROPE_BASE = 10000.0
EPS = 1e-6
NEG_INF = -1e30
LOG2E = math.log2(math.e)
PAST_LEN = 8192
SAMPLE_BATCHES_PER_STEP = 4
INPROJ_SLAB_ROWS = 256
VMEM_LIMIT = 56 * 1024 * 1024


def _cparams(n_axes):
    return pltpu.CompilerParams(dimension_semantics=("arbitrary",) * n_axes,
                                vmem_limit_bytes=VMEM_LIMIT)


def _sigmoid(x):
    return 1.0 / (1.0 + jnp.exp(-x))


def _modnorm(x, lnw, sc, sh):
    y = x * lax.rsqrt(jnp.mean(x * x, axis=-1, keepdims=True) + EPS)
    return (y * lnw) * (1.0 + sc) + sh


def _dot(a, b):
    return jnp.dot(a, b, preferred_element_type=F32)


def _dot_nt(a, b):
    return lax.dot_general(a, b, (((1,), (1,)), ((), ())), preferred_element_type=F32)


def _dot_tn(a, b):
    return lax.dot_general(a, b, (((0,), (0,)), ((), ())), preferred_element_type=F32)


def _ada_kernel(c_ref, w_ref, b_ref, o_ref):
    c = c_ref[...]
    s = (c * _sigmoid(c)).astype(BF16)
    o_ref[...] = _dot(s, w_ref[...].astype(BF16)) + b_ref[...]


def _ada(c_all, w_ada, b_ada):
    depth, d, n6 = w_ada.shape
    rows = c_all.shape[0]
    tn = min(1024, n6)
    return pl.pallas_call(
        _ada_kernel,
        grid=(depth, n6 // tn),
        in_specs=[pl.BlockSpec((rows, d), lambda l, n: (0, 0)),
                  pl.BlockSpec((None, d, tn), lambda l, n: (l, 0, n)),
                  pl.BlockSpec((None, 1, tn), lambda l, n: (l, 0, n))],
        out_specs=pl.BlockSpec((None, rows, tn), lambda l, n: (l, 0, n)),
        out_shape=jax.ShapeDtypeStruct((depth, rows, n6), F32),
        compiler_params=_cparams(2),
        name="ada",
    )(c_all, w_ada, b_ada.reshape(depth, 1, n6))


def _prenorm_kernel(x_ref, lnw_ref, sc_ref, sh_ref, h_ref):
    h_ref[...] = _modnorm(x_ref[...], lnw_ref[...], sc_ref[...], sh_ref[...]).astype(BF16)


class _Mod:
    def __init__(self, arr, layer, part, groups):
        self.arr, self.layer, self.part, self.groups = arr, layer, part, groups

    def spec(self, tm):
        l, j, arr = self.layer, self.part, self.arr
        if arr.shape[1] == 1:
            base = l * self.groups * 6 + j
            return pl.BlockSpec((None, 1, arr.shape[2]), lambda g, r: (base + g * 6, 0, 0))
        return pl.BlockSpec((None, tm, arr.shape[2] // 6), lambda g, r: (l, r, j))


def _mod_spec(mod, tm):
    return mod.spec(tm)


def _prenorm(x, lnw, sc, sh, tm):
    g, r, d = x.shape
    return pl.pallas_call(
        _prenorm_kernel,
        grid=(g, r // tm),
        in_specs=[pl.BlockSpec((None, tm, d), lambda i, j: (i, j, 0)),
                  pl.BlockSpec((1, d), lambda i, j: (0, 0)),
                  _mod_spec(sc, tm), _mod_spec(sh, tm)],
        out_specs=pl.BlockSpec((None, tm, d), lambda i, j: (i, j, 0)),
        out_shape=jax.ShapeDtypeStruct((g, r, d), BF16),
        compiler_params=_cparams(2),
        name="prenorm",
    )(x, lnw, sc.arr, sh.arr)


def _rms_heads(a, w):
    outs = []
    for j in range(a.shape[1] // HD):
        t = a[:, j * HD:(j + 1) * HD]
        outs.append(t * lax.rsqrt(jnp.mean(t * t, axis=-1, keepdims=True) + EPS) * w)
    return outs


def _inproj_kernel(h_ref, w_ref, cos_ref, sin_ref, qn_ref, kn_ref, o_ref, *, tn):
    n = pl.program_id(0)
    half = DK // 2
    tm = h_ref.shape[0]
    tr = min(tm, INPROJ_SLAB_ROWS)
    slabs = [(slice(r, r + tr), j) for j in range(tn // DK) for r in range(0, tm, tr)]
    full_slabs = [(slice(0, tm), j) for j in range(tn // DK)]

    def slab(rows, j):
        return _dot(h_ref[rows, :], w_ref[:, j * DK:(j + 1) * DK])

    @pl.when(n < 4)
    def _rotary():
        scale = jnp.where(n < 2, 1.0, DK ** -0.5).astype(F32)
        for rows, j in full_slabs:
            a = slab(rows, j)
            cos = cos_ref[rows, :]
            sin = sin_ref[rows, :]
            t1 = a[:, :half]
            t2 = a[:, half:]
            o_ref[rows, j * DK:j * DK + half] = ((t1 * cos - t2 * sin) * scale).astype(BF16)
            o_ref[rows, j * DK + half:(j + 1) * DK] = ((t1 * sin + t2 * cos) * scale).astype(BF16)

    def _rms_slabs(w, n_chunks):
        for rows, j in slabs:
            if j < n_chunks:
                for i, y in enumerate(_rms_heads(slab(rows, j), w)):
                    o_ref[rows, j * DK + i * HD:j * DK + (i + 1) * HD] = y.astype(BF16)

    @pl.when((n == 8) | (n == 9))
    def _qnorm():
        _rms_slabs(qn_ref[...], tn // DK)

    @pl.when(n == 10)
    def _kv():
        if tn // 2 >= DK:
            _rms_slabs(kn_ref[...], tn // (2 * DK))
        else:
            a = _dot(h_ref[...], w_ref[:, :tn // 2])
            for i, y in enumerate(_rms_heads(a, kn_ref[...])):
                o_ref[:, i * HD:(i + 1) * HD] = y.astype(BF16)
        o_ref[:, tn // 2:] = _dot(h_ref[...], w_ref[:, tn // 2:]).astype(BF16)

    @pl.when((n == 4) | (n == 5))
    def _plain():
        o_ref[...] = _dot(h_ref[...], w_ref[...]).astype(BF16)

    @pl.when((n == 6) | (n == 7))
    def _silu_gate():
        for rows, j in slabs:
            a = slab(rows, j)
            o_ref[rows, j * DK:(j + 1) * DK] = (a * _sigmoid(a)).astype(BF16)

    @pl.when(n > 10)
    def _sigmoid_gate():
        for rows, j in slabs:
            o_ref[rows, j * DK:(j + 1) * DK] = _sigmoid(slab(rows, j)).astype(BF16)


def _inproj(h, w, layer, cos, sin, qn, kn, tm):
    m, d = h.shape
    width = w.shape[2]
    tn = d // 2
    nblk = cos.shape[0] // tm
    return pl.pallas_call(
        functools.partial(_inproj_kernel, tn=tn),
        grid=(width // tn, m // tm),
        in_specs=[pl.BlockSpec((tm, d), lambda n, i: (i, 0)),
                  pl.BlockSpec((None, d, tn), lambda n, i: (layer, 0, n)),
                  pl.BlockSpec((tm, HD), lambda n, i: (i % nblk, 0)),
                  pl.BlockSpec((tm, HD), lambda n, i: (i % nblk, 0)),
                  pl.BlockSpec((1, HD), lambda n, i: (0, 0)),
                  pl.BlockSpec((1, HD), lambda n, i: (0, 0))],
        out_specs=pl.BlockSpec((tm, tn), lambda n, i: (i, n)),
        out_shape=jax.ShapeDtypeStruct((m, width), BF16),
        compiler_params=_cparams(2),
        name="inproj",
    )(h, w, cos, sin, qn, kn)


def _kvproj_kernel(h_ref, w_ref, kn_ref, o_ref):
    kvw = o_ref.shape[1] // 2
    a = _dot(h_ref[...], w_ref[...])
    for i, y in enumerate(_rms_heads(a[:, :kvw], kn_ref[...])):
        o_ref[:, i * HD:(i + 1) * HD] = y
    o_ref[:, kvw:] = a[:, kvw:]


def _kvproj(h, w, layer, kn, tm, n_blocks, row_block):
    d = h.shape[1]
    kvw = d // 4
    col_block = (5 * d) // (2 * kvw)
    return pl.pallas_call(
        _kvproj_kernel,
        grid=(n_blocks,),
        in_specs=[pl.BlockSpec((tm, d), lambda i: (row_block(i), 0)),
                  pl.BlockSpec((None, d, 2 * kvw), lambda i: (layer, 0, col_block)),
                  pl.BlockSpec((1, HD), lambda i: (0, 0))],
        out_specs=pl.BlockSpec((tm, 2 * kvw), lambda i: (i, 0)),
        out_shape=jax.ShapeDtypeStruct((n_blocks * tm, 2 * kvw), F32),
        compiler_params=_cparams(1),
        name="kvproj",
    )(h, w, kn)


def _segment_offsets(d):
    kvw = d // 4
    return dict(q=0, k=d, v=2 * d, g=3 * d, qa=4 * d, ka=5 * d, va=5 * d + kvw,
                mr=5 * d + 2 * kvw, ma=6 * d + 2 * kvw, width=7 * d + 2 * kvw)


def _retention_gate(o, g_act, mr_act):
    on = o * lax.rsqrt(jnp.mean(o * o, axis=-1, keepdims=True) + EPS)
    return mr_act.astype(F32) * (on * g_act.astype(F32))


def _mix_prompt_kernel(sink_ref, sdec_ref, blk_ref, prev_ref, dintra_ref, qdec_ref, kdec_ref,
                       *rest, d):
    out_ref, s_ref, macc_ref = rest[-3:]
    c = pl.program_id(1)
    off = _segment_offsets(d)
    n_ret = d // DK
    n_kv = d // (HD * GROUP)
    kvw = n_kv * HD
    blk = WINDOW

    @pl.when(c == 0)
    def _init():
        s_ref[...] = jnp.zeros_like(s_ref)

    def seg(name, h, width=DK):
        return blk_ref[:, off[name] + h * width:off[name] + (h + 1) * width]

    qi = lax.broadcasted_iota(jnp.int32, (blk, 2 * blk), 0)
    kj = lax.broadcasted_iota(jnp.int32, (blk, 2 * blk), 1)
    diff = kj - qi
    jmin = jnp.where(c == 0, blk, 0)
    mask = (diff >= 0) & (diff <= WINDOW) & (kj >= jmin)
    scale = HD ** -0.5
    kvs = range(n_kv)
    ones = jnp.ones((2 * blk, HD), BF16)
    scores, values = [], []
    for kv in kvs:
        qs = jnp.concatenate([seg("qa", kv * GROUP + g, HD) for g in range(GROUP)], axis=0)
        k2 = jnp.concatenate([prev_ref[:, kv * HD:(kv + 1) * HD], seg("ka", kv, HD)], axis=0)
        v2 = jnp.concatenate([prev_ref[:, kvw + kv * HD:kvw + (kv + 1) * HD], seg("va", kv, HD)], axis=0)
        scores.append(_dot_nt(qs, k2))
        values.append(jnp.concatenate([v2, ones], axis=1))
    heads = range(n_ret)
    att = [_dot_nt(seg("q", h), seg("k", h)) for h in heads]
    inter = [_dot(seg("q", h), s_ref[h].astype(BF16)) for h in heads]
    upd = [_dot_tn((seg("k", h).astype(F32) * kdec_ref[h]).astype(BF16), seg("v", h)) for h in heads]
    for h in heads:
        s_ref[h] = s_ref[h] * sdec_ref[h] + upd[h]
    att = [(att[h] * dintra_ref[h]).astype(BF16) for h in heads]
    o_ret = [_dot(att[h], seg("v", h)) + inter[h] * qdec_ref[h] for h in heads]
    for h in heads:
        macc_ref[:, h * DK:(h + 1) * DK] = _retention_gate(o_ret[h], seg("g", h), seg("mr", h))

    probs, sink_terms = [], []
    for kv in kvs:
        p_rows = []
        for g in range(GROUP):
            sink = sink_ref[kv * GROUP + g] * (1.0 / scale)
            s = jnp.where(mask, scores[kv][g * blk:(g + 1) * blk], NEG_INF)
            m = jnp.maximum(jnp.max(s, axis=-1, keepdims=True), sink)
            p_rows.append(jnp.exp2((s - m) * (scale * LOG2E)).astype(BF16))
            sink_terms.append(jnp.exp2((sink - m) * (scale * LOG2E)))
        probs.append(jnp.concatenate(p_rows, axis=0))
    pv = [_dot(probs[kv], values[kv]) for kv in kvs]
    for kv in kvs:
        for g in range(GROUP):
            head = kv * GROUP + g
            hs = slice(head * HD, (head + 1) * HD)
            o = pv[kv][g * blk:(g + 1) * blk, :HD]
            denom = pv[kv][g * blk:(g + 1) * blk, HD:] + sink_terms[head]
            out_ref[:, hs] = (macc_ref[:, hs] + seg("ma", head, HD).astype(F32) * (o / denom)).astype(BF16)


def _mix_prompt(proj, sinks, tabs, layer, depth, prev_state, b, seq, d):
    off = _segment_offsets(d)
    n_ret = d // DK
    kvw = d // 4
    nblk = seq // WINDOW
    proj3 = proj.reshape(b, seq, off["width"])
    kv_col_block = off["ka"] // (2 * kvw)
    smem = pl.BlockSpec(memory_space=pltpu.SMEM)
    const3 = lambda shape: pl.BlockSpec(shape, lambda i, c: (0, 0, 0))
    in_specs = [smem, smem,
                pl.BlockSpec((None, WINDOW, off["width"]), lambda i, c: (i, c, 0)),
                pl.BlockSpec((None, WINDOW, 2 * kvw),
                             lambda i, c: (i, jnp.maximum(c - 1, 0), kv_col_block)),
                const3((n_ret, WINDOW, WINDOW)), const3((n_ret, WINDOW, DK)),
                const3((n_ret, WINDOW, DK))]
    args = [sinks, tabs["sdec"], proj3, proj3, tabs["dintra"], tabs["qdec"], tabs["kdec"]]
    aliases = {}
    if prev_state is not None:
        aliases[len(args)] = 1
        in_specs.append(pl.BlockSpec(memory_space=pl.ANY))
        args.append(prev_state)
    merged, state = pl.pallas_call(
        functools.partial(_mix_prompt_kernel, d=d),
        grid=(b, nblk),
        in_specs=in_specs,
        out_specs=[pl.BlockSpec((None, WINDOW, d), lambda i, c: (i, c, 0)),
                   pl.BlockSpec((None, None, n_ret, DK, DK), lambda i, c: (layer, i, 0, 0, 0))],
        out_shape=[jax.ShapeDtypeStruct((b, seq, d), BF16),
                   jax.ShapeDtypeStruct((depth, b, n_ret, DK, DK), F32)],
        scratch_shapes=[pltpu.VMEM((WINDOW, d), F32)],
        input_output_aliases=aliases,
        compiler_params=_cparams(2),
        name="mix_prompt",
    )(*args)
    return merged.reshape(b * seq, d), state


def _mix_sample_kernel(sink_ref, sdec_ref, blk_ref, kvnew_ref, s0_ref, kc_ref, vc_ref, dmask_ref,
                       qdec_ref, kdec_ref, *rest, d, dec_seq, n_alias):
    out_ref, s_out_ref, kout_ref, vout_ref, pad_ref, macc_ref = rest[n_alias:]
    off = _segment_offsets(d)
    n_ret = d // DK
    n_kv = d // (HD * GROUP)
    kvw = n_kv * HD
    nb = SAMPLE_BATCHES_PER_STEP
    rows = nb * dec_seq
    pad = pad_ref.shape[0]
    p_k, p_v, p_ka, p_va = 0, d, 2 * d, 2 * d + kvw

    @pl.when(pl.program_id(0) == 0)
    def _init():
        pad_ref[...] = jnp.zeros_like(pad_ref)

    pad_ref[0:rows, p_k:p_k + 2 * d] = blk_ref[:, off["k"]:off["k"] + 2 * d]
    pad_ref[0:rows, p_ka:p_ka + 2 * kvw] = blk_ref[:, off["ka"]:off["ka"] + 2 * kvw]

    tok_bits = dec_seq.bit_length() - 1
    row_batch = lax.broadcasted_iota(jnp.int32, (rows, 1), 0) >> tok_bits
    pad_batch = lax.broadcasted_iota(jnp.int32, (pad, 1), 0) >> tok_bits

    def seg(name, h, width=DK):
        return blk_ref[:, off[name] + h * width:off[name] + (h + 1) * width]

    def padded(col0, h, width=DK):
        return pad_ref[:, col0 + h * width:col0 + (h + 1) * width]

    heads = range(n_ret)
    bis = range(nb)
    att = [_dot_nt(seg("q", h), padded(p_k, h)) for h in heads]
    inter = [[_dot(seg("q", h), s0_ref[bi, h].astype(BF16)) for bi in bis] for h in heads]
    kd = [(padded(p_k, h).astype(F32) * kdec_ref[h]).astype(BF16) for h in heads]
    v_b = [[jnp.where(pad_batch == bi, padded(p_v, h), jnp.zeros((pad, DK), BF16)) for bi in bis]
           for h in heads]
    upd = [[_dot_tn(kd[h], v_b[h][bi]) for bi in bis] for h in heads]
    for h in heads:
        for bi in bis:
            s_out_ref[bi, h] = s0_ref[bi, h] * sdec_ref[h] + upd[h][bi]
    att = [(att[h] * dmask_ref[h]).astype(BF16) for h in heads]
    for h in heads:
        own = inter[h][0]
        for bi in bis[1:]:
            own = jnp.where(row_batch == bi, inter[h][bi], own)
        o = _dot(att[h], padded(p_v, h)) + own * qdec_ref[h]
        macc_ref[:, h * DK:(h + 1) * DK] = _retention_gate(o, seg("g", h), seg("mr", h))

    qrows = GROUP * rows
    q_tok = lax.broadcasted_iota(jnp.int32, (qrows, 1), 0) & (dec_seq - 1)
    q_batch = (lax.broadcasted_iota(jnp.int32, (qrows, 1), 0) & (rows - 1)) >> tok_bits
    cache_j = lax.broadcasted_iota(jnp.int32, (qrows, WINDOW), 1)
    new_j = lax.broadcasted_iota(jnp.int32, (qrows, pad), 1)
    mask_cache = cache_j >= q_tok
    mask_new = [((new_j >> tok_bits) == bi) & ((new_j & (dec_seq - 1)) <= q_tok) for bi in bis]
    scale = HD ** -0.5
    kvs = range(n_kv)
    ones = jnp.ones((WINDOW, HD), BF16)
    qs = [jnp.concatenate([seg("qa", kv * GROUP + g, HD) for g in range(GROUP)], axis=0) for kv in kvs]
    s_new = [_dot_nt(qs[kv], padded(p_ka, kv, HD)) for kv in kvs]
    s_cache = [[_dot_nt(qs[kv], kc_ref[bi, :, kv, :].astype(BF16)) for bi in bis] for kv in kvs]
    v_new = [jnp.concatenate([padded(p_va, kv, HD), ones], axis=1) for kv in kvs]
    v_cache = [[jnp.concatenate([vc_ref[bi, :, kv, :].astype(BF16), ones], axis=1) for bi in bis]
               for kv in kvs]
    for kv in kvs:
        sink_col = jnp.concatenate(
            [jnp.full((rows, 1), sink_ref[kv * GROUP + g], F32) for g in range(GROUP)], axis=0)
        o_kv = None
        for bi in bis:
            s_c = jnp.where(mask_cache, s_cache[kv][bi] * scale, NEG_INF)
            s_n = jnp.where(mask_new[bi], s_new[kv] * scale, NEG_INF)
            m = jnp.maximum(jnp.maximum(jnp.max(s_c, axis=-1, keepdims=True),
                                        jnp.max(s_n, axis=-1, keepdims=True)), sink_col)
            pv = (_dot(jnp.exp(s_c - m).astype(BF16), v_cache[kv][bi])
                  + _dot(jnp.exp(s_n - m).astype(BF16), v_new[kv]))
            o = pv[:, :HD] / (pv[:, HD:] + jnp.exp(sink_col - m))
            o_kv = o if o_kv is None else jnp.where(q_batch == bi, o, o_kv)
        for g in range(GROUP):
            head = kv * GROUP + g
            hs = slice(head * HD, (head + 1) * HD)
            out_ref[:, hs] = (macc_ref[:, hs] + seg("ma", head, HD).astype(F32)
                              * o_kv[g * rows:(g + 1) * rows]).astype(BF16)

    keep = WINDOW - dec_seq
    for bi in range(nb):
        kout_ref[bi, 0:keep] = kc_ref[bi, dec_seq:WINDOW]
        vout_ref[bi, 0:keep] = vc_ref[bi, dec_seq:WINDOW]
        for t in range(dec_seq):
            r = bi * dec_seq + t
            for kv in range(n_kv):
                kout_ref[bi, keep + t, kv:kv + 1, :] = kvnew_ref[r:r + 1, kv * HD:(kv + 1) * HD]
                vout_ref[bi, keep + t, kv:kv + 1, :] = kvnew_ref[r:r + 1, kvw + kv * HD:kvw + (kv + 1) * HD]


def _mix_sample(proj, kvnew, sinks, tabs, layer, state_all, k_all, v_all, prev_outs, dec_seq, d):
    off = _segment_offsets(d)
    n_ret = d // DK
    n_kv = d // (HD * GROUP)
    kvw = n_kv * HD
    nb = SAMPLE_BATCHES_PER_STEP
    b = state_all.shape[1]
    rows = nb * dec_seq
    smem = pl.BlockSpec(memory_space=pltpu.SMEM)
    const3 = lambda shape: pl.BlockSpec(shape, lambda i: (0, 0, 0))
    state_spec = lambda: pl.BlockSpec((None, nb, n_ret, DK, DK), lambda i: (layer, i, 0, 0, 0))
    win_spec = lambda: pl.BlockSpec((None, nb, WINDOW, n_kv, HD), lambda i: (layer, i, 0, 0, 0))
    in_specs = [smem, smem,
                pl.BlockSpec((rows, off["width"]), lambda i: (i, 0)),
                pl.BlockSpec((rows, 2 * kvw), lambda i: (i, 0)),
                state_spec(), win_spec(), win_spec(),
                const3((n_ret, rows, WINDOW)), const3((n_ret, rows, DK)), const3((n_ret, WINDOW, DK))]
    args = [sinks, tabs["sdec"], proj, kvnew, state_all, k_all, v_all,
            tabs["dmask"], tabs["qdec"], tabs["kdec"]]
    aliases = {}
    if prev_outs is not None:
        for j, arr in enumerate(prev_outs):
            aliases[len(args)] = 1 + j
            in_specs.append(pl.BlockSpec(memory_space=pl.ANY))
            args.append(arr)
    n_alias = len(aliases)
    merged, state, kwin, vwin = pl.pallas_call(
        functools.partial(_mix_sample_kernel, d=d, dec_seq=dec_seq, n_alias=n_alias),
        grid=(b // nb,),
        in_specs=in_specs,
        out_specs=[pl.BlockSpec((rows, d), lambda i: (i, 0)), state_spec(), win_spec(), win_spec()],
        out_shape=[jax.ShapeDtypeStruct((b * dec_seq, d), BF16),
                   jax.ShapeDtypeStruct(state_all.shape, F32),
                   jax.ShapeDtypeStruct(k_all.shape, F32),
                   jax.ShapeDtypeStruct(v_all.shape, F32)],
        scratch_shapes=[pltpu.VMEM((WINDOW, 2 * d + 2 * kvw), BF16),
                        pltpu.VMEM((rows, d), F32)],
        input_output_aliases=aliases,
        compiler_params=_cparams(1),
        name="mix_sample",
    )(*args)
    return merged, (state, kwin, vwin)


def _proj_residual_kernel(a_ref, w_ref, x_ref, g_ref, *rest, with_norm):
    if with_norm:
        lnw_ref, sc_ref, sh_ref, xo_ref, ho_ref = rest
    else:
        (xo_ref,) = rest
    d = xo_ref.shape[1]
    half = DK // 2
    slabs = [slice(j * DK, (j + 1) * DK) for j in range(d // DK)]
    sq = None
    for cs in slabs:
        x = x_ref[:, cs] + g_ref[:, cs] * _dot(a_ref[...], w_ref[:, cs])
        xo_ref[:, cs] = x
        if with_norm:
            x2 = x * x
            part = x2[:, :half] + x2[:, half:]
            sq = part if sq is None else sq + part
    if with_norm:
        r = lax.rsqrt(jnp.sum(sq, axis=-1, keepdims=True) / d + EPS)
        for cs in slabs:
            y = (xo_ref[:, cs] * r) * lnw_ref[:, cs]
            ho_ref[:, cs] = (y * (1.0 + sc_ref[:, cs]) + sh_ref[:, cs]).astype(BF16)


def _proj_residual(a, w, layer, x, gate, norm, tm, name):
    g, r, d = x.shape
    kdim = w.shape[1]
    tok = lambda: pl.BlockSpec((None, tm, d), lambda i, j: (i, j, 0))
    in_specs = [pl.BlockSpec((None, tm, kdim), lambda i, j: (i, j, 0)),
                pl.BlockSpec((None, kdim, d), lambda i, j: (layer, 0, 0), pipeline_mode=pl.Buffered(1)),
                tok(), _mod_spec(gate, tm)]
    args = [a.reshape(g, r, kdim), w, x, gate.arr]
    out_specs = [tok()]
    out_shape = [jax.ShapeDtypeStruct((g, r, d), F32)]
    if norm is not None:
        lnw, sc, sh = norm
        in_specs += [pl.BlockSpec((1, d), lambda i, j: (0, 0)), _mod_spec(sc, tm), _mod_spec(sh, tm)]
        args += [lnw, sc.arr, sh.arr]
        out_specs.append(tok())
        out_shape.append(jax.ShapeDtypeStruct((g, r, d), BF16))
    out = pl.pallas_call(
        functools.partial(_proj_residual_kernel, with_norm=norm is not None),
        grid=(g, r // tm),
        in_specs=in_specs, out_specs=out_specs, out_shape=out_shape,
        compiler_params=_cparams(2),
        name=name,
    )(*args)
    return (out[0], out[1]) if norm is not None else (out[0], None)


def _up_kernel(h_ref, w_ref, o_ref):
    r = jnp.maximum(_dot(h_ref[...], w_ref[...]), 0.0)
    o_ref[...] = (r * r).astype(BF16)


def _up(h, w, layer, tm):
    m, d = h.shape
    f = w.shape[2]
    tn = min(1024, f)
    return pl.pallas_call(
        _up_kernel,
        grid=(f // tn, m // tm),
        in_specs=[pl.BlockSpec((tm, d), lambda n, i: (i, 0)),
                  pl.BlockSpec((None, d, tn), lambda n, i: (layer, 0, n))],
        out_specs=pl.BlockSpec((tm, tn), lambda n, i: (i, n)),
        out_shape=jax.ShapeDtypeStruct((m, f), BF16),
        compiler_params=_cparams(2),
        name="mlp_up",
    )(h, w)


def _rope_tables(pos):
    half = DK // 2
    inv = 1.0 / (ROPE_BASE ** (jnp.arange(half, dtype=F32) / half))
    ang = pos.astype(F32)[:, None] * inv[None, :]
    return jnp.cos(ang), jnp.sin(ang)


def _decay_tables(n_ret, chunk, reps, pad_cols):
    log_g = jnp.log1p(-jnp.exp2(-5.0 - jnp.arange(n_ret, dtype=F32)))
    rows = reps * chunk
    r = jnp.arange(rows)
    cidx = jnp.arange(pad_cols)
    idx = (r % chunk).astype(F32)
    diff = idx[:, None] - (cidx % chunk).astype(F32)[None, :]
    ok = (diff >= 0) & ((r // chunk)[:, None] == (cidx // chunk)[None, :]) & (cidx < rows)[None, :]
    dintra = jnp.where(ok, jnp.exp(jnp.where(ok, diff, 0.0) * log_g[:, None, None]), 0.0)
    qdec = jnp.exp((idx + 1.0) * log_g[:, None])[..., None]
    kidx = (jnp.arange(max(rows, pad_cols)) % chunk).astype(F32)
    kdec = jnp.exp((chunk - 1.0 - kidx) * log_g[:, None])[..., None]
    sdec = jnp.exp(chunk * log_g)
    return dict(dintra=dintra, dmask=dintra,
                qdec=jnp.broadcast_to(qdec, (n_ret, rows, DK)),
                kdec=jnp.broadcast_to(kdec, (n_ret, kidx.shape[0], DK)), sdec=sdec)


def kernel(x_prompt, x_sample, c_prompt, c_sample, state_ret, cache_k_win, cache_v_win, norm1_w,
           norm2_w, w_ada, b_ada, w_in, q_norm_w, k_norm_w, sinks, w_out, w_up, w_down):
    bp, seq, d = x_prompt.shape
    bs, dec_seq, _ = x_sample.shape
    depth = w_in.shape[0]
    n_ret = d // DK
    kvw = d // 4
    n_kv = kvw // HD
    win = cache_k_win.shape[2]
    assert win == WINDOW and seq % WINDOW == 0 and bs % SAMPLE_BATCHES_PER_STEP == 0
    assert dec_seq & (dec_seq - 1) == 0

    n_c = bp + bs
    c_rows = -(-n_c // 16) * 16
    c_all = jnp.concatenate([c_prompt, c_sample, jnp.zeros((c_rows - n_c, d), F32)], axis=0)
    mods = _ada(c_all, w_ada, b_ada)

    mods_prompt = mods[:, :bp].reshape(depth * bp * 6, 1, d)
    mods_sample = jnp.repeat(mods[:, bp:n_c], dec_seq, axis=1)

    def group_mods(l):
        return ([_Mod(mods_prompt, l, j, bp) for j in range(6)],
                [_Mod(mods_sample, l, j, 1) for j in range(6)])

    wi = w_in.astype(BF16)
    wo = w_out.astype(BF16)
    wu = w_up.astype(BF16)
    wd = w_down.astype(BF16)

    cos_p, sin_p = _rope_tables(jnp.arange(seq, dtype=jnp.int32))
    cos_s, sin_s = _rope_tables(PAST_LEN + jnp.arange(dec_seq, dtype=jnp.int32))
    cos_s = jnp.tile(cos_s, (bs, 1))
    sin_s = jnp.tile(sin_s, (bs, 1))
    tabs_p = _decay_tables(n_ret, WINDOW, 1, WINDOW)
    tabs_s = _decay_tables(n_ret, dec_seq, SAMPLE_BATCHES_PER_STEP, WINDOW)

    tm_p = min(512, seq)
    tm_down = min(256, seq)
    ms_rows = bs * dec_seq
    nblk = seq // WINDOW
    xp = x_prompt
    xs = x_sample.reshape(1, ms_rows, d)

    mods_p, mods_s = group_mods(0)
    hp = _prenorm(xp, norm1_w[0:1], mods_p[1], mods_p[0], tm_p)
    hs = _prenorm(xs, norm1_w[0:1], mods_s[1], mods_s[0], ms_rows)

    outs = {k: [] for k in ("kp", "vp")}
    prompt_state = None
    sample_state = None
    for l in range(depth):
        qn = q_norm_w[l:l + 1]
        kn = k_norm_w[l:l + 1]
        if l + 1 < depth:
            nxt_p, nxt_s = group_mods(l + 1)
            norm_p = (norm1_w[l + 1:l + 2], nxt_p[1], nxt_p[0])
            norm_s = (norm1_w[l + 1:l + 2], nxt_s[1], nxt_s[0])
        else:
            norm_p = norm_s = None

        hp2 = hp.reshape(bp * seq, d)
        proj = _inproj(hp2, wi, l, cos_p, sin_p, qn, kn, min(1024, seq))
        kv_tail = _kvproj(hp2, wi, l, kn, WINDOW, bp, lambda i: i * nblk + nblk - 1)
        merged, prompt_state = _mix_prompt(proj, sinks[l], tabs_p, l, depth, prompt_state, bp, seq, d)
        outs["kp"].append(kv_tail[:, :kvw].reshape(bp, WINDOW, n_kv, HD))
        outs["vp"].append(kv_tail[:, kvw:].reshape(bp, WINDOW, n_kv, HD))
        xp, h2 = _proj_residual(merged, wo, l, xp, mods_p[2], (norm2_w[l:l + 1], mods_p[4], mods_p[3]),
                                tm_p, "outproj")
        u = _up(h2.reshape(bp * seq, d), wu, l, min(1024, seq))
        xp, hp = _proj_residual(u, wd, l, xp, mods_p[5], norm_p, tm_down, "mlp_down")

        hs2 = hs.reshape(ms_rows, d)
        proj = _inproj(hs2, wi, l, cos_s, sin_s, qn, kn, ms_rows)
        kv_new = _kvproj(hs2, wi, l, kn, ms_rows, 1, lambda i: i)
        merged, sample_state = _mix_sample(proj, kv_new, sinks[l], tabs_s, l, state_ret, cache_k_win,
                                           cache_v_win, sample_state, dec_seq, d)
        xs, h2 = _proj_residual(merged, wo, l, xs, mods_s[2], (norm2_w[l:l + 1], mods_s[4], mods_s[3]),
                                ms_rows, "outproj")
        u = _up(h2.reshape(ms_rows, d), wu, l, ms_rows)
        xs, hs = _proj_residual(u, wd, l, xs, mods_s[5], norm_s, min(tm_down // 2, ms_rows), "mlp_down")

        if l + 1 < depth:
            mods_p, mods_s = nxt_p, nxt_s

    return (xp, xs.reshape(bs, dec_seq, d), prompt_state, jnp.stack(outs["kp"]),
            jnp.stack(outs["vp"])) + tuple(sample_state)
```

```python
import functools
import math

import jax
import jax.numpy as jnp
from jax import lax
from jax.experimental import pallas as pl
from jax.experimental.pallas import tpu as pltpu

F32 = jnp.float32
BF16 = jnp.bfloat16

DK = 256
HD = 128
GROUP = 4
WINDOW = 128
ROPE_BASE = 10000.0
EPS = 1e-6
NEG_INF = -1e30
LOG2E = math.log2(math.e)
PAST_LEN = 8192
SAMPLE_BATCHES_PER_STEP = 4
INPROJ_SLAB_ROWS = 256
VMEM_LIMIT = 56 * 1024 * 1024


def _cparams(n_axes):
    return pltpu.CompilerParams(dimension_semantics=("arbitrary",) * n_axes,
                                vmem_limit_bytes=VMEM_LIMIT)


def _sigmoid(x):
    return 1.0 / (1.0 + jnp.exp(-x))


def _modnorm(x, lnw, sc, sh):
    y = x * lax.rsqrt(jnp.mean(x * x, axis=-1, keepdims=True) + EPS)
    return (y * lnw) * (1.0 + sc) + sh


def _dot(a, b):
    return jnp.dot(a, b, preferred_element_type=F32)


def _dot_nt(a, b):
    return lax.dot_general(a, b, (((1,), (1,)), ((), ())), preferred_element_type=F32)


def _dot_tn(a, b):
    return lax.dot_general(a, b, (((0,), (0,)), ((), ())), preferred_element_type=F32)


def _ada_kernel(c_ref, w_ref, b_ref, o_ref):
    c = c_ref[...]
    s = (c * _sigmoid(c)).astype(BF16)
    o_ref[...] = _dot(s, w_ref[...].astype(BF16)) + b_ref[...]


def _ada(c_all, w_ada, b_ada):
    depth, d, n6 = w_ada.shape
    rows = c_all.shape[0]
    tn = min(1024, n6)
    return pl.pallas_call(
        _ada_kernel,
        grid=(depth, n6 // tn),
        in_specs=[pl.BlockSpec((rows, d), lambda l, n: (0, 0)),
                  pl.BlockSpec((None, d, tn), lambda l, n: (l, 0, n)),
                  pl.BlockSpec((None, 1, tn), lambda l, n: (l, 0, n))],
        out_specs=pl.BlockSpec((None, rows, tn), lambda l, n: (l, 0, n)),
        out_shape=jax.ShapeDtypeStruct((depth, rows, n6), F32),
        compiler_params=_cparams(2),
        name="ada",
    )(c_all, w_ada, b_ada.reshape(depth, 1, n6))


def _prenorm_kernel(x_ref, lnw_ref, sc_ref, sh_ref, h_ref):
    h_ref[...] = _modnorm(x_ref[...], lnw_ref[...], sc_ref[...], sh_ref[...]).astype(BF16)


class _Mod:
    def __init__(self, arr, layer, part, groups):
        self.arr, self.layer, self.part, self.groups = arr, layer, part, groups

    def spec(self, tm):
        l, j, arr = self.layer, self.part, self.arr
        if arr.shape[1] == 1:
            base = l * self.groups * 6 + j
            return pl.BlockSpec((None, 1, arr.shape[2]), lambda g, r: (base + g * 6, 0, 0))
        return pl.BlockSpec((None, tm, arr.shape[2] // 6), lambda g, r: (l, r, j))


def _mod_spec(mod, tm):
    return mod.spec(tm)


def _prenorm(x, lnw, sc, sh, tm):
    g, r, d = x.shape
    return pl.pallas_call(
        _prenorm_kernel,
        grid=(g, r // tm),
        in_specs=[pl.BlockSpec((None, tm, d), lambda i, j: (i, j, 0)),
                  pl.BlockSpec((1, d), lambda i, j: (0, 0)),
                  _mod_spec(sc, tm), _mod_spec(sh, tm)],
        out_specs=pl.BlockSpec((None, tm, d), lambda i, j: (i, j, 0)),
        out_shape=jax.ShapeDtypeStruct((g, r, d), BF16),
        compiler_params=_cparams(2),
        name="prenorm",
    )(x, lnw, sc.arr, sh.arr)


def _rms_heads(a, w):
    outs = []
    for j in range(a.shape[1] // HD):
        t = a[:, j * HD:(j + 1) * HD]
        outs.append(t * lax.rsqrt(jnp.mean(t * t, axis=-1, keepdims=True) + EPS) * w)
    return outs


def _side_cast_specs(casts, n_chunks):
    in_specs, out_specs, out_shape, arrays = [], [], [], []
    for w, layer in casts:
        _, rows, cols = w.shape
        chunk = rows // n_chunks
        pick = lambda n, i: jnp.where(n == 0, i, n_chunks - 1)
        in_specs.append(pl.BlockSpec((None, chunk, cols), lambda n, i, layer=layer: (layer, pick(n, i), 0)))
        out_specs.append(pl.BlockSpec((None, chunk, cols), lambda n, i: (0, pick(n, i), 0)))
        out_shape.append(jax.ShapeDtypeStruct((1, rows, cols), BF16))
        arrays.append(w)
    return in_specs, out_specs, out_shape, arrays


def _side_casts(rest):
    n_casts = (len(rest) - 1) // 2
    if n_casts:
        @pl.when(pl.program_id(0) == 0)
        def _cast():
            for src, dst in zip(rest[:n_casts], rest[n_casts + 1:]):
                dst[...] = src[...].astype(BF16)

    return rest[n_casts]


def _inproj_kernel(h_ref, w_ref, cos_ref, sin_ref, qn_ref, kn_ref, *rest, tn):
    o_ref = _side_casts(rest)
    n = pl.program_id(0)
    half = DK // 2
    tm = h_ref.shape[0]
    tr = min(tm, INPROJ_SLAB_ROWS)
    slabs = [(slice(r, r + tr), j) for j in range(tn // DK) for r in range(0, tm, tr)]
    full_slabs = [(slice(0, tm), j) for j in range(tn // DK)]

    def slab(rows, j):
        return _dot(h_ref[rows, :], w_ref[:, j * DK:(j + 1) * DK])

    @pl.when(n < 4)
    def _rotary():
        scale = jnp.where(n < 2, 1.0, DK ** -0.5).astype(F32)
        for rows, j in full_slabs:
            a = slab(rows, j)
            cos = cos_ref[rows, :]
            sin = sin_ref[rows, :]
            t1 = a[:, :half]
            t2 = a[:, half:]
            o_ref[rows, j * DK:j * DK + half] = ((t1 * cos - t2 * sin) * scale).astype(BF16)
            o_ref[rows, j * DK + half:(j + 1) * DK] = ((t1 * sin + t2 * cos) * scale).astype(BF16)

    def _rms_slabs(w, n_chunks):
        for rows, j in slabs:
            if j < n_chunks:
                for i, y in enumerate(_rms_heads(slab(rows, j), w)):
                    o_ref[rows, j * DK + i * HD:j * DK + (i + 1) * HD] = y.astype(BF16)

    @pl.when((n == 8) | (n == 9))
    def _qnorm():
        _rms_slabs(qn_ref[...], tn // DK)

    @pl.when(n == 10)
    def _kv():
        if tn // 2 >= DK:
            _rms_slabs(kn_ref[...], tn // (2 * DK))
        else:
            a = _dot(h_ref[...], w_ref[:, :tn // 2])
            for i, y in enumerate(_rms_heads(a, kn_ref[...])):
                o_ref[:, i * HD:(i + 1) * HD] = y.astype(BF16)
        o_ref[:, tn // 2:] = _dot(h_ref[...], w_ref[:, tn // 2:]).astype(BF16)

    @pl.when((n == 4) | (n == 5))
    def _plain():
        o_ref[...] = _dot(h_ref[...], w_ref[...]).astype(BF16)

    @pl.when((n == 6) | (n == 7))
    def _silu_gate():
        for rows, j in slabs:
            a = slab(rows, j)
            o_ref[rows, j * DK:(j + 1) * DK] = (a * _sigmoid(a)).astype(BF16)

    @pl.when(n > 10)
    def _sigmoid_gate():
        for rows, j in slabs:
            o_ref[rows, j * DK:(j + 1) * DK] = _sigmoid(slab(rows, j)).astype(BF16)


def _inproj(h, w, layer, cos, sin, qn, kn, tm, casts=()):
    m, d = h.shape
    width = w.shape[2]
    tn = d // 2
    nblk = cos.shape[0] // tm
    c_in, c_out, c_shape, c_arrays = _side_cast_specs(casts, m // tm)
    out = pl.pallas_call(
        functools.partial(_inproj_kernel, tn=tn),
        grid=(width // tn, m // tm),
        in_specs=[pl.BlockSpec((tm, d), lambda n, i: (i, 0)),
                  pl.BlockSpec((None, d, tn), lambda n, i: (layer, 0, n)),
                  pl.BlockSpec((tm, HD), lambda n, i: (i % nblk, 0)),
                  pl.BlockSpec((tm, HD), lambda n, i: (i % nblk, 0)),
                  pl.BlockSpec((1, HD), lambda n, i: (0, 0)),
                  pl.BlockSpec((1, HD), lambda n, i: (0, 0))] + c_in,
        out_specs=[pl.BlockSpec((tm, tn), lambda n, i: (i, n))] + c_out,
        out_shape=[jax.ShapeDtypeStruct((m, width), BF16)] + c_shape,
        compiler_params=_cparams(2),
        name="inproj",
    )(h, w, cos, sin, qn, kn, *c_arrays)
    return out[0], list(out[1:])


def _kvproj_kernel(h_ref, w_ref, kn_ref, o_ref):
    kvw = o_ref.shape[1] // 2
    a = _dot(h_ref[...], w_ref[...])
    for i, y in enumerate(_rms_heads(a[:, :kvw], kn_ref[...])):
        o_ref[:, i * HD:(i + 1) * HD] = y
    o_ref[:, kvw:] = a[:, kvw:]


def _kvproj(h, w, layer, kn, tm, n_blocks, row_block):
    d = h.shape[1]
    kvw = d // 4
    col_block = (5 * d) // (2 * kvw)
    return pl.pallas_call(
        _kvproj_kernel,
        grid=(n_blocks,),
        in_specs=[pl.BlockSpec((tm, d), lambda i: (row_block(i), 0)),
                  pl.BlockSpec((None, d, 2 * kvw), lambda i: (layer, 0, col_block)),
                  pl.BlockSpec((1, HD), lambda i: (0, 0))],
        out_specs=pl.BlockSpec((tm, 2 * kvw), lambda i: (i, 0)),
        out_shape=jax.ShapeDtypeStruct((n_blocks * tm, 2 * kvw), F32),
        compiler_params=_cparams(1),
        name="kvproj",
    )(h, w, kn)


def _segment_offsets(d):
    kvw = d // 4
    return dict(q=0, k=d, v=2 * d, g=3 * d, qa=4 * d, ka=5 * d, va=5 * d + kvw,
                mr=5 * d + 2 * kvw, ma=6 * d + 2 * kvw, width=7 * d + 2 * kvw)


def _retention_gate(o, g_act, mr_act):
    on = o * lax.rsqrt(jnp.mean(o * o, axis=-1, keepdims=True) + EPS)
    return mr_act.astype(F32) * (on * g_act.astype(F32))


def _mix_prompt_kernel(sink_ref, sdec_ref, blk_ref, prev_ref, dintra_ref, qdec_ref, kdec_ref,
                       *rest, d):
    out_ref, s_ref, macc_ref = rest[-3:]
    c = pl.program_id(1)
    off = _segment_offsets(d)
    n_ret = d // DK
    n_kv = d // (HD * GROUP)
    kvw = n_kv * HD
    blk = WINDOW

    @pl.when(c == 0)
    def _init():
        s_ref[...] = jnp.zeros_like(s_ref)

    def seg(name, h, width=DK):
        return blk_ref[:, off[name] + h * width:off[name] + (h + 1) * width]

    qi = lax.broadcasted_iota(jnp.int32, (blk, 2 * blk), 0)
    kj = lax.broadcasted_iota(jnp.int32, (blk, 2 * blk), 1)
    diff = kj - qi
    jmin = jnp.where(c == 0, blk, 0)
    mask = (diff >= 0) & (diff <= WINDOW) & (kj >= jmin)
    scale = HD ** -0.5
    kvs = range(n_kv)
    ones = jnp.ones((2 * blk, HD), BF16)
    scores, values = [], []
    for kv in kvs:
        qs = jnp.concatenate([seg("qa", kv * GROUP + g, HD) for g in range(GROUP)], axis=0)
        k2 = jnp.concatenate([prev_ref[:, kv * HD:(kv + 1) * HD], seg("ka", kv, HD)], axis=0)
        v2 = jnp.concatenate([prev_ref[:, kvw + kv * HD:kvw + (kv + 1) * HD], seg("va", kv, HD)], axis=0)
        scores.append(_dot_nt(qs, k2))
        values.append(jnp.concatenate([v2, ones], axis=1))
    heads = range(n_ret)
    att = [_dot_nt(seg("q", h), seg("k", h)) for h in heads]
    inter = [_dot(seg("q", h), s_ref[h].astype(BF16)) for h in heads]
    upd = [_dot_tn((seg("k", h).astype(F32) * kdec_ref[h]).astype(BF16), seg("v", h)) for h in heads]
    for h in heads:
        s_ref[h] = s_ref[h] * sdec_ref[h] + upd[h]
    att = [(att[h] * dintra_ref[h]).astype(BF16) for h in heads]
    o_ret = [_dot(att[h], seg("v", h)) + inter[h] * qdec_ref[h] for h in heads]
    for h in heads:
        macc_ref[:, h * DK:(h + 1) * DK] = _retention_gate(o_ret[h], seg("g", h), seg("mr", h))

    probs, sink_terms = [], []
    for kv in kvs:
        p_rows = []
        for g in range(GROUP):
            sink = sink_ref[kv * GROUP + g] * (1.0 / scale)
            s = jnp.where(mask, scores[kv][g * blk:(g + 1) * blk], NEG_INF)
            m = jnp.maximum(jnp.max(s, axis=-1, keepdims=True), sink)
            p_rows.append(jnp.exp2((s - m) * (scale * LOG2E)).astype(BF16))
            sink_terms.append(jnp.exp2((sink - m) * (scale * LOG2E)))
        probs.append(jnp.concatenate(p_rows, axis=0))
    pv = [_dot(probs[kv], values[kv]) for kv in kvs]
    for kv in kvs:
        for g in range(GROUP):
            head = kv * GROUP + g
            hs = slice(head * HD, (head + 1) * HD)
            o = pv[kv][g * blk:(g + 1) * blk, :HD]
            denom = pv[kv][g * blk:(g + 1) * blk, HD:] + sink_terms[head]
            out_ref[:, hs] = (macc_ref[:, hs] + seg("ma", head, HD).astype(F32) * (o / denom)).astype(BF16)


def _mix_prompt(proj, sinks, tabs, layer, depth, prev_state, b, seq, d):
    off = _segment_offsets(d)
    n_ret = d // DK
    kvw = d // 4
    nblk = seq // WINDOW
    proj3 = proj.reshape(b, seq, off["width"])
    kv_col_block = off["ka"] // (2 * kvw)
    smem = pl.BlockSpec(memory_space=pltpu.SMEM)
    const3 = lambda shape: pl.BlockSpec(shape, lambda i, c: (0, 0, 0))
    in_specs = [smem, smem,
                pl.BlockSpec((None, WINDOW, off["width"]), lambda i, c: (i, c, 0)),
                pl.BlockSpec((None, WINDOW, 2 * kvw),
                             lambda i, c: (i, jnp.maximum(c - 1, 0), kv_col_block)),
                const3((n_ret, WINDOW, WINDOW)), const3((n_ret, WINDOW, DK)),
                const3((n_ret, WINDOW, DK))]
    args = [sinks, tabs["sdec"], proj3, proj3, tabs["dintra"], tabs["qdec"], tabs["kdec"]]
    aliases = {}
    if prev_state is not None:
        aliases[len(args)] = 1
        in_specs.append(pl.BlockSpec(memory_space=pl.ANY))
        args.append(prev_state)
    merged, state = pl.pallas_call(
        functools.partial(_mix_prompt_kernel, d=d),
        grid=(b, nblk),
        in_specs=in_specs,
        out_specs=[pl.BlockSpec((None, WINDOW, d), lambda i, c: (i, c, 0)),
                   pl.BlockSpec((None, None, n_ret, DK, DK), lambda i, c: (layer, i, 0, 0, 0))],
        out_shape=[jax.ShapeDtypeStruct((b, seq, d), BF16),
                   jax.ShapeDtypeStruct((depth, b, n_ret, DK, DK), F32)],
        scratch_shapes=[pltpu.VMEM((WINDOW, d), F32)],
        input_output_aliases=aliases,
        compiler_params=_cparams(2),
        name="mix_prompt",
    )(*args)
    return merged.reshape(b * seq, d), state


def _mix_sample_kernel(sink_ref, sdec_ref, blk_ref, kvnew_ref, s0_ref, kc_ref, vc_ref, dmask_ref,
                       qdec_ref, kdec_ref, *rest, d, dec_seq, n_alias):
    out_ref, s_out_ref, kout_ref, vout_ref, pad_ref, macc_ref = rest[n_alias:]
    off = _segment_offsets(d)
    n_ret = d // DK
    n_kv = d // (HD * GROUP)
    kvw = n_kv * HD
    nb = SAMPLE_BATCHES_PER_STEP
    rows = nb * dec_seq
    pad = pad_ref.shape[0]
    p_k, p_v, p_ka, p_va = 0, d, 2 * d, 2 * d + kvw

    @pl.when(pl.program_id(0) == 0)
    def _init():
        pad_ref[...] = jnp.zeros_like(pad_ref)

    pad_ref[0:rows, p_k:p_k + 2 * d] = blk_ref[:, off["k"]:off["k"] + 2 * d]
    pad_ref[0:rows, p_ka:p_ka + 2 * kvw] = blk_ref[:, off["ka"]:off["ka"] + 2 * kvw]

    tok_bits = dec_seq.bit_length() - 1
    row_batch = lax.broadcasted_iota(jnp.int32, (rows, 1), 0) >> tok_bits
    pad_batch = lax.broadcasted_iota(jnp.int32, (pad, 1), 0) >> tok_bits

    def seg(name, h, width=DK):
        return blk_ref[:, off[name] + h * width:off[name] + (h + 1) * width]

    def padded(col0, h, width=DK):
        return pad_ref[:, col0 + h * width:col0 + (h + 1) * width]

    heads = range(n_ret)
    bis = range(nb)
    att = [_dot_nt(seg("q", h), padded(p_k, h)) for h in heads]
    inter = [[_dot(seg("q", h), s0_ref[bi, h].astype(BF16)) for bi in bis] for h in heads]
    kd = [(padded(p_k, h).astype(F32) * kdec_ref[h]).astype(BF16) for h in heads]
    v_b = [[jnp.where(pad_batch == bi, padded(p_v, h), jnp.zeros((pad, DK), BF16)) for bi in bis]
           for h in heads]
    upd = [[_dot_tn(kd[h], v_b[h][bi]) for bi in bis] for h in heads]
    for h in heads:
        for bi in bis:
            s_out_ref[bi, h] = s0_ref[bi, h] * sdec_ref[h] + upd[h][bi]
    att = [(att[h] * dmask_ref[h]).astype(BF16) for h in heads]
    for h in heads:
        own = inter[h][0]
        for bi in bis[1:]:
            own = jnp.where(row_batch == bi, inter[h][bi], own)
        o = _dot(att[h], padded(p_v, h)) + own * qdec_ref[h]
        macc_ref[:, h * DK:(h + 1) * DK] = _retention_gate(o, seg("g", h), seg("mr", h))

    qrows = GROUP * rows
    q_tok = lax.broadcasted_iota(jnp.int32, (qrows, 1), 0) & (dec_seq - 1)
    q_batch = (lax.broadcasted_iota(jnp.int32, (qrows, 1), 0) & (rows - 1)) >> tok_bits
    cache_j = lax.broadcasted_iota(jnp.int32, (qrows, WINDOW), 1)
    new_j = lax.broadcasted_iota(jnp.int32, (qrows, pad), 1)
    mask_cache = cache_j >= q_tok
    mask_new = [((new_j >> tok_bits) == bi) & ((new_j & (dec_seq - 1)) <= q_tok) for bi in bis]
    scale = HD ** -0.5
    kvs = range(n_kv)
    ones = jnp.ones((WINDOW, HD), BF16)
    qs = [jnp.concatenate([seg("qa", kv * GROUP + g, HD) for g in range(GROUP)], axis=0) for kv in kvs]
    s_new = [_dot_nt(qs[kv], padded(p_ka, kv, HD)) for kv in kvs]
    s_cache = [[_dot_nt(qs[kv], kc_ref[bi, :, kv, :].astype(BF16)) for bi in bis] for kv in kvs]
    v_new = [jnp.concatenate([padded(p_va, kv, HD), ones], axis=1) for kv in kvs]
    v_cache = [[jnp.concatenate([vc_ref[bi, :, kv, :].astype(BF16), ones], axis=1) for bi in bis]
               for kv in kvs]
    for kv in kvs:
        sink_col = jnp.concatenate(
            [jnp.full((rows, 1), sink_ref[kv * GROUP + g], F32) for g in range(GROUP)], axis=0)
        o_kv = None
        for bi in bis:
            s_c = jnp.where(mask_cache, s_cache[kv][bi] * scale, NEG_INF)
            s_n = jnp.where(mask_new[bi], s_new[kv] * scale, NEG_INF)
            m = jnp.maximum(jnp.maximum(jnp.max(s_c, axis=-1, keepdims=True),
                                        jnp.max(s_n, axis=-1, keepdims=True)), sink_col)
            pv = (_dot(jnp.exp(s_c - m).astype(BF16), v_cache[kv][bi])
                  + _dot(jnp.exp(s_n - m).astype(BF16), v_new[kv]))
            o = pv[:, :HD] / (pv[:, HD:] + jnp.exp(sink_col - m))
            o_kv = o if o_kv is None else jnp.where(q_batch == bi, o, o_kv)
        for g in range(GROUP):
            head = kv * GROUP + g
            hs = slice(head * HD, (head + 1) * HD)
            out_ref[:, hs] = (macc_ref[:, hs] + seg("ma", head, HD).astype(F32)
                              * o_kv[g * rows:(g + 1) * rows]).astype(BF16)

    keep = WINDOW - dec_seq
    for bi in range(nb):
        kout_ref[bi, 0:keep] = kc_ref[bi, dec_seq:WINDOW]
        vout_ref[bi, 0:keep] = vc_ref[bi, dec_seq:WINDOW]
        for t in range(dec_seq):
            r = bi * dec_seq + t
            for kv in range(n_kv):
                kout_ref[bi, keep + t, kv:kv + 1, :] = kvnew_ref[r:r + 1, kv * HD:(kv + 1) * HD]
                vout_ref[bi, keep + t, kv:kv + 1, :] = kvnew_ref[r:r + 1, kvw + kv * HD:kvw + (kv + 1) * HD]


def _mix_sample(proj, kvnew, sinks, tabs, layer, state_all, k_all, v_all, prev_outs, dec_seq, d):
    off = _segment_offsets(d)
    n_ret = d // DK
    n_kv = d // (HD * GROUP)
    kvw = n_kv * HD
    nb = SAMPLE_BATCHES_PER_STEP
    b = state_all.shape[1]
    rows = nb * dec_seq
    smem = pl.BlockSpec(memory_space=pltpu.SMEM)
    const3 = lambda shape: pl.BlockSpec(shape, lambda i: (0, 0, 0))
    state_spec = lambda: pl.BlockSpec((None, nb, n_ret, DK, DK), lambda i: (layer, i, 0, 0, 0))
    win_spec = lambda: pl.BlockSpec((None, nb, WINDOW, n_kv, HD), lambda i: (layer, i, 0, 0, 0))
    in_specs = [smem, smem,
                pl.BlockSpec((rows, off["width"]), lambda i: (i, 0)),
                pl.BlockSpec((rows, 2 * kvw), lambda i: (i, 0)),
                state_spec(), win_spec(), win_spec(),
                const3((n_ret, rows, WINDOW)), const3((n_ret, rows, DK)), const3((n_ret, WINDOW, DK))]
    args = [sinks, tabs["sdec"], proj, kvnew, state_all, k_all, v_all,
            tabs["dmask"], tabs["qdec"], tabs["kdec"]]
    aliases = {}
    if prev_outs is not None:
        for j, arr in enumerate(prev_outs):
            aliases[len(args)] = 1 + j
            in_specs.append(pl.BlockSpec(memory_space=pl.ANY))
            args.append(arr)
    n_alias = len(aliases)
    merged, state, kwin, vwin = pl.pallas_call(
        functools.partial(_mix_sample_kernel, d=d, dec_seq=dec_seq, n_alias=n_alias),
        grid=(b // nb,),
        in_specs=in_specs,
        out_specs=[pl.BlockSpec((rows, d), lambda i: (i, 0)), state_spec(), win_spec(), win_spec()],
        out_shape=[jax.ShapeDtypeStruct((b * dec_seq, d), BF16),
                   jax.ShapeDtypeStruct(state_all.shape, F32),
                   jax.ShapeDtypeStruct(k_all.shape, F32),
                   jax.ShapeDtypeStruct(v_all.shape, F32)],
        scratch_shapes=[pltpu.VMEM((WINDOW, 2 * d + 2 * kvw), BF16),
                        pltpu.VMEM((rows, d), F32)],
        input_output_aliases=aliases,
        compiler_params=_cparams(1),
        name="mix_sample",
    )(*args)
    return merged, (state, kwin, vwin)


def _proj_residual_kernel(a_ref, w_ref, x_ref, g_ref, *rest, with_norm):
    if with_norm:
        lnw_ref, sc_ref, sh_ref, xo_ref, ho_ref = rest
    else:
        (xo_ref,) = rest
    d = xo_ref.shape[1]
    half = DK // 2
    slabs = [slice(j * DK, (j + 1) * DK) for j in range(d // DK)]
    sq = None
    for cs in slabs:
        x = x_ref[:, cs] + g_ref[:, cs] * _dot(a_ref[...], w_ref[:, cs])
        xo_ref[:, cs] = x
        if with_norm:
            x2 = x * x
            part = x2[:, :half] + x2[:, half:]
            sq = part if sq is None else sq + part
    if with_norm:
        r = lax.rsqrt(jnp.sum(sq, axis=-1, keepdims=True) / d + EPS)
        for cs in slabs:
            y = (xo_ref[:, cs] * r) * lnw_ref[:, cs]
            ho_ref[:, cs] = (y * (1.0 + sc_ref[:, cs]) + sh_ref[:, cs]).astype(BF16)


def _proj_residual(a, w, layer, x, gate, norm, tm, name):
    g, r, d = x.shape
    kdim = w.shape[1]
    tok = lambda: pl.BlockSpec((None, tm, d), lambda i, j: (i, j, 0))
    in_specs = [pl.BlockSpec((None, tm, kdim), lambda i, j: (i, j, 0)),
                pl.BlockSpec((None, kdim, d), lambda i, j: (layer, 0, 0), pipeline_mode=pl.Buffered(1)),
                tok(), _mod_spec(gate, tm)]
    args = [a.reshape(g, r, kdim), w, x, gate.arr]
    out_specs = [tok()]
    out_shape = [jax.ShapeDtypeStruct((g, r, d), F32)]
    if norm is not None:
        lnw, sc, sh = norm
        in_specs += [pl.BlockSpec((1, d), lambda i, j: (0, 0)), _mod_spec(sc, tm), _mod_spec(sh, tm)]
        args += [lnw, sc.arr, sh.arr]
        out_specs.append(tok())
        out_shape.append(jax.ShapeDtypeStruct((g, r, d), BF16))
    out = pl.pallas_call(
        functools.partial(_proj_residual_kernel, with_norm=norm is not None),
        grid=(g, r // tm),
        in_specs=in_specs, out_specs=out_specs, out_shape=out_shape,
        compiler_params=_cparams(2),
        name=name,
    )(*args)
    return (out[0], out[1]) if norm is not None else (out[0], None)


def _up_kernel(h_ref, w_ref, *rest):
    o_ref = _side_casts(rest)
    r = jnp.maximum(_dot(h_ref[...], w_ref[...]), 0.0)
    o_ref[...] = (r * r).astype(BF16)


def _up(h, w, layer, tm, casts=()):
    m, d = h.shape
    f = w.shape[2]
    tn = min(1024, f)
    c_in, c_out, c_shape, c_arrays = _side_cast_specs(casts, m // tm)
    out = pl.pallas_call(
        _up_kernel,
        grid=(f // tn, m // tm),
        in_specs=[pl.BlockSpec((tm, d), lambda n, i: (i, 0)),
                  pl.BlockSpec((None, d, tn), lambda n, i: (layer, 0, n))] + c_in,
        out_specs=[pl.BlockSpec((tm, tn), lambda n, i: (i, n))] + c_out,
        out_shape=[jax.ShapeDtypeStruct((m, f), BF16)] + c_shape,
        compiler_params=_cparams(2),
        name="mlp_up",
    )(h, w, *c_arrays)
    return out[0], list(out[1:])


def _rope_tables(pos):
    half = DK // 2
    inv = 1.0 / (ROPE_BASE ** (jnp.arange(half, dtype=F32) / half))
    ang = pos.astype(F32)[:, None] * inv[None, :]
    return jnp.cos(ang), jnp.sin(ang)


def _decay_tables(n_ret, chunk, reps, pad_cols):
    log_g = jnp.log1p(-jnp.exp2(-5.0 - jnp.arange(n_ret, dtype=F32)))
    rows = reps * chunk
    r = jnp.arange(rows)
    cidx = jnp.arange(pad_cols)
    idx = (r % chunk).astype(F32)
    diff = idx[:, None] - (cidx % chunk).astype(F32)[None, :]
    ok = (diff >= 0) & ((r // chunk)[:, None] == (cidx // chunk)[None, :]) & (cidx < rows)[None, :]
    dintra = jnp.where(ok, jnp.exp(jnp.where(ok, diff, 0.0) * log_g[:, None, None]), 0.0)
    qdec = jnp.exp((idx + 1.0) * log_g[:, None])[..., None]
    kidx = (jnp.arange(max(rows, pad_cols)) % chunk).astype(F32)
    kdec = jnp.exp((chunk - 1.0 - kidx) * log_g[:, None])[..., None]
    sdec = jnp.exp(chunk * log_g)
    return dict(dintra=dintra, dmask=dintra,
                qdec=jnp.broadcast_to(qdec, (n_ret, rows, DK)),
                kdec=jnp.broadcast_to(kdec, (n_ret, kidx.shape[0], DK)), sdec=sdec)


def kernel(x_prompt, x_sample, c_prompt, c_sample, state_ret, cache_k_win, cache_v_win, norm1_w,
           norm2_w, w_ada, b_ada, w_in, q_norm_w, k_norm_w, sinks, w_out, w_up, w_down):
    bp, seq, d = x_prompt.shape
    bs, dec_seq, _ = x_sample.shape
    depth = w_in.shape[0]
    n_ret = d // DK
    kvw = d // 4
    n_kv = kvw // HD
    win = cache_k_win.shape[2]
    assert win == WINDOW and seq % WINDOW == 0 and bs % SAMPLE_BATCHES_PER_STEP == 0
    assert dec_seq & (dec_seq - 1) == 0

    ms_rows = bs * dec_seq
    n_c = ms_rows + bp
    c_rows = -(-n_c // 16) * 16
    c_all = jnp.concatenate([jnp.repeat(c_sample, dec_seq, axis=0), c_prompt,
                             jnp.zeros((c_rows - n_c, d), F32)], axis=0)
    mods = _ada(c_all, w_ada, b_ada)

    mods_prompt = mods[:, ms_rows:n_c].reshape(depth * bp * 6, 1, d)

    def group_mods(l):
        return ([_Mod(mods_prompt, l, j, bp) for j in range(6)],
                [_Mod(mods, l, j, 1) for j in range(6)])

    wi = w_in.astype(BF16)

    cos_p, sin_p = _rope_tables(jnp.arange(seq, dtype=jnp.int32))
    cos_s, sin_s = _rope_tables(PAST_LEN + jnp.arange(dec_seq, dtype=jnp.int32))
    cos_s = jnp.tile(cos_s, (bs, 1))
    sin_s = jnp.tile(sin_s, (bs, 1))
    tabs_p = _decay_tables(n_ret, WINDOW, 1, WINDOW)
    tabs_s = _decay_tables(n_ret, dec_seq, SAMPLE_BATCHES_PER_STEP, WINDOW)

    tm_p = min(512, seq)
    tm_down = min(256, seq)
    ms_rows = bs * dec_seq
    nblk = seq // WINDOW
    xp = x_prompt
    xs = x_sample.reshape(1, ms_rows, d)

    mods_p, mods_s = group_mods(0)
    hp = _prenorm(xp, norm1_w[0:1], mods_p[1], mods_p[0], tm_p)
    hs = _prenorm(xs, norm1_w[0:1], mods_s[1], mods_s[0], ms_rows)

    outs = {k: [] for k in ("kp", "vp")}
    prompt_state = None
    sample_state = None
    for l in range(depth):
        qn = q_norm_w[l:l + 1]
        kn = k_norm_w[l:l + 1]
        if l + 1 < depth:
            nxt_p, nxt_s = group_mods(l + 1)
            norm_p = (norm1_w[l + 1:l + 2], nxt_p[1], nxt_p[0])
            norm_s = (norm1_w[l + 1:l + 2], nxt_s[1], nxt_s[0])
        else:
            norm_p = norm_s = None

        hp2 = hp.reshape(bp * seq, d)
        proj, (wu, wo) = _inproj(hp2, wi, l, cos_p, sin_p, qn, kn, min(1024, seq),
                                 casts=[(w_up, l), (w_out, l)])
        kv_tail = _kvproj(hp2, wi, l, kn, WINDOW, bp, lambda i: i * nblk + nblk - 1)
        merged, prompt_state = _mix_prompt(proj, sinks[l], tabs_p, l, depth, prompt_state, bp, seq, d)
        outs["kp"].append(kv_tail[:, :kvw].reshape(bp, WINDOW, n_kv, HD))
        outs["vp"].append(kv_tail[:, kvw:].reshape(bp, WINDOW, n_kv, HD))
        xp, h2 = _proj_residual(merged, wo, 0, xp, mods_p[2], (norm2_w[l:l + 1], mods_p[4], mods_p[3]),
                                tm_p, "outproj")
        u, (wd,) = _up(h2.reshape(bp * seq, d), wu, 0, min(1024, seq), casts=[(w_down, l)])
        xp, hp = _proj_residual(u, wd, 0, xp, mods_p[5], norm_p, tm_down, "mlp_down")

        hs2 = hs.reshape(ms_rows, d)
        proj, _ = _inproj(hs2, wi, l, cos_s, sin_s, qn, kn, ms_rows)
        kv_new = _kvproj(hs2, wi, l, kn, ms_rows, 1, lambda i: i)
        merged, sample_state = _mix_sample(proj, kv_new, sinks[l], tabs_s, l, state_ret, cache_k_win,
                                           cache_v_win, sample_state, dec_seq, d)
        xs, h2 = _proj_residual(merged, wo, 0, xs, mods_s[2], (norm2_w[l:l + 1], mods_s[4], mods_s[3]),
                                ms_rows, "outproj")
        u, _ = _up(h2.reshape(ms_rows, d), wu, 0, ms_rows)
        xs, hs = _proj_residual(u, wd, 0, xs, mods_s[5], norm_s, min(tm_down // 2, ms_rows), "mlp_down")

        if l + 1 < depth:
            mods_p, mods_s = nxt_p, nxt_s

    return (xp, xs.reshape(bs, dec_seq, d), prompt_state, jnp.stack(outs["kp"]),
            jnp.stack(outs["vp"])) + tuple(sample_state)
```

```python
import functools
import math

import jax
import jax.numpy as jnp
from jax import lax
from jax.experimental import pallas as pl
from jax.experimental.pallas import tpu as pltpu

F32 = jnp.float32
BF16 = jnp.bfloat16

DK = 256
HD = 128
GROUP = 4
WINDOW = 128
ROPE_BASE = 10000.0
EPS = 1e-6
NEG_INF = -1e30
LOG2E = math.log2(math.e)
PAST_LEN = 8192
SAMPLE_BATCHES_PER_STEP = 4
INPROJ_SLAB_ROWS = 256
VMEM_LIMIT = 56 * 1024 * 1024


def _cparams(n_axes):
    return pltpu.CompilerParams(dimension_semantics=("arbitrary",) * n_axes,
                                vmem_limit_bytes=VMEM_LIMIT)


def _sigmoid(x):
    return 1.0 / (1.0 + jnp.exp(-x))


def _modnorm(x, lnw, sc, sh):
    y = x * lax.rsqrt(jnp.mean(x * x, axis=-1, keepdims=True) + EPS)
    return (y * lnw) * (1.0 + sc) + sh


def _dot(a, b):
    return jnp.dot(a, b, preferred_element_type=F32)


def _dot_nt(a, b):
    return lax.dot_general(a, b, (((1,), (1,)), ((), ())), preferred_element_type=F32)


def _dot_tn(a, b):
    return lax.dot_general(a, b, (((0,), (0,)), ((), ())), preferred_element_type=F32)


def _ada_kernel(c_ref, w_ref, b_ref, o_ref):
    c = c_ref[...]
    s = (c * _sigmoid(c)).astype(BF16)
    o_ref[...] = _dot(s, w_ref[...].astype(BF16)) + b_ref[...]


def _ada(c_all, w_ada, b_ada):
    depth, d, n6 = w_ada.shape
    rows = c_all.shape[0]
    tn = min(1024, n6)
    return pl.pallas_call(
        _ada_kernel,
        grid=(depth, n6 // tn),
        in_specs=[pl.BlockSpec((rows, d), lambda l, n: (0, 0)),
                  pl.BlockSpec((None, d, tn), lambda l, n: (l, 0, n)),
                  pl.BlockSpec((None, 1, tn), lambda l, n: (l, 0, n))],
        out_specs=pl.BlockSpec((None, rows, tn), lambda l, n: (l, 0, n)),
        out_shape=jax.ShapeDtypeStruct((depth, rows, n6), F32),
        compiler_params=_cparams(2),
        name="ada",
    )(c_all, w_ada, b_ada.reshape(depth, 1, n6))


def _prenorm_kernel(x_ref, lnw_ref, sc_ref, sh_ref, h_ref):
    h_ref[...] = _modnorm(x_ref[...], lnw_ref[...], sc_ref[...], sh_ref[...]).astype(BF16)


class _Mod:
    def __init__(self, arr, layer, part, groups):
        self.arr, self.layer, self.part, self.groups = arr, layer, part, groups

    def spec(self, tm):
        l, j, arr = self.layer, self.part, self.arr
        if arr.shape[1] == 1:
            base = l * self.groups * 6 + j
            return pl.BlockSpec((None, 1, arr.shape[2]), lambda g, r: (base + g * 6, 0, 0))
        return pl.BlockSpec((None, tm, arr.shape[2] // 6), lambda g, r: (l, r, j))


def _mod_spec(mod, tm):
    return mod.spec(tm)


def _prenorm(x, lnw, sc, sh, tm):
    g, r, d = x.shape
    return pl.pallas_call(
        _prenorm_kernel,
        grid=(g, r // tm),
        in_specs=[pl.BlockSpec((None, tm, d), lambda i, j: (i, j, 0)),
                  pl.BlockSpec((1, d), lambda i, j: (0, 0)),
                  _mod_spec(sc, tm), _mod_spec(sh, tm)],
        out_specs=pl.BlockSpec((None, tm, d), lambda i, j: (i, j, 0)),
        out_shape=jax.ShapeDtypeStruct((g, r, d), BF16),
        compiler_params=_cparams(2),
        name="prenorm",
    )(x, lnw, sc.arr, sh.arr)


def _rms_heads(a, w):
    outs = []
    for j in range(a.shape[1] // HD):
        t = a[:, j * HD:(j + 1) * HD]
        outs.append(t * lax.rsqrt(jnp.mean(t * t, axis=-1, keepdims=True) + EPS) * w)
    return outs


BF16_SUBLANES = 16


def _side_cast_specs(casts, n_n, n_i):
    in_specs, out_specs, out_shape, arrays, passes = [], [], [], [], []
    for w, layer in casts:
        _, rows, cols = w.shape
        p = max(q for q in range(1, n_n + 1)
                if rows % (q * n_i) == 0 and (rows // (q * n_i)) % BF16_SUBLANES == 0)
        chunk = rows // (p * n_i)
        pick = lambda n, i, p=p: jnp.where(n < p, n * n_i + i, p * n_i - 1)
        in_specs.append(pl.BlockSpec((None, chunk, cols), lambda n, i, layer=layer, pick=pick: (layer, pick(n, i), 0)))
        out_specs.append(pl.BlockSpec((None, chunk, cols), lambda n, i, pick=pick: (0, pick(n, i), 0)))
        out_shape.append(jax.ShapeDtypeStruct((1, rows, cols), BF16))
        arrays.append(w)
        passes.append(None if p == n_n else p)
    return in_specs, out_specs, out_shape, arrays, tuple(passes)


def _side_casts(rest, passes):
    n_casts = len(passes)
    for src, dst, p in zip(rest[:n_casts], rest[n_casts + 1:], passes):
        if p is None:
            dst[...] = src[...].astype(BF16)
        else:
            @pl.when(pl.program_id(0) < p)
            def _cast(src=src, dst=dst):
                dst[...] = src[...].astype(BF16)

    return rest[n_casts]


def _inproj_kernel(h_ref, w_ref, cos_ref, sin_ref, qn_ref, kn_ref, *rest, tn, cast_passes):
    o_ref = _side_casts(rest, cast_passes)
    n = pl.program_id(0)
    half = DK // 2
    tm = h_ref.shape[0]
    tr = min(tm, INPROJ_SLAB_ROWS)
    slabs = [(slice(r, r + tr), j) for j in range(tn // DK) for r in range(0, tm, tr)]
    full_slabs = [(slice(0, tm), j) for j in range(tn // DK)]

    def slab(rows, j):
        return _dot(h_ref[rows, :], w_ref[:, j * DK:(j + 1) * DK])

    @pl.when(n < 4)
    def _rotary():
        scale = jnp.where(n < 2, 1.0, DK ** -0.5).astype(F32)
        for rows, j in full_slabs:
            a = slab(rows, j)
            cos = cos_ref[rows, :]
            sin = sin_ref[rows, :]
            t1 = a[:, :half]
            t2 = a[:, half:]
            o_ref[rows, j * DK:j * DK + half] = ((t1 * cos - t2 * sin) * scale).astype(BF16)
            o_ref[rows, j * DK + half:(j + 1) * DK] = ((t1 * sin + t2 * cos) * scale).astype(BF16)

    def _rms_slabs(w, n_chunks):
        for rows, j in slabs:
            if j < n_chunks:
                for i, y in enumerate(_rms_heads(slab(rows, j), w)):
                    o_ref[rows, j * DK + i * HD:j * DK + (i + 1) * HD] = y.astype(BF16)

    @pl.when((n == 8) | (n == 9))
    def _qnorm():
        _rms_slabs(qn_ref[...], tn // DK)

    @pl.when(n == 10)
    def _kv():
        if tn // 2 >= DK:
            _rms_slabs(kn_ref[...], tn // (2 * DK))
        else:
            a = _dot(h_ref[...], w_ref[:, :tn // 2])
            for i, y in enumerate(_rms_heads(a, kn_ref[...])):
                o_ref[:, i * HD:(i + 1) * HD] = y.astype(BF16)
        o_ref[:, tn // 2:] = _dot(h_ref[...], w_ref[:, tn // 2:]).astype(BF16)

    @pl.when((n == 4) | (n == 5))
    def _plain():
        o_ref[...] = _dot(h_ref[...], w_ref[...]).astype(BF16)

    @pl.when((n == 6) | (n == 7))
    def _silu_gate():
        for rows, j in slabs:
            a = slab(rows, j)
            o_ref[rows, j * DK:(j + 1) * DK] = (a * _sigmoid(a)).astype(BF16)

    @pl.when(n > 10)
    def _sigmoid_gate():
        for rows, j in slabs:
            o_ref[rows, j * DK:(j + 1) * DK] = _sigmoid(slab(rows, j)).astype(BF16)


def _inproj(h, w, layer, cos, sin, qn, kn, tm, casts=()):
    m, d = h.shape
    width = w.shape[2]
    tn = d // 2
    nblk = cos.shape[0] // tm
    rope_block = lambda n, i: (jnp.where(n < 4, i % nblk, 0), 0)
    c_in, c_out, c_shape, c_arrays, c_passes = _side_cast_specs(casts, width // tn, m // tm)
    out = pl.pallas_call(
        functools.partial(_inproj_kernel, tn=tn, cast_passes=c_passes),
        grid=(width // tn, m // tm),
        in_specs=[pl.BlockSpec((tm, d), lambda n, i: (i, 0)),
                  pl.BlockSpec((None, d, tn), lambda n, i: (layer, 0, n)),
                  pl.BlockSpec((tm, HD), rope_block), pl.BlockSpec((tm, HD), rope_block),
                  pl.BlockSpec((1, HD), lambda n, i: (0, 0)),
                  pl.BlockSpec((1, HD), lambda n, i: (0, 0))] + c_in,
        out_specs=[pl.BlockSpec((tm, tn), lambda n, i: (i, n))] + c_out,
        out_shape=[jax.ShapeDtypeStruct((m, width), BF16)] + c_shape,
        compiler_params=_cparams(2),
        name="inproj",
    )(h, w, cos, sin, qn, kn, *c_arrays)
    return out[0], list(out[1:])


def _kvproj_kernel(h_ref, w_ref, kn_ref, o_ref):
    kvw = o_ref.shape[1] // 2
    a = _dot(h_ref[...], w_ref[...])
    for i, y in enumerate(_rms_heads(a[:, :kvw], kn_ref[...])):
        o_ref[:, i * HD:(i + 1) * HD] = y
    o_ref[:, kvw:] = a[:, kvw:]


def _kvproj(h, w, layer, kn, tm, n_blocks, row_block):
    d = h.shape[1]
    kvw = d // 4
    col_block = (5 * d) // (2 * kvw)
    return pl.pallas_call(
        _kvproj_kernel,
        grid=(n_blocks,),
        in_specs=[pl.BlockSpec((tm, d), lambda i: (row_block(i), 0)),
                  pl.BlockSpec((None, d, 2 * kvw), lambda i: (layer, 0, col_block)),
                  pl.BlockSpec((1, HD), lambda i: (0, 0))],
        out_specs=pl.BlockSpec((tm, 2 * kvw), lambda i: (i, 0)),
        out_shape=jax.ShapeDtypeStruct((n_blocks * tm, 2 * kvw), F32),
        compiler_params=_cparams(1),
        name="kvproj",
    )(h, w, kn)


def _segment_offsets(d):
    kvw = d // 4
    return dict(q=0, k=d, v=2 * d, g=3 * d, qa=4 * d, ka=5 * d, va=5 * d + kvw,
                mr=5 * d + 2 * kvw, ma=6 * d + 2 * kvw, width=7 * d + 2 * kvw)


def _retention_gate(o, g_act, mr_act):
    on = o * lax.rsqrt(jnp.mean(o * o, axis=-1, keepdims=True) + EPS)
    return mr_act.astype(F32) * (on * g_act.astype(F32))


def _mix_prompt_kernel(sink_ref, sdec_ref, blk_ref, prev_ref, dintra_ref, qdec_ref, kdec_ref,
                       *rest, d):
    out_ref, s_ref, macc_ref = rest[-3:]
    c = pl.program_id(1)
    off = _segment_offsets(d)
    n_ret = d // DK
    n_kv = d // (HD * GROUP)
    kvw = n_kv * HD
    blk = WINDOW

    @pl.when(c == 0)
    def _init():
        s_ref[...] = jnp.zeros_like(s_ref)

    def seg(name, h, width=DK):
        return blk_ref[:, off[name] + h * width:off[name] + (h + 1) * width]

    qi = lax.broadcasted_iota(jnp.int32, (blk, 2 * blk), 0)
    kj = lax.broadcasted_iota(jnp.int32, (blk, 2 * blk), 1)
    diff = kj - qi
    jmin = jnp.where(c == 0, blk, 0)
    mask = (diff >= 0) & (diff <= WINDOW) & (kj >= jmin)
    scale = HD ** -0.5
    kvs = range(n_kv)
    ones = jnp.ones((2 * blk, HD), BF16)
    scores, values = [], []
    for kv in kvs:
        qs = jnp.concatenate([seg("qa", kv * GROUP + g, HD) for g in range(GROUP)], axis=0)
        k2 = jnp.concatenate([prev_ref[:, kv * HD:(kv + 1) * HD], seg("ka", kv, HD)], axis=0)
        v2 = jnp.concatenate([prev_ref[:, kvw + kv * HD:kvw + (kv + 1) * HD], seg("va", kv, HD)], axis=0)
        scores.append(_dot_nt(qs, k2))
        values.append(jnp.concatenate([v2, ones], axis=1))
    heads = range(n_ret)
    att = [_dot_nt(seg("q", h), seg("k", h)) for h in heads]
    inter = [_dot(seg("q", h), s_ref[h].astype(BF16)) for h in heads]
    upd = [_dot_tn((seg("k", h).astype(F32) * kdec_ref[h]).astype(BF16), seg("v", h)) for h in heads]
    for h in heads:
        s_ref[h] = s_ref[h] * sdec_ref[h] + upd[h]
    att = [(att[h] * dintra_ref[h]).astype(BF16) for h in heads]
    o_ret = [_dot(att[h], seg("v", h)) + inter[h] * qdec_ref[h] for h in heads]
    for h in heads:
        macc_ref[:, h * DK:(h + 1) * DK] = _retention_gate(o_ret[h], seg("g", h), seg("mr", h))

    probs, sink_terms = [], []
    for kv in kvs:
        p_rows = []
        for g in range(GROUP):
            sink = sink_ref[kv * GROUP + g] * (1.0 / scale)
            s = jnp.where(mask, scores[kv][g * blk:(g + 1) * blk], NEG_INF)
            m = jnp.maximum(jnp.max(s, axis=-1, keepdims=True), sink)
            p_rows.append(jnp.exp2((s - m) * (scale * LOG2E)).astype(BF16))
            sink_terms.append(jnp.exp2((sink - m) * (scale * LOG2E)))
        probs.append(jnp.concatenate(p_rows, axis=0))
    pv = [_dot(probs[kv], values[kv]) for kv in kvs]
    for kv in kvs:
        for g in range(GROUP):
            head = kv * GROUP + g
            hs = slice(head * HD, (head + 1) * HD)
            o = pv[kv][g * blk:(g + 1) * blk, :HD]
            denom = pv[kv][g * blk:(g + 1) * blk, HD:] + sink_terms[head]
            out_ref[:, hs] = (macc_ref[:, hs] + seg("ma", head, HD).astype(F32) * (o / denom)).astype(BF16)


def _mix_prompt(proj, sinks, tabs, layer, depth, prev_state, b, seq, d):
    off = _segment_offsets(d)
    n_ret = d // DK
    kvw = d // 4
    nblk = seq // WINDOW
    proj3 = proj.reshape(b, seq, off["width"])
    kv_col_block = off["ka"] // (2 * kvw)
    smem = pl.BlockSpec(memory_space=pltpu.SMEM)
    const3 = lambda shape: pl.BlockSpec(shape, lambda i, c: (0, 0, 0))
    in_specs = [smem, smem,
                pl.BlockSpec((None, WINDOW, off["width"]), lambda i, c: (i, c, 0)),
                pl.BlockSpec((None, WINDOW, 2 * kvw),
                             lambda i, c: (i, jnp.maximum(c - 1, 0), kv_col_block)),
                const3((n_ret, WINDOW, WINDOW)), const3((n_ret, WINDOW, DK)),
                const3((n_ret, WINDOW, DK))]
    args = [sinks, tabs["sdec"], proj3, proj3, tabs["dintra"], tabs["qdec"], tabs["kdec"]]
    aliases = {}
    if prev_state is not None:
        aliases[len(args)] = 1
        in_specs.append(pl.BlockSpec(memory_space=pl.ANY))
        args.append(prev_state)
    merged, state = pl.pallas_call(
        functools.partial(_mix_prompt_kernel, d=d),
        grid=(b, nblk),
        in_specs=in_specs,
        out_specs=[pl.BlockSpec((None, WINDOW, d), lambda i, c: (i, c, 0)),
                   pl.BlockSpec((None, None, n_ret, DK, DK), lambda i, c: (layer, i, 0, 0, 0))],
        out_shape=[jax.ShapeDtypeStruct((b, seq, d), BF16),
                   jax.ShapeDtypeStruct((depth, b, n_ret, DK, DK), F32)],
        scratch_shapes=[pltpu.VMEM((WINDOW, d), F32)],
        input_output_aliases=aliases,
        compiler_params=_cparams(2),
        name="mix_prompt",
    )(*args)
    return merged.reshape(b * seq, d), state


def _mix_sample_kernel(sink_ref, sdec_ref, blk_ref, kvnew_ref, s0_ref, kc_ref, vc_ref, dmask_ref,
                       qdec_ref, kdec_ref, *rest, d, dec_seq, n_alias):
    out_ref, s_out_ref, kout_ref, vout_ref, pad_ref, macc_ref = rest[n_alias:]
    off = _segment_offsets(d)
    n_ret = d // DK
    n_kv = d // (HD * GROUP)
    kvw = n_kv * HD
    nb = SAMPLE_BATCHES_PER_STEP
    rows = nb * dec_seq
    pad = pad_ref.shape[0]
    p_k, p_v, p_ka, p_va = 0, d, 2 * d, 2 * d + kvw

    @pl.when(pl.program_id(0) == 0)
    def _init():
        pad_ref[...] = jnp.zeros_like(pad_ref)

    pad_ref[0:rows, p_k:p_k + 2 * d] = blk_ref[:, off["k"]:off["k"] + 2 * d]
    pad_ref[0:rows, p_ka:p_ka + 2 * kvw] = blk_ref[:, off["ka"]:off["ka"] + 2 * kvw]

    tok_bits = dec_seq.bit_length() - 1
    row_batch = lax.broadcasted_iota(jnp.int32, (rows, 1), 0) >> tok_bits
    pad_batch = lax.broadcasted_iota(jnp.int32, (pad, 1), 0) >> tok_bits

    def seg(name, h, width=DK):
        return blk_ref[:, off[name] + h * width:off[name] + (h + 1) * width]

    def padded(col0, h, width=DK):
        return pad_ref[:, col0 + h * width:col0 + (h + 1) * width]

    heads = range(n_ret)
    bis = range(nb)
    att = [_dot_nt(seg("q", h), padded(p_k, h)) for h in heads]
    inter = [[_dot(seg("q", h), s0_ref[bi, h].astype(BF16)) for bi in bis] for h in heads]
    kd = [(padded(p_k, h).astype(F32) * kdec_ref[h]).astype(BF16) for h in heads]
    v_b = [[jnp.where(pad_batch == bi, padded(p_v, h), jnp.zeros((pad, DK), BF16)) for bi in bis]
           for h in heads]
    upd = [[_dot_tn(kd[h], v_b[h][bi]) for bi in bis] for h in heads]
    for h in heads:
        for bi in bis:
            s_out_ref[bi, h] = s0_ref[bi, h] * sdec_ref[h] + upd[h][bi]
    att = [(att[h] * dmask_ref[h]).astype(BF16) for h in heads]
    for h in heads:
        own = inter[h][0]
        for bi in bis[1:]:
            own = jnp.where(row_batch == bi, inter[h][bi], own)
        o = _dot(att[h], padded(p_v, h)) + own * qdec_ref[h]
        macc_ref[:, h * DK:(h + 1) * DK] = _retention_gate(o, seg("g", h), seg("mr", h))

    qrows = GROUP * rows
    q_tok = lax.broadcasted_iota(jnp.int32, (qrows, 1), 0) & (dec_seq - 1)
    q_batch = (lax.broadcasted_iota(jnp.int32, (qrows, 1), 0) & (rows - 1)) >> tok_bits
    cache_j = lax.broadcasted_iota(jnp.int32, (qrows, WINDOW), 1)
    new_j = lax.broadcasted_iota(jnp.int32, (qrows, pad), 1)
    mask_cache = cache_j >= q_tok
    mask_new = [((new_j >> tok_bits) == bi) & ((new_j & (dec_seq - 1)) <= q_tok) for bi in bis]
    scale = HD ** -0.5
    kvs = range(n_kv)
    ones = jnp.ones((WINDOW, HD), BF16)
    qs = [jnp.concatenate([seg("qa", kv * GROUP + g, HD) for g in range(GROUP)], axis=0) for kv in kvs]
    s_new = [_dot_nt(qs[kv], padded(p_ka, kv, HD)) for kv in kvs]
    s_cache = [[_dot_nt(qs[kv], kc_ref[bi, :, kv, :].astype(BF16)) for bi in bis] for kv in kvs]
    v_new = [jnp.concatenate([padded(p_va, kv, HD), ones], axis=1) for kv in kvs]
    v_cache = [[jnp.concatenate([vc_ref[bi, :, kv, :].astype(BF16), ones], axis=1) for bi in bis]
               for kv in kvs]
    for kv in kvs:
        sink_col = jnp.concatenate(
            [jnp.full((rows, 1), sink_ref[kv * GROUP + g], F32) for g in range(GROUP)], axis=0)
        o_kv = None
        for bi in bis:
            s_c = jnp.where(mask_cache, s_cache[kv][bi] * scale, NEG_INF)
            s_n = jnp.where(mask_new[bi], s_new[kv] * scale, NEG_INF)
            m = jnp.maximum(jnp.maximum(jnp.max(s_c, axis=-1, keepdims=True),
                                        jnp.max(s_n, axis=-1, keepdims=True)), sink_col)
            pv = (_dot(jnp.exp(s_c - m).astype(BF16), v_cache[kv][bi])
                  + _dot(jnp.exp(s_n - m).astype(BF16), v_new[kv]))
            o = pv[:, :HD] / (pv[:, HD:] + jnp.exp(sink_col - m))
            o_kv = o if o_kv is None else jnp.where(q_batch == bi, o, o_kv)
        for g in range(GROUP):
            head = kv * GROUP + g
            hs = slice(head * HD, (head + 1) * HD)
            out_ref[:, hs] = (macc_ref[:, hs] + seg("ma", head, HD).astype(F32)
                              * o_kv[g * rows:(g + 1) * rows]).astype(BF16)

    keep = WINDOW - dec_seq
    for bi in range(nb):
        kout_ref[bi, 0:keep] = kc_ref[bi, dec_seq:WINDOW]
        vout_ref[bi, 0:keep] = vc_ref[bi, dec_seq:WINDOW]
        for t in range(dec_seq):
            r = bi * dec_seq + t
            for kv in range(n_kv):
                kout_ref[bi, keep + t, kv:kv + 1, :] = kvnew_ref[r:r + 1, kv * HD:(kv + 1) * HD]
                vout_ref[bi, keep + t, kv:kv + 1, :] = kvnew_ref[r:r + 1, kvw + kv * HD:kvw + (kv + 1) * HD]


def _mix_sample(proj, kvnew, sinks, tabs, layer, state_all, k_all, v_all, prev_outs, dec_seq, d):
    off = _segment_offsets(d)
    n_ret = d // DK
    n_kv = d // (HD * GROUP)
    kvw = n_kv * HD
    nb = SAMPLE_BATCHES_PER_STEP
    b = state_all.shape[1]
    rows = nb * dec_seq
    smem = pl.BlockSpec(memory_space=pltpu.SMEM)
    const3 = lambda shape: pl.BlockSpec(shape, lambda i: (0, 0, 0))
    state_spec = lambda: pl.BlockSpec((None, nb, n_ret, DK, DK), lambda i: (layer, i, 0, 0, 0))
    win_spec = lambda: pl.BlockSpec((None, nb, WINDOW, n_kv, HD), lambda i: (layer, i, 0, 0, 0))
    in_specs = [smem, smem,
                pl.BlockSpec((rows, off["width"]), lambda i: (i, 0)),
                pl.BlockSpec((rows, 2 * kvw), lambda i: (i, 0)),
                state_spec(), win_spec(), win_spec(),
                const3((n_ret, rows, WINDOW)), const3((n_ret, rows, DK)), const3((n_ret, WINDOW, DK))]
    args = [sinks, tabs["sdec"], proj, kvnew, state_all, k_all, v_all,
            tabs["dmask"], tabs["qdec"], tabs["kdec"]]
    aliases = {}
    if prev_outs is not None:
        for j, arr in enumerate(prev_outs):
            aliases[len(args)] = 1 + j
            in_specs.append(pl.BlockSpec(memory_space=pl.ANY))
            args.append(arr)
    n_alias = len(aliases)
    merged, state, kwin, vwin = pl.pallas_call(
        functools.partial(_mix_sample_kernel, d=d, dec_seq=dec_seq, n_alias=n_alias),
        grid=(b // nb,),
        in_specs=in_specs,
        out_specs=[pl.BlockSpec((rows, d), lambda i: (i, 0)), state_spec(), win_spec(), win_spec()],
        out_shape=[jax.ShapeDtypeStruct((b * dec_seq, d), BF16),
                   jax.ShapeDtypeStruct(state_all.shape, F32),
                   jax.ShapeDtypeStruct(k_all.shape, F32),
                   jax.ShapeDtypeStruct(v_all.shape, F32)],
        scratch_shapes=[pltpu.VMEM((WINDOW, 2 * d + 2 * kvw), BF16),
                        pltpu.VMEM((rows, d), F32)],
        input_output_aliases=aliases,
        compiler_params=_cparams(1),
        name="mix_sample",
    )(*args)
    return merged, (state, kwin, vwin)


def _proj_residual_kernel(a_ref, w_ref, x_ref, g_ref, *rest, with_norm):
    if with_norm:
        lnw_ref, sc_ref, sh_ref, xo_ref, ho_ref = rest
    else:
        (xo_ref,) = rest
    d = xo_ref.shape[1]
    half = DK // 2
    slabs = [slice(j * DK, (j + 1) * DK) for j in range(d // DK)]
    sq = None
    for cs in slabs:
        x = x_ref[:, cs] + g_ref[:, cs] * _dot(a_ref[...], w_ref[:, cs])
        xo_ref[:, cs] = x
        if with_norm:
            x2 = x * x
            part = x2[:, :half] + x2[:, half:]
            sq = part if sq is None else sq + part
    if with_norm:
        r = lax.rsqrt(jnp.sum(sq, axis=-1, keepdims=True) / d + EPS)
        for cs in slabs:
            y = (xo_ref[:, cs] * r) * lnw_ref[:, cs]
            ho_ref[:, cs] = (y * (1.0 + sc_ref[:, cs]) + sh_ref[:, cs]).astype(BF16)


def _proj_residual(a, w, layer, x, gate, norm, tm, name):
    g, r, d = x.shape
    kdim = w.shape[1]
    tok = lambda: pl.BlockSpec((None, tm, d), lambda i, j: (i, j, 0))
    in_specs = [pl.BlockSpec((None, tm, kdim), lambda i, j: (i, j, 0)),
                pl.BlockSpec((None, kdim, d), lambda i, j: (layer, 0, 0), pipeline_mode=pl.Buffered(1)),
                tok(), _mod_spec(gate, tm)]
    args = [a.reshape(g, r, kdim), w, x, gate.arr]
    out_specs = [tok()]
    out_shape = [jax.ShapeDtypeStruct((g, r, d), F32)]
    if norm is not None:
        lnw, sc, sh = norm
        in_specs += [pl.BlockSpec((1, d), lambda i, j: (0, 0)), _mod_spec(sc, tm), _mod_spec(sh, tm)]
        args += [lnw, sc.arr, sh.arr]
        out_specs.append(tok())
        out_shape.append(jax.ShapeDtypeStruct((g, r, d), BF16))
    out = pl.pallas_call(
        functools.partial(_proj_residual_kernel, with_norm=norm is not None),
        grid=(g, r // tm),
        in_specs=in_specs, out_specs=out_specs, out_shape=out_shape,
        compiler_params=_cparams(2),
        name=name,
    )(*args)
    return (out[0], out[1]) if norm is not None else (out[0], None)


def _up_kernel(h_ref, w_ref, *rest, cast_passes):
    o_ref = _side_casts(rest, cast_passes)
    r = jnp.maximum(_dot(h_ref[...], w_ref[...]), 0.0)
    o_ref[...] = (r * r).astype(BF16)


def _up(h, w, layer, tm, casts=()):
    m, d = h.shape
    f = w.shape[2]
    tn = min(1024, f)
    c_in, c_out, c_shape, c_arrays, c_passes = _side_cast_specs(casts, f // tn, m // tm)
    out = pl.pallas_call(
        functools.partial(_up_kernel, cast_passes=c_passes),
        grid=(f // tn, m // tm),
        in_specs=[pl.BlockSpec((tm, d), lambda n, i: (i, 0)),
                  pl.BlockSpec((None, d, tn), lambda n, i: (layer, 0, n))] + c_in,
        out_specs=[pl.BlockSpec((tm, tn), lambda n, i: (i, n))] + c_out,
        out_shape=[jax.ShapeDtypeStruct((m, f), BF16)] + c_shape,
        compiler_params=_cparams(2),
        name="mlp_up",
    )(h, w, *c_arrays)
    return out[0], list(out[1:])


def _rope_tables(pos):
    half = DK // 2
    inv = 1.0 / (ROPE_BASE ** (jnp.arange(half, dtype=F32) / half))
    ang = pos.astype(F32)[:, None] * inv[None, :]
    return jnp.cos(ang), jnp.sin(ang)


def _decay_tables(n_ret, chunk, reps, pad_cols):
    log_g = jnp.log1p(-jnp.exp2(-5.0 - jnp.arange(n_ret, dtype=F32)))
    rows = reps * chunk
    r = jnp.arange(rows)
    cidx = jnp.arange(pad_cols)
    idx = (r % chunk).astype(F32)
    diff = idx[:, None] - (cidx % chunk).astype(F32)[None, :]
    ok = (diff >= 0) & ((r // chunk)[:, None] == (cidx // chunk)[None, :]) & (cidx < rows)[None, :]
    dintra = jnp.where(ok, jnp.exp(jnp.where(ok, diff, 0.0) * log_g[:, None, None]), 0.0)
    qdec = jnp.exp((idx + 1.0) * log_g[:, None])[..., None]
    kidx = (jnp.arange(max(rows, pad_cols)) % chunk).astype(F32)
    kdec = jnp.exp((chunk - 1.0 - kidx) * log_g[:, None])[..., None]
    sdec = jnp.exp(chunk * log_g)
    return dict(dintra=dintra, dmask=dintra,
                qdec=jnp.broadcast_to(qdec, (n_ret, rows, DK)),
                kdec=jnp.broadcast_to(kdec, (n_ret, kidx.shape[0], DK)), sdec=sdec)


def kernel(x_prompt, x_sample, c_prompt, c_sample, state_ret, cache_k_win, cache_v_win, norm1_w,
           norm2_w, w_ada, b_ada, w_in, q_norm_w, k_norm_w, sinks, w_out, w_up, w_down):
    bp, seq, d = x_prompt.shape
    bs, dec_seq, _ = x_sample.shape
    depth = w_in.shape[0]
    n_ret = d // DK
    kvw = d // 4
    n_kv = kvw // HD
    win = cache_k_win.shape[2]
    assert win == WINDOW and seq % WINDOW == 0 and bs % SAMPLE_BATCHES_PER_STEP == 0
    assert dec_seq & (dec_seq - 1) == 0

    ms_rows = bs * dec_seq
    n_c = ms_rows + bp
    c_rows = -(-n_c // 16) * 16
    c_all = jnp.concatenate([jnp.repeat(c_sample, dec_seq, axis=0), c_prompt,
                             jnp.zeros((c_rows - n_c, d), F32)], axis=0)
    mods = _ada(c_all, w_ada, b_ada)

    mods_prompt = mods[:, ms_rows:n_c].reshape(depth * bp * 6, 1, d)

    def group_mods(l):
        return ([_Mod(mods_prompt, l, j, bp) for j in range(6)],
                [_Mod(mods, l, j, 1) for j in range(6)])

    wi = w_in[0:1].astype(BF16)

    cos_p, sin_p = _rope_tables(jnp.arange(seq, dtype=jnp.int32))
    cos_s, sin_s = _rope_tables(PAST_LEN + jnp.arange(dec_seq, dtype=jnp.int32))
    cos_s = jnp.tile(cos_s, (bs, 1))
    sin_s = jnp.tile(sin_s, (bs, 1))
    tabs_p = _decay_tables(n_ret, WINDOW, 1, WINDOW)
    tabs_s = _decay_tables(n_ret, dec_seq, SAMPLE_BATCHES_PER_STEP, WINDOW)

    tm_p = min(512, seq)
    tm_down = min(256, seq)
    ms_rows = bs * dec_seq
    nblk = seq // WINDOW
    xp = x_prompt
    xs = x_sample.reshape(1, ms_rows, d)

    mods_p, mods_s = group_mods(0)
    hp = _prenorm(xp, norm1_w[0:1], mods_p[1], mods_p[0], tm_p)
    hs = _prenorm(xs, norm1_w[0:1], mods_s[1], mods_s[0], ms_rows)

    outs = {k: [] for k in ("kp", "vp")}
    prompt_state = None
    sample_state = None
    for l in range(depth):
        qn = q_norm_w[l:l + 1]
        kn = k_norm_w[l:l + 1]
        if l + 1 < depth:
            nxt_p, nxt_s = group_mods(l + 1)
            norm_p = (norm1_w[l + 1:l + 2], nxt_p[1], nxt_p[0])
            norm_s = (norm1_w[l + 1:l + 2], nxt_s[1], nxt_s[0])
        else:
            norm_p = norm_s = None

        hp2 = hp.reshape(bp * seq, d)
        proj, (wu, wo) = _inproj(hp2, wi, 0, cos_p, sin_p, qn, kn, min(1024, seq),
                                 casts=[(w_up, l), (w_out, l)])
        kv_tail = _kvproj(hp2, wi, 0, kn, WINDOW, bp, lambda i: i * nblk + nblk - 1)
        merged, prompt_state = _mix_prompt(proj, sinks[l], tabs_p, l, depth, prompt_state, bp, seq, d)
        outs["kp"].append(kv_tail[:, :kvw].reshape(bp, WINDOW, n_kv, HD))
        outs["vp"].append(kv_tail[:, kvw:].reshape(bp, WINDOW, n_kv, HD))
        xp, h2 = _proj_residual(merged, wo, 0, xp, mods_p[2], (norm2_w[l:l + 1], mods_p[4], mods_p[3]),
                                tm_p, "outproj")
        next_w_in = [(w_in, l + 1)] if l + 1 < depth else []
        u, (wd, *wi_next) = _up(h2.reshape(bp * seq, d), wu, 0, min(1024, seq),
                                casts=[(w_down, l)] + next_w_in)
        xp, hp = _proj_residual(u, wd, 0, xp, mods_p[5], norm_p, tm_down, "mlp_down")

        hs2 = hs.reshape(ms_rows, d)
        proj, _ = _inproj(hs2, wi, 0, cos_s, sin_s, qn, kn, ms_rows)
        kv_new = _kvproj(hs2, wi, 0, kn, ms_rows, 1, lambda i: i)
        merged, sample_state = _mix_sample(proj, kv_new, sinks[l], tabs_s, l, state_ret, cache_k_win,
                                           cache_v_win, sample_state, dec_seq, d)
        xs, h2 = _proj_residual(merged, wo, 0, xs, mods_s[2], (norm2_w[l:l + 1], mods_s[4], mods_s[3]),
                                ms_rows, "outproj")
        u, _ = _up(h2.reshape(ms_rows, d), wu, 0, ms_rows)
        xs, hs = _proj_residual(u, wd, 0, xs, mods_s[5], norm_s, min(tm_down // 2, ms_rows), "mlp_down")

        if l + 1 < depth:
            mods_p, mods_s = nxt_p, nxt_s
            wi = wi_next[0]

    return (xp, xs.reshape(bs, dec_seq, d), prompt_state, jnp.stack(outs["kp"]),
            jnp.stack(outs["vp"])) + tuple(sample_state)
```

```python
import functools
import math

import jax
import jax.numpy as jnp
from jax import lax
from jax.experimental import pallas as pl
from jax.experimental.pallas import tpu as pltpu

F32 = jnp.float32
BF16 = jnp.bfloat16

DK = 256
HD = 128
GROUP = 4
WINDOW = 128
ROPE_BASE = 10000.0
EPS = 1e-6
NEG_INF = -1e30
LOG2E = math.log2(math.e)
PAST_LEN = 8192

V7X_VMEM_BYTES = 64 * 1024 * 1024
VMEM_LIMIT = V7X_VMEM_BYTES * 7 // 8
BF16_SUBLANES = 16
SAMPLE_BATCHES_PER_STEP = 4
INPROJ_SLAB_ROWS = 256


def _tile_plan(seq, decode_rows):
    return dict(prompt=dict(wide=min(1024, seq), outproj=min(512, seq), down=min(256, seq)),
                decode=dict(wide=decode_rows, outproj=decode_rows, down=min(128, decode_rows)))


def _cparams(n_axes):
    return pltpu.CompilerParams(dimension_semantics=("arbitrary",) * n_axes,
                                vmem_limit_bytes=VMEM_LIMIT)


def _sigmoid(x):
    return 1.0 / (1.0 + jnp.exp(-x))


def _modnorm(x, lnw, sc, sh):
    y = x * lax.rsqrt(jnp.mean(x * x, axis=-1, keepdims=True) + EPS)
    return (y * lnw) * (1.0 + sc) + sh


def _dot(a, b):
    return jnp.dot(a, b, preferred_element_type=F32)


def _dot_nt(a, b):
    return lax.dot_general(a, b, (((1,), (1,)), ((), ())), preferred_element_type=F32)


def _dot_tn(a, b):
    return lax.dot_general(a, b, (((0,), (0,)), ((), ())), preferred_element_type=F32)


def _ada_kernel(c_ref, w_ref, b_ref, o_ref):
    c = c_ref[...]
    s = (c * _sigmoid(c)).astype(BF16)
    o_ref[...] = _dot(s, w_ref[...].astype(BF16)) + b_ref[...]


def _ada(c_all, w_ada, b_ada):
    depth, d, n6 = w_ada.shape
    rows = c_all.shape[0]
    tn = min(1024, n6)
    return pl.pallas_call(
        _ada_kernel,
        grid=(depth, n6 // tn),
        in_specs=[pl.BlockSpec((rows, d), lambda l, n: (0, 0)),
                  pl.BlockSpec((None, d, tn), lambda l, n: (l, 0, n)),
                  pl.BlockSpec((None, 1, tn), lambda l, n: (l, 0, n))],
        out_specs=pl.BlockSpec((None, rows, tn), lambda l, n: (l, 0, n)),
        out_shape=jax.ShapeDtypeStruct((depth, rows, n6), F32),
        compiler_params=_cparams(2),
        name="ada",
    )(c_all, w_ada, b_ada.reshape(depth, 1, n6))


def _prenorm_kernel(x_ref, lnw_ref, sc_ref, sh_ref, h_ref):
    h_ref[...] = _modnorm(x_ref[...], lnw_ref[...], sc_ref[...], sh_ref[...]).astype(BF16)


class _Mod:
    def __init__(self, arr, layer, part, groups):
        self.arr, self.layer, self.part, self.groups = arr, layer, part, groups

    def spec(self, tm):
        l, j, arr = self.layer, self.part, self.arr
        if arr.shape[1] == 1:
            base = l * self.groups * 6 + j
            return pl.BlockSpec((None, 1, arr.shape[2]), lambda g, r: (base + g * 6, 0, 0))
        return pl.BlockSpec((None, tm, arr.shape[2] // 6), lambda g, r: (l, r, j))


def _mod_spec(mod, tm):
    return mod.spec(tm)


def _prenorm(x, lnw, sc, sh, tm):
    g, r, d = x.shape
    return pl.pallas_call(
        _prenorm_kernel,
        grid=(g, r // tm),
        in_specs=[pl.BlockSpec((None, tm, d), lambda i, j: (i, j, 0)),
                  pl.BlockSpec((1, d), lambda i, j: (0, 0)),
                  _mod_spec(sc, tm), _mod_spec(sh, tm)],
        out_specs=pl.BlockSpec((None, tm, d), lambda i, j: (i, j, 0)),
        out_shape=jax.ShapeDtypeStruct((g, r, d), BF16),
        compiler_params=_cparams(2),
        name="prenorm",
    )(x, lnw, sc.arr, sh.arr)


def _rms_heads(a, w):
    outs = []
    for j in range(a.shape[1] // HD):
        t = a[:, j * HD:(j + 1) * HD]
        outs.append(t * lax.rsqrt(jnp.mean(t * t, axis=-1, keepdims=True) + EPS) * w)
    return outs


def _side_cast_specs(casts, n_n, n_i):
    in_specs, out_specs, out_shape, arrays, passes = [], [], [], [], []
    for w, layer in casts:
        _, rows, cols = w.shape
        p = max(q for q in range(1, n_n + 1)
                if rows % (q * n_i) == 0 and (rows // (q * n_i)) % BF16_SUBLANES == 0)
        chunk = rows // (p * n_i)
        pick = lambda n, i, p=p: jnp.where(n < p, n * n_i + i, p * n_i - 1)
        in_specs.append(pl.BlockSpec((None, chunk, cols), lambda n, i, layer=layer, pick=pick: (layer, pick(n, i), 0)))
        out_specs.append(pl.BlockSpec((None, chunk, cols), lambda n, i, pick=pick: (0, pick(n, i), 0)))
        out_shape.append(jax.ShapeDtypeStruct((1, rows, cols), BF16))
        arrays.append(w)
        passes.append(None if p == n_n else p)
    return in_specs, out_specs, out_shape, arrays, tuple(passes)


def _side_casts(rest, passes):
    n_casts = len(passes)
    for src, dst, p in zip(rest[:n_casts], rest[n_casts + 1:], passes):
        if p is None:
            dst[...] = src[...].astype(BF16)
        else:
            @pl.when(pl.program_id(0) < p)
            def _cast(src=src, dst=dst):
                dst[...] = src[...].astype(BF16)

    return rest[n_casts]


def _inproj_kernel(h_ref, w_ref, cos_ref, sin_ref, qn_ref, kn_ref, *rest, tn, cast_passes):
    o_ref = _side_casts(rest, cast_passes)
    n = pl.program_id(0)
    half = DK // 2
    tm = h_ref.shape[0]
    tr = min(tm, INPROJ_SLAB_ROWS)
    slabs = [(slice(r, r + tr), j) for j in range(tn // DK) for r in range(0, tm, tr)]
    full_slabs = [(slice(0, tm), j) for j in range(tn // DK)]

    def slab(rows, j):
        return _dot(h_ref[rows, :], w_ref[:, j * DK:(j + 1) * DK])

    @pl.when(n < 4)
    def _rotary():
        scale = jnp.where(n < 2, 1.0, DK ** -0.5).astype(F32)
        for rows, j in full_slabs:
            a = slab(rows, j)
            cos = cos_ref[rows, :]
            sin = sin_ref[rows, :]
            t1 = a[:, :half]
            t2 = a[:, half:]
            o_ref[rows, j * DK:j * DK + half] = ((t1 * cos - t2 * sin) * scale).astype(BF16)
            o_ref[rows, j * DK + half:(j + 1) * DK] = ((t1 * sin + t2 * cos) * scale).astype(BF16)

    def _rms_slabs(w, n_chunks):
        for rows, j in slabs:
            if j < n_chunks:
                for i, y in enumerate(_rms_heads(slab(rows, j), w)):
                    o_ref[rows, j * DK + i * HD:j * DK + (i + 1) * HD] = y.astype(BF16)

    @pl.when((n == 8) | (n == 9))
    def _qnorm():
        _rms_slabs(qn_ref[...], tn // DK)

    @pl.when(n == 10)
    def _kv():
        if tn // 2 >= DK:
            _rms_slabs(kn_ref[...], tn // (2 * DK))
        else:
            a = _dot(h_ref[...], w_ref[:, :tn // 2])
            for i, y in enumerate(_rms_heads(a, kn_ref[...])):
                o_ref[:, i * HD:(i + 1) * HD] = y.astype(BF16)
        o_ref[:, tn // 2:] = _dot(h_ref[...], w_ref[:, tn // 2:]).astype(BF16)

    @pl.when((n == 4) | (n == 5))
    def _plain():
        o_ref[...] = _dot(h_ref[...], w_ref[...]).astype(BF16)

    @pl.when((n == 6) | (n == 7))
    def _silu_gate():
        for rows, j in slabs:
            a = slab(rows, j)
            o_ref[rows, j * DK:(j + 1) * DK] = (a * _sigmoid(a)).astype(BF16)

    @pl.when(n > 10)
    def _sigmoid_gate():
        for rows, j in slabs:
            o_ref[rows, j * DK:(j + 1) * DK] = _sigmoid(slab(rows, j)).astype(BF16)


def _inproj(h, w, layer, cos, sin, qn, kn, tm, casts=()):
    m, d = h.shape
    width = w.shape[2]
    tn = d // 2
    nblk = cos.shape[0] // tm
    rope_block = lambda n, i: (jnp.where(n < 4, i % nblk, 0), 0)
    c_in, c_out, c_shape, c_arrays, c_passes = _side_cast_specs(casts, width // tn, m // tm)
    out = pl.pallas_call(
        functools.partial(_inproj_kernel, tn=tn, cast_passes=c_passes),
        grid=(width // tn, m // tm),
        in_specs=[pl.BlockSpec((tm, d), lambda n, i: (i, 0)),
                  pl.BlockSpec((None, d, tn), lambda n, i: (layer, 0, n)),
                  pl.BlockSpec((tm, HD), rope_block), pl.BlockSpec((tm, HD), rope_block),
                  pl.BlockSpec((1, HD), lambda n, i: (0, 0)),
                  pl.BlockSpec((1, HD), lambda n, i: (0, 0))] + c_in,
        out_specs=[pl.BlockSpec((tm, tn), lambda n, i: (i, n))] + c_out,
        out_shape=[jax.ShapeDtypeStruct((m, width), BF16)] + c_shape,
        compiler_params=_cparams(2),
        name="inproj",
    )(h, w, cos, sin, qn, kn, *c_arrays)
    return out[0], list(out[1:])


def _kvproj_kernel(h_ref, w_ref, kn_ref, o_ref):
    kvw = o_ref.shape[1] // 2
    a = _dot(h_ref[...], w_ref[...])
    for i, y in enumerate(_rms_heads(a[:, :kvw], kn_ref[...])):
        o_ref[:, i * HD:(i + 1) * HD] = y
    o_ref[:, kvw:] = a[:, kvw:]


def _kvproj(h, w, layer, kn, tm, n_blocks, row_block):
    d = h.shape[1]
    kvw = d // 4
    col_block = (5 * d) // (2 * kvw)
    return pl.pallas_call(
        _kvproj_kernel,
        grid=(n_blocks,),
        in_specs=[pl.BlockSpec((tm, d), lambda i: (row_block(i), 0)),
                  pl.BlockSpec((None, d, 2 * kvw), lambda i: (layer, 0, col_block)),
                  pl.BlockSpec((1, HD), lambda i: (0, 0))],
        out_specs=pl.BlockSpec((tm, 2 * kvw), lambda i: (i, 0)),
        out_shape=jax.ShapeDtypeStruct((n_blocks * tm, 2 * kvw), F32),
        compiler_params=_cparams(1),
        name="kvproj",
    )(h, w, kn)


def _segment_offsets(d):
    kvw = d // 4
    return dict(q=0, k=d, v=2 * d, g=3 * d, qa=4 * d, ka=5 * d, va=5 * d + kvw,
                mr=5 * d + 2 * kvw, ma=6 * d + 2 * kvw, width=7 * d + 2 * kvw)


def _retention_gate(o, g_act, mr_act):
    on = o * lax.rsqrt(jnp.mean(o * o, axis=-1, keepdims=True) + EPS)
    return mr_act.astype(F32) * (on * g_act.astype(F32))


def _mix_prompt_kernel(sink_ref, sdec_ref, blk_ref, prev_ref, dintra_ref, qdec_ref, kdec_ref,
                       *rest, d):
    out_ref, s_ref, macc_ref = rest[-3:]
    c = pl.program_id(1)
    off = _segment_offsets(d)
    n_ret = d // DK
    n_kv = d // (HD * GROUP)
    kvw = n_kv * HD
    blk = WINDOW

    @pl.when(c == 0)
    def _init():
        s_ref[...] = jnp.zeros_like(s_ref)

    def seg(name, h, width=DK):
        return blk_ref[:, off[name] + h * width:off[name] + (h + 1) * width]

    qi = lax.broadcasted_iota(jnp.int32, (blk, 2 * blk), 0)
    kj = lax.broadcasted_iota(jnp.int32, (blk, 2 * blk), 1)
    diff = kj - qi
    jmin = jnp.where(c == 0, blk, 0)
    mask = (diff >= 0) & (diff <= WINDOW) & (kj >= jmin)
    scale = HD ** -0.5
    kvs = range(n_kv)
    ones = jnp.ones((2 * blk, HD), BF16)
    scores, values = [], []
    for kv in kvs:
        qs = jnp.concatenate([seg("qa", kv * GROUP + g, HD) for g in range(GROUP)], axis=0)
        k2 = jnp.concatenate([prev_ref[:, kv * HD:(kv + 1) * HD], seg("ka", kv, HD)], axis=0)
        v2 = jnp.concatenate([prev_ref[:, kvw + kv * HD:kvw + (kv + 1) * HD], seg("va", kv, HD)], axis=0)
        scores.append(_dot_nt(qs, k2))
        values.append(jnp.concatenate([v2, ones], axis=1))
    heads = range(n_ret)
    att = [_dot_nt(seg("q", h), seg("k", h)) for h in heads]
    inter = [_dot(seg("q", h), s_ref[h].astype(BF16)) for h in heads]
    upd = [_dot_tn((seg("k", h).astype(F32) * kdec_ref[h]).astype(BF16), seg("v", h)) for h in heads]
    for h in heads:
        s_ref[h] = s_ref[h] * sdec_ref[h] + upd[h]
    att = [(att[h] * dintra_ref[h]).astype(BF16) for h in heads]
    o_ret = [_dot(att[h], seg("v", h)) + inter[h] * qdec_ref[h] for h in heads]
    for h in heads:
        macc_ref[:, h * DK:(h + 1) * DK] = _retention_gate(o_ret[h], seg("g", h), seg("mr", h))

    probs, sink_terms = [], []
    for kv in kvs:
        p_rows = []
        for g in range(GROUP):
            sink = sink_ref[kv * GROUP + g] * (1.0 / scale)
            s = jnp.where(mask, scores[kv][g * blk:(g + 1) * blk], NEG_INF)
            m = jnp.maximum(jnp.max(s, axis=-1, keepdims=True), sink)
            p_rows.append(jnp.exp2((s - m) * (scale * LOG2E)).astype(BF16))
            sink_terms.append(jnp.exp2((sink - m) * (scale * LOG2E)))
        probs.append(jnp.concatenate(p_rows, axis=0))
    pv = [_dot(probs[kv], values[kv]) for kv in kvs]
    for kv in kvs:
        for g in range(GROUP):
            head = kv * GROUP + g
            hs = slice(head * HD, (head + 1) * HD)
            o = pv[kv][g * blk:(g + 1) * blk, :HD]
            denom = pv[kv][g * blk:(g + 1) * blk, HD:] + sink_terms[head]
            out_ref[:, hs] = (macc_ref[:, hs] + seg("ma", head, HD).astype(F32) * (o / denom)).astype(BF16)


def _mix_prompt(proj, sinks, tabs, layer, depth, prev_state, b, seq, d):
    off = _segment_offsets(d)
    n_ret = d // DK
    kvw = d // 4
    nblk = seq // WINDOW
    proj3 = proj.reshape(b, seq, off["width"])
    kv_col_block = off["ka"] // (2 * kvw)
    smem = pl.BlockSpec(memory_space=pltpu.SMEM)
    const3 = lambda shape: pl.BlockSpec(shape, lambda i, c: (0, 0, 0))
    in_specs = [smem, smem,
                pl.BlockSpec((None, WINDOW, off["width"]), lambda i, c: (i, c, 0)),
                pl.BlockSpec((None, WINDOW, 2 * kvw),
                             lambda i, c: (i, jnp.maximum(c - 1, 0), kv_col_block)),
                const3((n_ret, WINDOW, WINDOW)), const3((n_ret, WINDOW, DK)),
                const3((n_ret, WINDOW, DK))]
    args = [sinks, tabs["sdec"], proj3, proj3, tabs["dintra"], tabs["qdec"], tabs["kdec"]]
    aliases = {}
    if prev_state is not None:
        aliases[len(args)] = 1
        in_specs.append(pl.BlockSpec(memory_space=pl.ANY))
        args.append(prev_state)
    merged, state = pl.pallas_call(
        functools.partial(_mix_prompt_kernel, d=d),
        grid=(b, nblk),
        in_specs=in_specs,
        out_specs=[pl.BlockSpec((None, WINDOW, d), lambda i, c: (i, c, 0)),
                   pl.BlockSpec((None, None, n_ret, DK, DK), lambda i, c: (layer, i, 0, 0, 0))],
        out_shape=[jax.ShapeDtypeStruct((b, seq, d), BF16),
                   jax.ShapeDtypeStruct((depth, b, n_ret, DK, DK), F32)],
        scratch_shapes=[pltpu.VMEM((WINDOW, d), F32)],
        input_output_aliases=aliases,
        compiler_params=_cparams(2),
        name="mix_prompt",
    )(*args)
    return merged.reshape(b * seq, d), state


def _mix_sample_kernel(sink_ref, sdec_ref, blk_ref, kvnew_ref, s0_ref, kc_ref, vc_ref, dmask_ref,
                       qdec_ref, kdec_ref, *rest, d, dec_seq, n_alias):
    out_ref, s_out_ref, kout_ref, vout_ref, pad_ref, macc_ref = rest[n_alias:]
    off = _segment_offsets(d)
    n_ret = d // DK
    n_kv = d // (HD * GROUP)
    kvw = n_kv * HD
    nb = SAMPLE_BATCHES_PER_STEP
    rows = nb * dec_seq
    pad = pad_ref.shape[0]
    p_k, p_v, p_ka, p_va = 0, d, 2 * d, 2 * d + kvw

    @pl.when(pl.program_id(0) == 0)
    def _init():
        pad_ref[...] = jnp.zeros_like(pad_ref)

    pad_ref[0:rows, p_k:p_k + 2 * d] = blk_ref[:, off["k"]:off["k"] + 2 * d]
    pad_ref[0:rows, p_ka:p_ka + 2 * kvw] = blk_ref[:, off["ka"]:off["ka"] + 2 * kvw]

    tok_bits = dec_seq.bit_length() - 1
    row_batch = lax.broadcasted_iota(jnp.int32, (rows, 1), 0) >> tok_bits
    pad_batch = lax.broadcasted_iota(jnp.int32, (pad, 1), 0) >> tok_bits

    def seg(name, h, width=DK):
        return blk_ref[:, off[name] + h * width:off[name] + (h + 1) * width]

    def padded(col0, h, width=DK):
        return pad_ref[:, col0 + h * width:col0 + (h + 1) * width]

    heads = range(n_ret)
    bis = range(nb)
    att = [_dot_nt(seg("q", h), padded(p_k, h)) for h in heads]
    inter = [[_dot(seg("q", h), s0_ref[bi, h].astype(BF16)) for bi in bis] for h in heads]
    kd = [(padded(p_k, h).astype(F32) * kdec_ref[h]).astype(BF16) for h in heads]
    v_b = [[jnp.where(pad_batch == bi, padded(p_v, h), jnp.zeros((pad, DK), BF16)) for bi in bis]
           for h in heads]
    upd = [[_dot_tn(kd[h], v_b[h][bi]) for bi in bis] for h in heads]
    for h in heads:
        for bi in bis:
            s_out_ref[bi, h] = s0_ref[bi, h] * sdec_ref[h] + upd[h][bi]
    att = [(att[h] * dmask_ref[h]).astype(BF16) for h in heads]
    for h in heads:
        own = inter[h][0]
        for bi in bis[1:]:
            own = jnp.where(row_batch == bi, inter[h][bi], own)
        o = _dot(att[h], padded(p_v, h)) + own * qdec_ref[h]
        macc_ref[:, h * DK:(h + 1) * DK] = _retention_gate(o, seg("g", h), seg("mr", h))

    qrows = GROUP * rows
    q_tok = lax.broadcasted_iota(jnp.int32, (qrows, 1), 0) & (dec_seq - 1)
    q_batch = (lax.broadcasted_iota(jnp.int32, (qrows, 1), 0) & (rows - 1)) >> tok_bits
    cache_j = lax.broadcasted_iota(jnp.int32, (qrows, WINDOW), 1)
    new_j = lax.broadcasted_iota(jnp.int32, (qrows, pad), 1)
    mask_cache = cache_j >= q_tok
    mask_new = [((new_j >> tok_bits) == bi) & ((new_j & (dec_seq - 1)) <= q_tok) for bi in bis]
    scale = HD ** -0.5
    kvs = range(n_kv)
    ones = jnp.ones((WINDOW, HD), BF16)
    qs = [jnp.concatenate([seg("qa", kv * GROUP + g, HD) for g in range(GROUP)], axis=0) for kv in kvs]
    s_new = [_dot_nt(qs[kv], padded(p_ka, kv, HD)) for kv in kvs]
    s_cache = [[_dot_nt(qs[kv], kc_ref[bi, :, kv, :].astype(BF16)) for bi in bis] for kv in kvs]
    v_new = [jnp.concatenate([padded(p_va, kv, HD), ones], axis=1) for kv in kvs]
    v_cache = [[jnp.concatenate([vc_ref[bi, :, kv, :].astype(BF16), ones], axis=1) for bi in bis]
               for kv in kvs]
    for kv in kvs:
        sink_col = jnp.concatenate(
            [jnp.full((rows, 1), sink_ref[kv * GROUP + g], F32) for g in range(GROUP)], axis=0)
        o_kv = None
        for bi in bis:
            s_c = jnp.where(mask_cache, s_cache[kv][bi] * scale, NEG_INF)
            s_n = jnp.where(mask_new[bi], s_new[kv] * scale, NEG_INF)
            m = jnp.maximum(jnp.maximum(jnp.max(s_c, axis=-1, keepdims=True),
                                        jnp.max(s_n, axis=-1, keepdims=True)), sink_col)
            pv = (_dot(jnp.exp(s_c - m).astype(BF16), v_cache[kv][bi])
                  + _dot(jnp.exp(s_n - m).astype(BF16), v_new[kv]))
            o = pv[:, :HD] / (pv[:, HD:] + jnp.exp(sink_col - m))
            o_kv = o if o_kv is None else jnp.where(q_batch == bi, o, o_kv)
        for g in range(GROUP):
            head = kv * GROUP + g
            hs = slice(head * HD, (head + 1) * HD)
            out_ref[:, hs] = (macc_ref[:, hs] + seg("ma", head, HD).astype(F32)
                              * o_kv[g * rows:(g + 1) * rows]).astype(BF16)

    keep = WINDOW - dec_seq
    for bi in range(nb):
        kout_ref[bi, 0:keep] = kc_ref[bi, dec_seq:WINDOW]
        vout_ref[bi, 0:keep] = vc_ref[bi, dec_seq:WINDOW]
        for t in range(dec_seq):
            r = bi * dec_seq + t
            for kv in range(n_kv):
                kout_ref[bi, keep + t, kv:kv + 1, :] = kvnew_ref[r:r + 1, kv * HD:(kv + 1) * HD]
                vout_ref[bi, keep + t, kv:kv + 1, :] = kvnew_ref[r:r + 1, kvw + kv * HD:kvw + (kv + 1) * HD]


def _mix_sample(proj, kvnew, sinks, tabs, layer, state_all, k_all, v_all, prev_outs, dec_seq, d):
    off = _segment_offsets(d)
    n_ret = d // DK
    n_kv = d // (HD * GROUP)
    kvw = n_kv * HD
    nb = SAMPLE_BATCHES_PER_STEP
    b = state_all.shape[1]
    rows = nb * dec_seq
    smem = pl.BlockSpec(memory_space=pltpu.SMEM)
    const3 = lambda shape: pl.BlockSpec(shape, lambda i: (0, 0, 0))
    state_spec = lambda: pl.BlockSpec((None, nb, n_ret, DK, DK), lambda i: (layer, i, 0, 0, 0))
    win_spec = lambda: pl.BlockSpec((None, nb, WINDOW, n_kv, HD), lambda i: (layer, i, 0, 0, 0))
    in_specs = [smem, smem,
                pl.BlockSpec((rows, off["width"]), lambda i: (i, 0)),
                pl.BlockSpec((rows, 2 * kvw), lambda i: (i, 0)),
                state_spec(), win_spec(), win_spec(),
                const3((n_ret, rows, WINDOW)), const3((n_ret, rows, DK)), const3((n_ret, WINDOW, DK))]
    args = [sinks, tabs["sdec"], proj, kvnew, state_all, k_all, v_all,
            tabs["dmask"], tabs["qdec"], tabs["kdec"]]
    aliases = {}
    if prev_outs is not None:
        for j, arr in enumerate(prev_outs):
            aliases[len(args)] = 1 + j
            in_specs.append(pl.BlockSpec(memory_space=pl.ANY))
            args.append(arr)
    n_alias = len(aliases)
    merged, state, kwin, vwin = pl.pallas_call(
        functools.partial(_mix_sample_kernel, d=d, dec_seq=dec_seq, n_alias=n_alias),
        grid=(b // nb,),
        in_specs=in_specs,
        out_specs=[pl.BlockSpec((rows, d), lambda i: (i, 0)), state_spec(), win_spec(), win_spec()],
        out_shape=[jax.ShapeDtypeStruct((b * dec_seq, d), BF16),
                   jax.ShapeDtypeStruct(state_all.shape, F32),
                   jax.ShapeDtypeStruct(k_all.shape, F32),
                   jax.ShapeDtypeStruct(v_all.shape, F32)],
        scratch_shapes=[pltpu.VMEM((WINDOW, 2 * d + 2 * kvw), BF16),
                        pltpu.VMEM((rows, d), F32)],
        input_output_aliases=aliases,
        compiler_params=_cparams(1),
        name="mix_sample",
    )(*args)
    return merged, (state, kwin, vwin)


def _proj_residual_kernel(a_ref, w_ref, x_ref, g_ref, *rest, with_norm):
    if with_norm:
        lnw_ref, sc_ref, sh_ref, xo_ref, ho_ref = rest
    else:
        (xo_ref,) = rest
    d = xo_ref.shape[1]
    half = DK // 2
    slabs = [slice(j * DK, (j + 1) * DK) for j in range(d // DK)]
    sq = None
    for cs in slabs:
        x = x_ref[:, cs] + g_ref[:, cs] * _dot(a_ref[...], w_ref[:, cs])
        xo_ref[:, cs] = x
        if with_norm:
            x2 = x * x
            part = x2[:, :half] + x2[:, half:]
            sq = part if sq is None else sq + part
    if with_norm:
        r = lax.rsqrt(jnp.sum(sq, axis=-1, keepdims=True) / d + EPS)
        for cs in slabs:
            y = (xo_ref[:, cs] * r) * lnw_ref[:, cs]
            ho_ref[:, cs] = (y * (1.0 + sc_ref[:, cs]) + sh_ref[:, cs]).astype(BF16)


def _proj_residual(a, w, layer, x, gate, norm, tm, name):
    g, r, d = x.shape
    kdim = w.shape[1]
    tok = lambda: pl.BlockSpec((None, tm, d), lambda i, j: (i, j, 0))
    in_specs = [pl.BlockSpec((None, tm, kdim), lambda i, j: (i, j, 0)),
                pl.BlockSpec((None, kdim, d), lambda i, j: (layer, 0, 0), pipeline_mode=pl.Buffered(1)),
                tok(), _mod_spec(gate, tm)]
    args = [a.reshape(g, r, kdim), w, x, gate.arr]
    out_specs = [tok()]
    out_shape = [jax.ShapeDtypeStruct((g, r, d), F32)]
    if norm is not None:
        lnw, sc, sh = norm
        in_specs += [pl.BlockSpec((1, d), lambda i, j: (0, 0)), _mod_spec(sc, tm), _mod_spec(sh, tm)]
        args += [lnw, sc.arr, sh.arr]
        out_specs.append(tok())
        out_shape.append(jax.ShapeDtypeStruct((g, r, d), BF16))
    out = pl.pallas_call(
        functools.partial(_proj_residual_kernel, with_norm=norm is not None),
        grid=(g, r // tm),
        in_specs=in_specs, out_specs=out_specs, out_shape=out_shape,
        compiler_params=_cparams(2),
        name=name,
    )(*args)
    return (out[0], out[1]) if norm is not None else (out[0], None)


def _up_kernel(h_ref, w_ref, *rest, cast_passes):
    o_ref = _side_casts(rest, cast_passes)
    r = jnp.maximum(_dot(h_ref[...], w_ref[...]), 0.0)
    o_ref[...] = (r * r).astype(BF16)


def _up(h, w, layer, tm, casts=()):
    m, d = h.shape
    f = w.shape[2]
    tn = min(2048, f)
    c_in, c_out, c_shape, c_arrays, c_passes = _side_cast_specs(casts, f // tn, m // tm)
    out = pl.pallas_call(
        functools.partial(_up_kernel, cast_passes=c_passes),
        grid=(f // tn, m // tm),
        in_specs=[pl.BlockSpec((tm, d), lambda n, i: (i, 0)),
                  pl.BlockSpec((None, d, tn), lambda n, i: (layer, 0, n))] + c_in,
        out_specs=[pl.BlockSpec((tm, tn), lambda n, i: (i, n))] + c_out,
        out_shape=[jax.ShapeDtypeStruct((m, f), BF16)] + c_shape,
        compiler_params=_cparams(2),
        name="mlp_up",
    )(h, w, *c_arrays)
    return out[0], list(out[1:])


def _rope_tables(pos):
    half = DK // 2
    inv = 1.0 / (ROPE_BASE ** (jnp.arange(half, dtype=F32) / half))
    ang = pos.astype(F32)[:, None] * inv[None, :]
    return jnp.cos(ang), jnp.sin(ang)


def _decay_tables(n_ret, chunk, reps, pad_cols):
    log_g = jnp.log1p(-jnp.exp2(-5.0 - jnp.arange(n_ret, dtype=F32)))
    rows = reps * chunk
    r = jnp.arange(rows)
    cidx = jnp.arange(pad_cols)
    idx = (r % chunk).astype(F32)
    diff = idx[:, None] - (cidx % chunk).astype(F32)[None, :]
    ok = (diff >= 0) & ((r // chunk)[:, None] == (cidx // chunk)[None, :]) & (cidx < rows)[None, :]
    dintra = jnp.where(ok, jnp.exp(jnp.where(ok, diff, 0.0) * log_g[:, None, None]), 0.0)
    qdec = jnp.exp((idx + 1.0) * log_g[:, None])[..., None]
    kidx = (jnp.arange(max(rows, pad_cols)) % chunk).astype(F32)
    kdec = jnp.exp((chunk - 1.0 - kidx) * log_g[:, None])[..., None]
    sdec = jnp.exp(chunk * log_g)
    return dict(dintra=dintra, dmask=dintra,
                qdec=jnp.broadcast_to(qdec, (n_ret, rows, DK)),
                kdec=jnp.broadcast_to(kdec, (n_ret, kidx.shape[0], DK)), sdec=sdec)


def kernel(x_prompt, x_sample, c_prompt, c_sample, state_ret, cache_k_win, cache_v_win, norm1_w,
           norm2_w, w_ada, b_ada, w_in, q_norm_w, k_norm_w, sinks, w_out, w_up, w_down):
    bp, seq, d = x_prompt.shape
    bs, dec_seq, _ = x_sample.shape
    depth = w_in.shape[0]
    n_ret = d // DK
    kvw = d // 4
    n_kv = kvw // HD
    win = cache_k_win.shape[2]
    assert win == WINDOW and seq % WINDOW == 0 and bs % SAMPLE_BATCHES_PER_STEP == 0
    assert dec_seq & (dec_seq - 1) == 0

    ms_rows = bs * dec_seq
    n_c = ms_rows + bp
    c_rows = -(-n_c // 16) * 16
    c_all = jnp.concatenate([jnp.repeat(c_sample, dec_seq, axis=0), c_prompt,
                             jnp.zeros((c_rows - n_c, d), F32)], axis=0)
    mods = _ada(c_all, w_ada, b_ada)

    mods_prompt = mods[:, ms_rows:n_c].reshape(depth * bp * 6, 1, d)

    def group_mods(l):
        return ([_Mod(mods_prompt, l, j, bp) for j in range(6)],
                [_Mod(mods, l, j, 1) for j in range(6)])

    wi = w_in[0:1].astype(BF16)

    cos_p, sin_p = _rope_tables(jnp.arange(seq, dtype=jnp.int32))
    cos_s, sin_s = _rope_tables(PAST_LEN + jnp.arange(dec_seq, dtype=jnp.int32))
    cos_s = jnp.tile(cos_s, (bs, 1))
    sin_s = jnp.tile(sin_s, (bs, 1))
    tabs_p = _decay_tables(n_ret, WINDOW, 1, WINDOW)
    tabs_s = _decay_tables(n_ret, dec_seq, SAMPLE_BATCHES_PER_STEP, WINDOW)

    tiles = _tile_plan(seq, ms_rows)
    tp, ts = tiles["prompt"], tiles["decode"]
    nblk = seq // WINDOW
    xp = x_prompt
    xs = x_sample.reshape(1, ms_rows, d)

    mods_p, mods_s = group_mods(0)
    hp = _prenorm(xp, norm1_w[0:1], mods_p[1], mods_p[0], tp["outproj"])
    hs = _prenorm(xs, norm1_w[0:1], mods_s[1], mods_s[0], ts["outproj"])

    outs = {k: [] for k in ("kp", "vp")}
    prompt_state = None
    sample_state = None
    for l in range(depth):
        qn = q_norm_w[l:l + 1]
        kn = k_norm_w[l:l + 1]
        if l + 1 < depth:
            nxt_p, nxt_s = group_mods(l + 1)
            norm_p = (norm1_w[l + 1:l + 2], nxt_p[1], nxt_p[0])
            norm_s = (norm1_w[l + 1:l + 2], nxt_s[1], nxt_s[0])
        else:
            norm_p = norm_s = None

        hp2 = hp.reshape(bp * seq, d)
        proj, (wu, wo) = _inproj(hp2, wi, 0, cos_p, sin_p, qn, kn, tp["wide"],
                                 casts=[(w_up, l), (w_out, l)])
        kv_tail = _kvproj(hp2, wi, 0, kn, WINDOW, bp, lambda i: i * nblk + nblk - 1)
        merged, prompt_state = _mix_prompt(proj, sinks[l], tabs_p, l, depth, prompt_state, bp, seq, d)
        outs["kp"].append(kv_tail[:, :kvw].reshape(bp, WINDOW, n_kv, HD))
        outs["vp"].append(kv_tail[:, kvw:].reshape(bp, WINDOW, n_kv, HD))
        xp, h2 = _proj_residual(merged, wo, 0, xp, mods_p[2], (norm2_w[l:l + 1], mods_p[4], mods_p[3]),
                                tp["outproj"], "outproj")
        next_w_in = [(w_in, l + 1)] if l + 1 < depth else []
        u, (wd, *wi_next) = _up(h2.reshape(bp * seq, d), wu, 0, tp["wide"],
                                casts=[(w_down, l)] + next_w_in)
        xp, hp = _proj_residual(u, wd, 0, xp, mods_p[5], norm_p, tp["down"], "mlp_down")

        hs2 = hs.reshape(ms_rows, d)
        proj, _ = _inproj(hs2, wi, 0, cos_s, sin_s, qn, kn, ts["wide"])
        kv_new = _kvproj(hs2, wi, 0, kn, ms_rows, 1, lambda i: i)
        merged, sample_state = _mix_sample(proj, kv_new, sinks[l], tabs_s, l, state_ret, cache_k_win,
                                           cache_v_win, sample_state, dec_seq, d)
        xs, h2 = _proj_residual(merged, wo, 0, xs, mods_s[2], (norm2_w[l:l + 1], mods_s[4], mods_s[3]),
                                ts["outproj"], "outproj")
        u, _ = _up(h2.reshape(ms_rows, d), wu, 0, ts["wide"])
        xs, hs = _proj_residual(u, wd, 0, xs, mods_s[5], norm_s, ts["down"], "mlp_down")

        if l + 1 < depth:
            mods_p, mods_s = nxt_p, nxt_s
            wi = wi_next[0]

    return (xp, xs.reshape(bs, dec_seq, d), prompt_state, jnp.stack(outs["kp"]),
            jnp.stack(outs["vp"])) + tuple(sample_state)
```

```python
import functools
import math

import jax
import jax.numpy as jnp
from jax import lax
from jax.experimental import pallas as pl
from jax.experimental.pallas import tpu as pltpu

F32 = jnp.float32
BF16 = jnp.bfloat16

DK = 256
HD = 128
GROUP = 4
WINDOW = 128
ROPE_BASE = 10000.0
EPS = 1e-6
NEG_INF = -1e30
LOG2E = math.log2(math.e)
PAST_LEN = 8192

V7X_VMEM_BYTES = 64 * 1024 * 1024
VMEM_LIMIT = V7X_VMEM_BYTES * 7 // 8
BF16_SUBLANES = 16
SAMPLE_BATCHES_PER_STEP = 4
INPROJ_SLAB_ROWS = 256
PROMPT_BLOCKS_PER_STEP = 2


def _tile_plan(seq, decode_rows):
    return dict(prompt=dict(wide=min(1024, seq), outproj=min(512, seq), down=min(256, seq)),
                decode=dict(wide=decode_rows, outproj=decode_rows, down=min(128, decode_rows)))


def _cparams(n_axes):
    return pltpu.CompilerParams(dimension_semantics=("arbitrary",) * n_axes,
                                vmem_limit_bytes=VMEM_LIMIT)


def _sigmoid(x):
    return 1.0 / (1.0 + jnp.exp(-x))


def _modnorm(x, lnw, sc, sh):
    y = x * lax.rsqrt(jnp.mean(x * x, axis=-1, keepdims=True) + EPS)
    return (y * lnw) * (1.0 + sc) + sh


def _dot(a, b):
    return jnp.dot(a, b, preferred_element_type=F32)


def _dot_nt(a, b):
    return lax.dot_general(a, b, (((1,), (1,)), ((), ())), preferred_element_type=F32)


def _dot_tn(a, b):
    return lax.dot_general(a, b, (((0,), (0,)), ((), ())), preferred_element_type=F32)


def _ada_kernel(c_ref, w_ref, b_ref, o_ref):
    c = c_ref[...]
    s = (c * _sigmoid(c)).astype(BF16)
    o_ref[...] = _dot(s, w_ref[...].astype(BF16)) + b_ref[...]


def _ada(c_all, w_ada, b_ada):
    depth, d, n6 = w_ada.shape
    rows = c_all.shape[0]
    tn = min(1024, n6)
    return pl.pallas_call(
        _ada_kernel,
        grid=(depth, n6 // tn),
        in_specs=[pl.BlockSpec((rows, d), lambda l, n: (0, 0)),
                  pl.BlockSpec((None, d, tn), lambda l, n: (l, 0, n)),
                  pl.BlockSpec((None, 1, tn), lambda l, n: (l, 0, n))],
        out_specs=pl.BlockSpec((None, rows, tn), lambda l, n: (l, 0, n)),
        out_shape=jax.ShapeDtypeStruct((depth, rows, n6), F32),
        compiler_params=_cparams(2),
        name="ada",
    )(c_all, w_ada, b_ada.reshape(depth, 1, n6))


def _prenorm_kernel(x_ref, lnw_ref, sc_ref, sh_ref, h_ref):
    h_ref[...] = _modnorm(x_ref[...], lnw_ref[...], sc_ref[...], sh_ref[...]).astype(BF16)


class _Mod:
    def __init__(self, arr, layer, part, groups):
        self.arr, self.layer, self.part, self.groups = arr, layer, part, groups

    def spec(self, tm):
        l, j, arr = self.layer, self.part, self.arr
        if arr.shape[1] == 1:
            base = l * self.groups * 6 + j
            return pl.BlockSpec((None, 1, arr.shape[2]), lambda g, r: (base + g * 6, 0, 0))
        return pl.BlockSpec((None, tm, arr.shape[2] // 6), lambda g, r: (l, r, j))


def _mod_spec(mod, tm):
    return mod.spec(tm)


def _prenorm(x, lnw, sc, sh, tm):
    g, r, d = x.shape
    return pl.pallas_call(
        _prenorm_kernel,
        grid=(g, r // tm),
        in_specs=[pl.BlockSpec((None, tm, d), lambda i, j: (i, j, 0)),
                  pl.BlockSpec((1, d), lambda i, j: (0, 0)),
                  _mod_spec(sc, tm), _mod_spec(sh, tm)],
        out_specs=pl.BlockSpec((None, tm, d), lambda i, j: (i, j, 0)),
        out_shape=jax.ShapeDtypeStruct((g, r, d), BF16),
        compiler_params=_cparams(2),
        name="prenorm",
    )(x, lnw, sc.arr, sh.arr)


def _rms_heads(a, w):
    outs = []
    for j in range(a.shape[1] // HD):
        t = a[:, j * HD:(j + 1) * HD]
        outs.append(t * lax.rsqrt(jnp.mean(t * t, axis=-1, keepdims=True) + EPS) * w)
    return outs


def _side_cast_specs(casts, n_n, n_i):
    in_specs, out_specs, out_shape, arrays, passes = [], [], [], [], []
    for w, layer in casts:
        _, rows, cols = w.shape
        p = max(q for q in range(1, n_n + 1)
                if rows % (q * n_i) == 0 and (rows // (q * n_i)) % BF16_SUBLANES == 0)
        chunk = rows // (p * n_i)
        pick = lambda n, i, p=p: jnp.where(n < p, n * n_i + i, p * n_i - 1)
        in_specs.append(pl.BlockSpec((None, chunk, cols), lambda n, i, layer=layer, pick=pick: (layer, pick(n, i), 0)))
        out_specs.append(pl.BlockSpec((None, chunk, cols), lambda n, i, pick=pick: (0, pick(n, i), 0)))
        out_shape.append(jax.ShapeDtypeStruct((1, rows, cols), BF16))
        arrays.append(w)
        passes.append(None if p == n_n else p)
    return in_specs, out_specs, out_shape, arrays, tuple(passes)


def _side_casts(rest, passes):
    n_casts = len(passes)
    for src, dst, p in zip(rest[:n_casts], rest[n_casts + 1:], passes):
        if p is None:
            dst[...] = src[...].astype(BF16)
        else:
            @pl.when(pl.program_id(0) < p)
            def _cast(src=src, dst=dst):
                dst[...] = src[...].astype(BF16)

    return rest[n_casts]


def _inproj_kernel(h_ref, w_ref, cos_ref, sin_ref, qn_ref, kn_ref, *rest, tn, cast_passes):
    o_ref = _side_casts(rest, cast_passes)
    n = pl.program_id(0)
    half = DK // 2
    tm = h_ref.shape[0]
    tr = min(tm, INPROJ_SLAB_ROWS)
    slabs = [(slice(r, r + tr), j) for j in range(tn // DK) for r in range(0, tm, tr)]
    full_slabs = [(slice(0, tm), j) for j in range(tn // DK)]

    def slab(rows, j):
        return _dot(h_ref[rows, :], w_ref[:, j * DK:(j + 1) * DK])

    @pl.when(n < 4)
    def _rotary():
        scale = jnp.where(n < 2, 1.0, DK ** -0.5).astype(F32)
        for rows, j in full_slabs:
            a = slab(rows, j)
            cos = cos_ref[rows, :]
            sin = sin_ref[rows, :]
            t1 = a[:, :half]
            t2 = a[:, half:]
            o_ref[rows, j * DK:j * DK + half] = ((t1 * cos - t2 * sin) * scale).astype(BF16)
            o_ref[rows, j * DK + half:(j + 1) * DK] = ((t1 * sin + t2 * cos) * scale).astype(BF16)

    def _rms_slabs(w, n_chunks):
        for rows, j in slabs:
            if j < n_chunks:
                for i, y in enumerate(_rms_heads(slab(rows, j), w)):
                    o_ref[rows, j * DK + i * HD:j * DK + (i + 1) * HD] = y.astype(BF16)

    @pl.when((n == 8) | (n == 9))
    def _qnorm():
        _rms_slabs(qn_ref[...], tn // DK)

    @pl.when(n == 10)
    def _kv():
        if tn // 2 >= DK:
            _rms_slabs(kn_ref[...], tn // (2 * DK))
        else:
            a = _dot(h_ref[...], w_ref[:, :tn // 2])
            for i, y in enumerate(_rms_heads(a, kn_ref[...])):
                o_ref[:, i * HD:(i + 1) * HD] = y.astype(BF16)
        o_ref[:, tn // 2:] = _dot(h_ref[...], w_ref[:, tn // 2:]).astype(BF16)

    @pl.when((n == 4) | (n == 5))
    def _plain():
        o_ref[...] = _dot(h_ref[...], w_ref[...]).astype(BF16)

    @pl.when((n == 6) | (n == 7))
    def _silu_gate():
        for rows, j in slabs:
            a = slab(rows, j)
            o_ref[rows, j * DK:(j + 1) * DK] = (a * _sigmoid(a)).astype(BF16)

    @pl.when(n > 10)
    def _sigmoid_gate():
        for rows, j in slabs:
            o_ref[rows, j * DK:(j + 1) * DK] = _sigmoid(slab(rows, j)).astype(BF16)


def _inproj(h, w, layer, cos, sin, qn, kn, tm, casts=()):
    m, d = h.shape
    width = w.shape[2]
    tn = d // 2
    nblk = cos.shape[0] // tm
    rope_block = lambda n, i: (jnp.where(n < 4, i % nblk, 0), 0)
    c_in, c_out, c_shape, c_arrays, c_passes = _side_cast_specs(casts, width // tn, m // tm)
    out = pl.pallas_call(
        functools.partial(_inproj_kernel, tn=tn, cast_passes=c_passes),
        grid=(width // tn, m // tm),
        in_specs=[pl.BlockSpec((tm, d), lambda n, i: (i, 0)),
                  pl.BlockSpec((None, d, tn), lambda n, i: (layer, 0, n)),
                  pl.BlockSpec((tm, HD), rope_block), pl.BlockSpec((tm, HD), rope_block),
                  pl.BlockSpec((1, HD), lambda n, i: (0, 0)),
                  pl.BlockSpec((1, HD), lambda n, i: (0, 0))] + c_in,
        out_specs=[pl.BlockSpec((tm, tn), lambda n, i: (i, n))] + c_out,
        out_shape=[jax.ShapeDtypeStruct((m, width), BF16)] + c_shape,
        compiler_params=_cparams(2),
        name="inproj",
    )(h, w, cos, sin, qn, kn, *c_arrays)
    return out[0], list(out[1:])


def _kvproj_kernel(h_ref, w_ref, kn_ref, o_ref):
    kvw = o_ref.shape[1] // 2
    a = _dot(h_ref[...], w_ref[...])
    for i, y in enumerate(_rms_heads(a[:, :kvw], kn_ref[...])):
        o_ref[:, i * HD:(i + 1) * HD] = y
    o_ref[:, kvw:] = a[:, kvw:]


def _kvproj(h, w, layer, kn, tm, n_blocks, row_block):
    d = h.shape[1]
    kvw = d // 4
    col_block = (5 * d) // (2 * kvw)
    return pl.pallas_call(
        _kvproj_kernel,
        grid=(n_blocks,),
        in_specs=[pl.BlockSpec((tm, d), lambda i: (row_block(i), 0)),
                  pl.BlockSpec((None, d, 2 * kvw), lambda i: (layer, 0, col_block)),
                  pl.BlockSpec((1, HD), lambda i: (0, 0))],
        out_specs=pl.BlockSpec((tm, 2 * kvw), lambda i: (i, 0)),
        out_shape=jax.ShapeDtypeStruct((n_blocks * tm, 2 * kvw), F32),
        compiler_params=_cparams(1),
        name="kvproj",
    )(h, w, kn)


def _segment_offsets(d):
    kvw = d // 4
    return dict(q=0, k=d, v=2 * d, g=3 * d, qa=4 * d, ka=5 * d, va=5 * d + kvw,
                mr=5 * d + 2 * kvw, ma=6 * d + 2 * kvw, width=7 * d + 2 * kvw)


def _retention_gate(o, g_act, mr_act):
    on = o * lax.rsqrt(jnp.mean(o * o, axis=-1, keepdims=True) + EPS)
    return mr_act.astype(F32) * (on * g_act.astype(F32))


def _mix_prompt_kernel(sink_ref, sdec_ref, blk_ref, prev_ref, dintra_ref, qdec_ref, kdec_ref,
                       *rest, d):
    out_ref, s_ref, macc_ref = rest[-3:]
    c = pl.program_id(1)
    off = _segment_offsets(d)
    n_ret = d // DK
    n_kv = d // (HD * GROUP)
    kvw = n_kv * HD
    blk = WINDOW
    n_sub = blk_ref.shape[0] // blk

    @pl.when(c == 0)
    def _init():
        s_ref[...] = jnp.zeros_like(s_ref)

    qi = lax.broadcasted_iota(jnp.int32, (blk, 2 * blk), 0)
    kj = lax.broadcasted_iota(jnp.int32, (blk, 2 * blk), 1)
    diff = kj - qi
    band = (diff >= 0) & (diff <= WINDOW)
    scale = HD ** -0.5
    kvs = range(n_kv)
    heads = range(n_ret)
    ones = jnp.ones((2 * blk, HD), BF16)

    for sub in range(n_sub):
        rows = slice(sub * blk, (sub + 1) * blk)

        def seg(name, h, width=DK, rows=rows):
            return blk_ref[rows, off[name] + h * width:off[name] + (h + 1) * width]

        if sub == 0:
            prev_k = lambda kv: prev_ref[:, kv * HD:(kv + 1) * HD]
            prev_v = lambda kv: prev_ref[:, kvw + kv * HD:kvw + (kv + 1) * HD]
            mask = band & (kj >= jnp.where(c == 0, blk, 0))
        else:
            before = slice((sub - 1) * blk, sub * blk)
            prev_k = lambda kv, r=before: blk_ref[r, off["ka"] + kv * HD:off["ka"] + (kv + 1) * HD]
            prev_v = lambda kv, r=before: blk_ref[r, off["va"] + kv * HD:off["va"] + (kv + 1) * HD]
            mask = band

        scores, values = [], []
        for kv in kvs:
            qs = jnp.concatenate([seg("qa", kv * GROUP + g, HD) for g in range(GROUP)], axis=0)
            k2 = jnp.concatenate([prev_k(kv), seg("ka", kv, HD)], axis=0)
            v2 = jnp.concatenate([prev_v(kv), seg("va", kv, HD)], axis=0)
            scores.append(_dot_nt(qs, k2))
            values.append(jnp.concatenate([v2, ones], axis=1))
        att = [_dot_nt(seg("q", h), seg("k", h)) for h in heads]
        inter = [_dot(seg("q", h), s_ref[h].astype(BF16)) for h in heads]
        upd = [_dot_tn((seg("k", h).astype(F32) * kdec_ref[h]).astype(BF16), seg("v", h)) for h in heads]
        for h in heads:
            s_ref[h] = s_ref[h] * sdec_ref[h] + upd[h]
        att = [(att[h] * dintra_ref[h]).astype(BF16) for h in heads]
        o_ret = [_dot(att[h], seg("v", h)) + inter[h] * qdec_ref[h] for h in heads]
        for h in heads:
            macc_ref[rows, h * DK:(h + 1) * DK] = _retention_gate(o_ret[h], seg("g", h), seg("mr", h))

        probs, sink_terms = [], []
        for kv in kvs:
            p_rows = []
            for g in range(GROUP):
                sink = sink_ref[kv * GROUP + g] * (1.0 / scale)
                s = jnp.where(mask, scores[kv][g * blk:(g + 1) * blk], NEG_INF)
                m = jnp.maximum(jnp.max(s, axis=-1, keepdims=True), sink)
                p_rows.append(jnp.exp2((s - m) * (scale * LOG2E)).astype(BF16))
                sink_terms.append(jnp.exp2((sink - m) * (scale * LOG2E)))
            probs.append(jnp.concatenate(p_rows, axis=0))
        pv = [_dot(probs[kv], values[kv]) for kv in kvs]
        for kv in kvs:
            for g in range(GROUP):
                head = kv * GROUP + g
                hs = slice(head * HD, (head + 1) * HD)
                o = pv[kv][g * blk:(g + 1) * blk, :HD]
                denom = pv[kv][g * blk:(g + 1) * blk, HD:] + sink_terms[head]
                out_ref[rows, hs] = (macc_ref[rows, hs]
                                     + seg("ma", head, HD).astype(F32) * (o / denom)).astype(BF16)


def _mix_prompt(proj, sinks, tabs, layer, depth, prev_state, b, seq, d):
    off = _segment_offsets(d)
    n_ret = d // DK
    kvw = d // 4
    nblk = seq // WINDOW
    proj3 = proj.reshape(b, seq, off["width"])
    kv_col_block = off["ka"] // (2 * kvw)
    smem = pl.BlockSpec(memory_space=pltpu.SMEM)
    const3 = lambda shape: pl.BlockSpec(shape, lambda i, c: (0, 0, 0))
    n_sub = PROMPT_BLOCKS_PER_STEP if nblk % PROMPT_BLOCKS_PER_STEP == 0 else 1
    rows = n_sub * WINDOW
    in_specs = [smem, smem,
                pl.BlockSpec((None, rows, off["width"]), lambda i, c: (i, c, 0)),
                pl.BlockSpec((None, WINDOW, 2 * kvw),
                             lambda i, c: (i, jnp.maximum(c * n_sub - 1, 0), kv_col_block)),
                const3((n_ret, WINDOW, WINDOW)), const3((n_ret, WINDOW, DK)),
                const3((n_ret, WINDOW, DK))]
    args = [sinks, tabs["sdec"], proj3, proj3, tabs["dintra"], tabs["qdec"], tabs["kdec"]]
    aliases = {}
    if prev_state is not None:
        aliases[len(args)] = 1
        in_specs.append(pl.BlockSpec(memory_space=pl.ANY))
        args.append(prev_state)
    merged, state = pl.pallas_call(
        functools.partial(_mix_prompt_kernel, d=d),
        grid=(b, nblk // n_sub),
        in_specs=in_specs,
        out_specs=[pl.BlockSpec((None, rows, d), lambda i, c: (i, c, 0)),
                   pl.BlockSpec((None, None, n_ret, DK, DK), lambda i, c: (layer, i, 0, 0, 0))],
        out_shape=[jax.ShapeDtypeStruct((b, seq, d), BF16),
                   jax.ShapeDtypeStruct((depth, b, n_ret, DK, DK), F32)],
        scratch_shapes=[pltpu.VMEM((rows, d), F32)],
        input_output_aliases=aliases,
        compiler_params=_cparams(2),
        name="mix_prompt",
    )(*args)
    return merged.reshape(b * seq, d), state


def _mix_sample_kernel(sink_ref, sdec_ref, blk_ref, kvnew_ref, s0_ref, kc_ref, vc_ref, dmask_ref,
                       qdec_ref, kdec_ref, *rest, d, dec_seq, n_alias):
    out_ref, s_out_ref, kout_ref, vout_ref, pad_ref, macc_ref = rest[n_alias:]
    off = _segment_offsets(d)
    n_ret = d // DK
    n_kv = d // (HD * GROUP)
    kvw = n_kv * HD
    nb = SAMPLE_BATCHES_PER_STEP
    rows = nb * dec_seq
    pad = pad_ref.shape[0]
    p_k, p_v, p_ka, p_va = 0, d, 2 * d, 2 * d + kvw

    @pl.when(pl.program_id(0) == 0)
    def _init():
        pad_ref[...] = jnp.zeros_like(pad_ref)

    pad_ref[0:rows, p_k:p_k + 2 * d] = blk_ref[:, off["k"]:off["k"] + 2 * d]
    pad_ref[0:rows, p_ka:p_ka + 2 * kvw] = blk_ref[:, off["ka"]:off["ka"] + 2 * kvw]

    tok_bits = dec_seq.bit_length() - 1
    row_batch = lax.broadcasted_iota(jnp.int32, (rows, 1), 0) >> tok_bits
    pad_batch = lax.broadcasted_iota(jnp.int32, (pad, 1), 0) >> tok_bits

    def seg(name, h, width=DK):
        return blk_ref[:, off[name] + h * width:off[name] + (h + 1) * width]

    def padded(col0, h, width=DK):
        return pad_ref[:, col0 + h * width:col0 + (h + 1) * width]

    heads = range(n_ret)
    bis = range(nb)
    att = [_dot_nt(seg("q", h), padded(p_k, h)) for h in heads]
    inter = [[_dot(seg("q", h), s0_ref[bi, h].astype(BF16)) for bi in bis] for h in heads]
    kd = [(padded(p_k, h).astype(F32) * kdec_ref[h]).astype(BF16) for h in heads]
    v_b = [[jnp.where(pad_batch == bi, padded(p_v, h), jnp.zeros((pad, DK), BF16)) for bi in bis]
           for h in heads]
    upd = [[_dot_tn(kd[h], v_b[h][bi]) for bi in bis] for h in heads]
    for h in heads:
        for bi in bis:
            s_out_ref[bi, h] = s0_ref[bi, h] * sdec_ref[h] + upd[h][bi]
    att = [(att[h] * dmask_ref[h]).astype(BF16) for h in heads]
    for h in heads:
        own = inter[h][0]
        for bi in bis[1:]:
            own = jnp.where(row_batch == bi, inter[h][bi], own)
        o = _dot(att[h], padded(p_v, h)) + own * qdec_ref[h]
        macc_ref[:, h * DK:(h + 1) * DK] = _retention_gate(o, seg("g", h), seg("mr", h))

    qrows = GROUP * rows
    q_tok = lax.broadcasted_iota(jnp.int32, (qrows, 1), 0) & (dec_seq - 1)
    q_batch = (lax.broadcasted_iota(jnp.int32, (qrows, 1), 0) & (rows - 1)) >> tok_bits
    cache_j = lax.broadcasted_iota(jnp.int32, (qrows, WINDOW), 1)
    new_j = lax.broadcasted_iota(jnp.int32, (qrows, pad), 1)
    mask_cache = cache_j >= q_tok
    mask_new = [((new_j >> tok_bits) == bi) & ((new_j & (dec_seq - 1)) <= q_tok) for bi in bis]
    scale = HD ** -0.5
    kvs = range(n_kv)
    ones = jnp.ones((WINDOW, HD), BF16)
    qs = [jnp.concatenate([seg("qa", kv * GROUP + g, HD) for g in range(GROUP)], axis=0) for kv in kvs]
    s_new = [_dot_nt(qs[kv], padded(p_ka, kv, HD)) for kv in kvs]
    s_cache = [[_dot_nt(qs[kv], kc_ref[bi, :, kv, :].astype(BF16)) for bi in bis] for kv in kvs]
    v_new = [jnp.concatenate([padded(p_va, kv, HD), ones], axis=1) for kv in kvs]
    v_cache = [[jnp.concatenate([vc_ref[bi, :, kv, :].astype(BF16), ones], axis=1) for bi in bis]
               for kv in kvs]
    for kv in kvs:
        sink_col = jnp.concatenate(
            [jnp.full((rows, 1), sink_ref[kv * GROUP + g], F32) for g in range(GROUP)], axis=0)
        o_kv = None
        for bi in bis:
            s_c = jnp.where(mask_cache, s_cache[kv][bi] * scale, NEG_INF)
            s_n = jnp.where(mask_new[bi], s_new[kv] * scale, NEG_INF)
            m = jnp.maximum(jnp.maximum(jnp.max(s_c, axis=-1, keepdims=True),
                                        jnp.max(s_n, axis=-1, keepdims=True)), sink_col)
            pv = (_dot(jnp.exp(s_c - m).astype(BF16), v_cache[kv][bi])
                  + _dot(jnp.exp(s_n - m).astype(BF16), v_new[kv]))
            o = pv[:, :HD] / (pv[:, HD:] + jnp.exp(sink_col - m))
            o_kv = o if o_kv is None else jnp.where(q_batch == bi, o, o_kv)
        for g in range(GROUP):
            head = kv * GROUP + g
            hs = slice(head * HD, (head + 1) * HD)
            out_ref[:, hs] = (macc_ref[:, hs] + seg("ma", head, HD).astype(F32)
                              * o_kv[g * rows:(g + 1) * rows]).astype(BF16)

    keep = WINDOW - dec_seq
    for bi in range(nb):
        kout_ref[bi, 0:keep] = kc_ref[bi, dec_seq:WINDOW]
        vout_ref[bi, 0:keep] = vc_ref[bi, dec_seq:WINDOW]
        for t in range(dec_seq):
            r = bi * dec_seq + t
            for kv in range(n_kv):
                kout_ref[bi, keep + t, kv:kv + 1, :] = kvnew_ref[r:r + 1, kv * HD:(kv + 1) * HD]
                vout_ref[bi, keep + t, kv:kv + 1, :] = kvnew_ref[r:r + 1, kvw + kv * HD:kvw + (kv + 1) * HD]


def _mix_sample(proj, kvnew, sinks, tabs, layer, state_all, k_all, v_all, prev_outs, dec_seq, d):
    off = _segment_offsets(d)
    n_ret = d // DK
    n_kv = d // (HD * GROUP)
    kvw = n_kv * HD
    nb = SAMPLE_BATCHES_PER_STEP
    b = state_all.shape[1]
    rows = nb * dec_seq
    smem = pl.BlockSpec(memory_space=pltpu.SMEM)
    const3 = lambda shape: pl.BlockSpec(shape, lambda i: (0, 0, 0))
    state_spec = lambda: pl.BlockSpec((None, nb, n_ret, DK, DK), lambda i: (layer, i, 0, 0, 0))
    win_spec = lambda: pl.BlockSpec((None, nb, WINDOW, n_kv, HD), lambda i: (layer, i, 0, 0, 0))
    in_specs = [smem, smem,
                pl.BlockSpec((rows, off["width"]), lambda i: (i, 0)),
                pl.BlockSpec((rows, 2 * kvw), lambda i: (i, 0)),
                state_spec(), win_spec(), win_spec(),
                const3((n_ret, rows, WINDOW)), const3((n_ret, rows, DK)), const3((n_ret, WINDOW, DK))]
    args = [sinks, tabs["sdec"], proj, kvnew, state_all, k_all, v_all,
            tabs["dmask"], tabs["qdec"], tabs["kdec"]]
    aliases = {}
    if prev_outs is not None:
        for j, arr in enumerate(prev_outs):
            aliases[len(args)] = 1 + j
            in_specs.append(pl.BlockSpec(memory_space=pl.ANY))
            args.append(arr)
    n_alias = len(aliases)
    merged, state, kwin, vwin = pl.pallas_call(
        functools.partial(_mix_sample_kernel, d=d, dec_seq=dec_seq, n_alias=n_alias),
        grid=(b // nb,),
        in_specs=in_specs,
        out_specs=[pl.BlockSpec((rows, d), lambda i: (i, 0)), state_spec(), win_spec(), win_spec()],
        out_shape=[jax.ShapeDtypeStruct((b * dec_seq, d), BF16),
                   jax.ShapeDtypeStruct(state_all.shape, F32),
                   jax.ShapeDtypeStruct(k_all.shape, F32),
                   jax.ShapeDtypeStruct(v_all.shape, F32)],
        scratch_shapes=[pltpu.VMEM((WINDOW, 2 * d + 2 * kvw), BF16),
                        pltpu.VMEM((rows, d), F32)],
        input_output_aliases=aliases,
        compiler_params=_cparams(1),
        name="mix_sample",
    )(*args)
    return merged, (state, kwin, vwin)


def _proj_residual_kernel(a_ref, w_ref, x_ref, g_ref, *rest, with_norm):
    if with_norm:
        lnw_ref, sc_ref, sh_ref, xo_ref, ho_ref = rest
    else:
        (xo_ref,) = rest
    d = xo_ref.shape[1]
    half = DK // 2
    slabs = [slice(j * DK, (j + 1) * DK) for j in range(d // DK)]
    sq = None
    for cs in slabs:
        x = x_ref[:, cs] + g_ref[:, cs] * _dot(a_ref[...], w_ref[:, cs])
        xo_ref[:, cs] = x
        if with_norm:
            x2 = x * x
            part = x2[:, :half] + x2[:, half:]
            sq = part if sq is None else sq + part
    if with_norm:
        r = lax.rsqrt(jnp.sum(sq, axis=-1, keepdims=True) / d + EPS)
        for cs in slabs:
            y = (xo_ref[:, cs] * r) * lnw_ref[:, cs]
            ho_ref[:, cs] = (y * (1.0 + sc_ref[:, cs]) + sh_ref[:, cs]).astype(BF16)


def _proj_residual(a, w, layer, x, gate, norm, tm, name):
    g, r, d = x.shape
    kdim = w.shape[1]
    tok = lambda: pl.BlockSpec((None, tm, d), lambda i, j: (i, j, 0))
    in_specs = [pl.BlockSpec((None, tm, kdim), lambda i, j: (i, j, 0)),
                pl.BlockSpec((None, kdim, d), lambda i, j: (layer, 0, 0), pipeline_mode=pl.Buffered(1)),
                tok(), _mod_spec(gate, tm)]
    args = [a.reshape(g, r, kdim), w, x, gate.arr]
    out_specs = [tok()]
    out_shape = [jax.ShapeDtypeStruct((g, r, d), F32)]
    if norm is not None:
        lnw, sc, sh = norm
        in_specs += [pl.BlockSpec((1, d), lambda i, j: (0, 0)), _mod_spec(sc, tm), _mod_spec(sh, tm)]
        args += [lnw, sc.arr, sh.arr]
        out_specs.append(tok())
        out_shape.append(jax.ShapeDtypeStruct((g, r, d), BF16))
    out = pl.pallas_call(
        functools.partial(_proj_residual_kernel, with_norm=norm is not None),
        grid=(g, r // tm),
        in_specs=in_specs, out_specs=out_specs, out_shape=out_shape,
        compiler_params=_cparams(2),
        name=name,
    )(*args)
    return (out[0], out[1]) if norm is not None else (out[0], None)


def _up_kernel(h_ref, w_ref, *rest, cast_passes):
    o_ref = _side_casts(rest, cast_passes)
    r = jnp.maximum(_dot(h_ref[...], w_ref[...]), 0.0)
    o_ref[...] = (r * r).astype(BF16)


def _up(h, w, layer, tm, casts=()):
    m, d = h.shape
    f = w.shape[2]
    tn = min(2048, f)
    c_in, c_out, c_shape, c_arrays, c_passes = _side_cast_specs(casts, f // tn, m // tm)
    out = pl.pallas_call(
        functools.partial(_up_kernel, cast_passes=c_passes),
        grid=(f // tn, m // tm),
        in_specs=[pl.BlockSpec((tm, d), lambda n, i: (i, 0)),
                  pl.BlockSpec((None, d, tn), lambda n, i: (layer, 0, n))] + c_in,
        out_specs=[pl.BlockSpec((tm, tn), lambda n, i: (i, n))] + c_out,
        out_shape=[jax.ShapeDtypeStruct((m, f), BF16)] + c_shape,
        compiler_params=_cparams(2),
        name="mlp_up",
    )(h, w, *c_arrays)
    return out[0], list(out[1:])


def _rope_tables(pos):
    half = DK // 2
    inv = 1.0 / (ROPE_BASE ** (jnp.arange(half, dtype=F32) / half))
    ang = pos.astype(F32)[:, None] * inv[None, :]
    return jnp.cos(ang), jnp.sin(ang)


def _decay_tables(n_ret, chunk, reps, pad_cols):
    log_g = jnp.log1p(-jnp.exp2(-5.0 - jnp.arange(n_ret, dtype=F32)))
    rows = reps * chunk
    r = jnp.arange(rows)
    cidx = jnp.arange(pad_cols)
    idx = (r % chunk).astype(F32)
    diff = idx[:, None] - (cidx % chunk).astype(F32)[None, :]
    ok = (diff >= 0) & ((r // chunk)[:, None] == (cidx // chunk)[None, :]) & (cidx < rows)[None, :]
    dintra = jnp.where(ok, jnp.exp(jnp.where(ok, diff, 0.0) * log_g[:, None, None]), 0.0)
    qdec = jnp.exp((idx + 1.0) * log_g[:, None])[..., None]
    kidx = (jnp.arange(max(rows, pad_cols)) % chunk).astype(F32)
    kdec = jnp.exp((chunk - 1.0 - kidx) * log_g[:, None])[..., None]
    sdec = jnp.exp(chunk * log_g)
    return dict(dintra=dintra, dmask=dintra,
                qdec=jnp.broadcast_to(qdec, (n_ret, rows, DK)),
                kdec=jnp.broadcast_to(kdec, (n_ret, kidx.shape[0], DK)), sdec=sdec)


def kernel(x_prompt, x_sample, c_prompt, c_sample, state_ret, cache_k_win, cache_v_win, norm1_w,
           norm2_w, w_ada, b_ada, w_in, q_norm_w, k_norm_w, sinks, w_out, w_up, w_down):
    bp, seq, d = x_prompt.shape
    bs, dec_seq, _ = x_sample.shape
    depth = w_in.shape[0]
    n_ret = d // DK
    kvw = d // 4
    n_kv = kvw // HD
    win = cache_k_win.shape[2]
    assert win == WINDOW and seq % WINDOW == 0 and bs % SAMPLE_BATCHES_PER_STEP == 0
    assert dec_seq & (dec_seq - 1) == 0

    ms_rows = bs * dec_seq
    n_c = ms_rows + bp
    c_rows = -(-n_c // 16) * 16
    c_all = jnp.concatenate([jnp.repeat(c_sample, dec_seq, axis=0), c_prompt,
                             jnp.zeros((c_rows - n_c, d), F32)], axis=0)
    mods = _ada(c_all, w_ada, b_ada)

    mods_prompt = mods[:, ms_rows:n_c].reshape(depth * bp * 6, 1, d)

    def group_mods(l):
        return ([_Mod(mods_prompt, l, j, bp) for j in range(6)],
                [_Mod(mods, l, j, 1) for j in range(6)])

    wi = w_in[0:1].astype(BF16)

    cos_p, sin_p = _rope_tables(jnp.arange(seq, dtype=jnp.int32))
    cos_s, sin_s = _rope_tables(PAST_LEN + jnp.arange(dec_seq, dtype=jnp.int32))
    cos_s = jnp.tile(cos_s, (bs, 1))
    sin_s = jnp.tile(sin_s, (bs, 1))
    tabs_p = _decay_tables(n_ret, WINDOW, 1, WINDOW)
    tabs_s = _decay_tables(n_ret, dec_seq, SAMPLE_BATCHES_PER_STEP, WINDOW)

    tiles = _tile_plan(seq, ms_rows)
    tp, ts = tiles["prompt"], tiles["decode"]
    nblk = seq // WINDOW
    xp = x_prompt
    xs = x_sample.reshape(1, ms_rows, d)

    mods_p, mods_s = group_mods(0)
    hp = _prenorm(xp, norm1_w[0:1], mods_p[1], mods_p[0], tp["outproj"])
    hs = _prenorm(xs, norm1_w[0:1], mods_s[1], mods_s[0], ts["outproj"])

    outs = {k: [] for k in ("kp", "vp")}
    prompt_state = None
    sample_state = None
    for l in range(depth):
        qn = q_norm_w[l:l + 1]
        kn = k_norm_w[l:l + 1]
        if l + 1 < depth:
            nxt_p, nxt_s = group_mods(l + 1)
            norm_p = (norm1_w[l + 1:l + 2], nxt_p[1], nxt_p[0])
            norm_s = (norm1_w[l + 1:l + 2], nxt_s[1], nxt_s[0])
        else:
            norm_p = norm_s = None

        hp2 = hp.reshape(bp * seq, d)
        proj, (wu, wo) = _inproj(hp2, wi, 0, cos_p, sin_p, qn, kn, tp["wide"],
                                 casts=[(w_up, l), (w_out, l)])
        kv_tail = _kvproj(hp2, wi, 0, kn, WINDOW, bp, lambda i: i * nblk + nblk - 1)
        merged, prompt_state = _mix_prompt(proj, sinks[l], tabs_p, l, depth, prompt_state, bp, seq, d)
        outs["kp"].append(kv_tail[:, :kvw].reshape(bp, WINDOW, n_kv, HD))
        outs["vp"].append(kv_tail[:, kvw:].reshape(bp, WINDOW, n_kv, HD))
        xp, h2 = _proj_residual(merged, wo, 0, xp, mods_p[2], (norm2_w[l:l + 1], mods_p[4], mods_p[3]),
                                tp["outproj"], "outproj")
        next_w_in = [(w_in, l + 1)] if l + 1 < depth else []
        u, (wd, *wi_next) = _up(h2.reshape(bp * seq, d), wu, 0, tp["wide"],
                                casts=[(w_down, l)] + next_w_in)
        xp, hp = _proj_residual(u, wd, 0, xp, mods_p[5], norm_p, tp["down"], "mlp_down")

        hs2 = hs.reshape(ms_rows, d)
        proj, _ = _inproj(hs2, wi, 0, cos_s, sin_s, qn, kn, ts["wide"])
        kv_new = _kvproj(hs2, wi, 0, kn, ms_rows, 1, lambda i: i)
        merged, sample_state = _mix_sample(proj, kv_new, sinks[l], tabs_s, l, state_ret, cache_k_win,
                                           cache_v_win, sample_state, dec_seq, d)
        xs, h2 = _proj_residual(merged, wo, 0, xs, mods_s[2], (norm2_w[l:l + 1], mods_s[4], mods_s[3]),
                                ts["outproj"], "outproj")
        u, _ = _up(h2.reshape(ms_rows, d), wu, 0, ts["wide"])
        xs, hs = _proj_residual(u, wd, 0, xs, mods_s[5], norm_s, ts["down"], "mlp_down")

        if l + 1 < depth:
            mods_p, mods_s = nxt_p, nxt_s
            wi = wi_next[0]

    return (xp, xs.reshape(bs, dec_seq, d), prompt_state, jnp.stack(outs["kp"]),
            jnp.stack(outs["vp"])) + tuple(sample_state)
```

```python
import functools
import math

import jax
import jax.numpy as jnp
from jax import lax
from jax.experimental import pallas as pl
from jax.experimental.pallas import tpu as pltpu

F32 = jnp.float32
BF16 = jnp.bfloat16

DK = 256
HD = 128
GROUP = 4
WINDOW = 128
ROPE_BASE = 10000.0
EPS = 1e-6
NEG_INF = -1e30
LOG2E = math.log2(math.e)
PAST_LEN = 8192

V7X_VMEM_BYTES = 64 * 1024 * 1024
VMEM_LIMIT = V7X_VMEM_BYTES * 7 // 8
BF16_SUBLANES = 16
SAMPLE_BATCHES_PER_STEP = 4
INPROJ_SLAB_ROWS = 256
PROMPT_BLOCKS_PER_STEP = 4


def _tile_plan(seq, decode_rows):
    return dict(prompt=dict(wide=min(1024, seq), outproj=min(512, seq), down=min(256, seq)),
                decode=dict(wide=decode_rows, outproj=decode_rows, down=min(128, decode_rows)))


def _cparams(n_axes):
    return pltpu.CompilerParams(dimension_semantics=("arbitrary",) * n_axes,
                                vmem_limit_bytes=VMEM_LIMIT)


def _sigmoid(x):
    return 1.0 / (1.0 + jnp.exp(-x))


def _modnorm(x, lnw, sc, sh):
    y = x * lax.rsqrt(jnp.mean(x * x, axis=-1, keepdims=True) + EPS)
    return (y * lnw) * (1.0 + sc) + sh


def _dot(a, b):
    return jnp.dot(a, b, preferred_element_type=F32)


def _dot_nt(a, b):
    return lax.dot_general(a, b, (((1,), (1,)), ((), ())), preferred_element_type=F32)


def _dot_tn(a, b):
    return lax.dot_general(a, b, (((0,), (0,)), ((), ())), preferred_element_type=F32)


def _ada_kernel(c_ref, w_ref, b_ref, o_ref):
    c = c_ref[...]
    s = (c * _sigmoid(c)).astype(BF16)
    o_ref[...] = _dot(s, w_ref[...].astype(BF16)) + b_ref[...]


def _ada(c_all, w_ada, b_ada):
    depth, d, n6 = w_ada.shape
    rows = c_all.shape[0]
    tn = min(1024, n6)
    return pl.pallas_call(
        _ada_kernel,
        grid=(depth, n6 // tn),
        in_specs=[pl.BlockSpec((rows, d), lambda l, n: (0, 0)),
                  pl.BlockSpec((None, d, tn), lambda l, n: (l, 0, n)),
                  pl.BlockSpec((None, 1, tn), lambda l, n: (l, 0, n))],
        out_specs=pl.BlockSpec((None, rows, tn), lambda l, n: (l, 0, n)),
        out_shape=jax.ShapeDtypeStruct((depth, rows, n6), F32),
        compiler_params=_cparams(2),
        name="ada",
    )(c_all, w_ada, b_ada.reshape(depth, 1, n6))


def _prenorm_kernel(x_ref, lnw_ref, sc_ref, sh_ref, h_ref):
    h_ref[...] = _modnorm(x_ref[...], lnw_ref[...], sc_ref[...], sh_ref[...]).astype(BF16)


class _Mod:
    def __init__(self, arr, layer, part, groups):
        self.arr, self.layer, self.part, self.groups = arr, layer, part, groups

    def spec(self, tm):
        l, j, arr = self.layer, self.part, self.arr
        if arr.shape[1] == 1:
            base = l * self.groups * 6 + j
            return pl.BlockSpec((None, 1, arr.shape[2]), lambda g, r: (base + g * 6, 0, 0))
        return pl.BlockSpec((None, tm, arr.shape[2] // 6), lambda g, r: (l, r, j))


def _mod_spec(mod, tm):
    return mod.spec(tm)


def _prenorm(x, lnw, sc, sh, tm):
    g, r, d = x.shape
    return pl.pallas_call(
        _prenorm_kernel,
        grid=(g, r // tm),
        in_specs=[pl.BlockSpec((None, tm, d), lambda i, j: (i, j, 0)),
                  pl.BlockSpec((1, d), lambda i, j: (0, 0)),
                  _mod_spec(sc, tm), _mod_spec(sh, tm)],
        out_specs=pl.BlockSpec((None, tm, d), lambda i, j: (i, j, 0)),
        out_shape=jax.ShapeDtypeStruct((g, r, d), BF16),
        compiler_params=_cparams(2),
        name="prenorm",
    )(x, lnw, sc.arr, sh.arr)


def _rms_heads(a, w):
    outs = []
    for j in range(a.shape[1] // HD):
        t = a[:, j * HD:(j + 1) * HD]
        outs.append(t * lax.rsqrt(jnp.mean(t * t, axis=-1, keepdims=True) + EPS) * w)
    return outs


def _side_cast_specs(casts, n_n, n_i):
    in_specs, out_specs, out_shape, arrays, passes = [], [], [], [], []
    for w, layer in casts:
        _, rows, cols = w.shape
        p = max(q for q in range(1, n_n + 1)
                if rows % (q * n_i) == 0 and (rows // (q * n_i)) % BF16_SUBLANES == 0)
        chunk = rows // (p * n_i)
        pick = lambda n, i, p=p: jnp.where(n < p, n * n_i + i, p * n_i - 1)
        in_specs.append(pl.BlockSpec((None, chunk, cols), lambda n, i, layer=layer, pick=pick: (layer, pick(n, i), 0)))
        out_specs.append(pl.BlockSpec((None, chunk, cols), lambda n, i, pick=pick: (0, pick(n, i), 0)))
        out_shape.append(jax.ShapeDtypeStruct((1, rows, cols), BF16))
        arrays.append(w)
        passes.append(None if p == n_n else p)
    return in_specs, out_specs, out_shape, arrays, tuple(passes)


def _side_casts(rest, passes):
    n_casts = len(passes)
    for src, dst, p in zip(rest[:n_casts], rest[n_casts + 1:], passes):
        if p is None:
            dst[...] = src[...].astype(BF16)
        else:
            @pl.when(pl.program_id(0) < p)
            def _cast(src=src, dst=dst):
                dst[...] = src[...].astype(BF16)

    return rest[n_casts]


def _inproj_kernel(h_ref, w_ref, cos_ref, sin_ref, qn_ref, kn_ref, *rest, tn, cast_passes):
    o_ref = _side_casts(rest, cast_passes)
    n = pl.program_id(0)
    half = DK // 2
    tm = h_ref.shape[0]
    tr = min(tm, INPROJ_SLAB_ROWS)
    slabs = [(slice(r, r + tr), j) for j in range(tn // DK) for r in range(0, tm, tr)]
    full_slabs = [(slice(0, tm), j) for j in range(tn // DK)]

    def slab(rows, j):
        return _dot(h_ref[rows, :], w_ref[:, j * DK:(j + 1) * DK])

    @pl.when(n < 4)
    def _rotary():
        scale = jnp.where(n < 2, 1.0, DK ** -0.5).astype(F32)
        for rows, j in full_slabs:
            a = slab(rows, j)
            cos = cos_ref[rows, :]
            sin = sin_ref[rows, :]
            t1 = a[:, :half]
            t2 = a[:, half:]
            o_ref[rows, j * DK:j * DK + half] = ((t1 * cos - t2 * sin) * scale).astype(BF16)
            o_ref[rows, j * DK + half:(j + 1) * DK] = ((t1 * sin + t2 * cos) * scale).astype(BF16)

    def _rms_slabs(w, n_chunks):
        for rows, j in slabs:
            if j < n_chunks:
                for i, y in enumerate(_rms_heads(slab(rows, j), w)):
                    o_ref[rows, j * DK + i * HD:j * DK + (i + 1) * HD] = y.astype(BF16)

    @pl.when((n == 8) | (n == 9))
    def _qnorm():
        _rms_slabs(qn_ref[...], tn // DK)

    @pl.when(n == 10)
    def _kv():
        if tn // 2 >= DK:
            _rms_slabs(kn_ref[...], tn // (2 * DK))
        else:
            a = _dot(h_ref[...], w_ref[:, :tn // 2])
            for i, y in enumerate(_rms_heads(a, kn_ref[...])):
                o_ref[:, i * HD:(i + 1) * HD] = y.astype(BF16)
        o_ref[:, tn // 2:] = _dot(h_ref[...], w_ref[:, tn // 2:]).astype(BF16)

    @pl.when((n == 4) | (n == 5))
    def _plain():
        o_ref[...] = _dot(h_ref[...], w_ref[...]).astype(BF16)

    @pl.when((n == 6) | (n == 7))
    def _silu_gate():
        for rows, j in slabs:
            a = slab(rows, j)
            o_ref[rows, j * DK:(j + 1) * DK] = (a * _sigmoid(a)).astype(BF16)

    @pl.when(n > 10)
    def _sigmoid_gate():
        for rows, j in slabs:
            o_ref[rows, j * DK:(j + 1) * DK] = _sigmoid(slab(rows, j)).astype(BF16)


def _inproj(h, w, layer, cos, sin, qn, kn, tm, casts=()):
    m, d = h.shape
    width = w.shape[2]
    tn = d // 2
    nblk = cos.shape[0] // tm
    rope_block = lambda n, i: (jnp.where(n < 4, i % nblk, 0), 0)
    c_in, c_out, c_shape, c_arrays, c_passes = _side_cast_specs(casts, width // tn, m // tm)
    out = pl.pallas_call(
        functools.partial(_inproj_kernel, tn=tn, cast_passes=c_passes),
        grid=(width // tn, m // tm),
        in_specs=[pl.BlockSpec((tm, d), lambda n, i: (i, 0)),
                  pl.BlockSpec((None, d, tn), lambda n, i: (layer, 0, n)),
                  pl.BlockSpec((tm, HD), rope_block), pl.BlockSpec((tm, HD), rope_block),
                  pl.BlockSpec((1, HD), lambda n, i: (0, 0)),
                  pl.BlockSpec((1, HD), lambda n, i: (0, 0))] + c_in,
        out_specs=[pl.BlockSpec((tm, tn), lambda n, i: (i, n))] + c_out,
        out_shape=[jax.ShapeDtypeStruct((m, width), BF16)] + c_shape,
        compiler_params=_cparams(2),
        name="inproj",
    )(h, w, cos, sin, qn, kn, *c_arrays)
    return out[0], list(out[1:])


def _kvproj_kernel(h_ref, w_ref, kn_ref, o_ref):
    kvw = o_ref.shape[1] // 2
    a = _dot(h_ref[...], w_ref[...])
    for i, y in enumerate(_rms_heads(a[:, :kvw], kn_ref[...])):
        o_ref[:, i * HD:(i + 1) * HD] = y
    o_ref[:, kvw:] = a[:, kvw:]


def _kvproj(h, w, layer, kn, tm, n_blocks, row_block):
    d = h.shape[1]
    kvw = d // 4
    col_block = (5 * d) // (2 * kvw)
    return pl.pallas_call(
        _kvproj_kernel,
        grid=(n_blocks,),
        in_specs=[pl.BlockSpec((tm, d), lambda i: (row_block(i), 0)),
                  pl.BlockSpec((None, d, 2 * kvw), lambda i: (layer, 0, col_block)),
                  pl.BlockSpec((1, HD), lambda i: (0, 0))],
        out_specs=pl.BlockSpec((tm, 2 * kvw), lambda i: (i, 0)),
        out_shape=jax.ShapeDtypeStruct((n_blocks * tm, 2 * kvw), F32),
        compiler_params=_cparams(1),
        name="kvproj",
    )(h, w, kn)


def _segment_offsets(d):
    kvw = d // 4
    return dict(q=0, k=d, v=2 * d, g=3 * d, qa=4 * d, ka=5 * d, va=5 * d + kvw,
                mr=5 * d + 2 * kvw, ma=6 * d + 2 * kvw, width=7 * d + 2 * kvw)


def _retention_gate(o, g_act, mr_act):
    on = o * lax.rsqrt(jnp.mean(o * o, axis=-1, keepdims=True) + EPS)
    return mr_act.astype(F32) * (on * g_act.astype(F32))


def _mix_prompt_kernel(sink_ref, sdec_ref, blk_ref, prev_ref, dintra_ref, qdec_ref, kdec_ref,
                       *rest, d):
    out_ref, s_ref, macc_ref = rest[-3:]
    c = pl.program_id(1)
    off = _segment_offsets(d)
    n_ret = d // DK
    n_kv = d // (HD * GROUP)
    kvw = n_kv * HD
    blk = WINDOW
    n_sub = blk_ref.shape[0] // blk

    @pl.when(c == 0)
    def _init():
        s_ref[...] = jnp.zeros_like(s_ref)

    qi = lax.broadcasted_iota(jnp.int32, (blk, 2 * blk), 0)
    kj = lax.broadcasted_iota(jnp.int32, (blk, 2 * blk), 1)
    diff = kj - qi
    band = (diff >= 0) & (diff <= WINDOW)
    scale = HD ** -0.5
    kvs = range(n_kv)
    heads = range(n_ret)
    ones = jnp.ones((2 * blk, HD), BF16)

    for sub in range(n_sub):
        rows = slice(sub * blk, (sub + 1) * blk)

        def seg(name, h, width=DK, rows=rows):
            return blk_ref[rows, off[name] + h * width:off[name] + (h + 1) * width]

        if sub == 0:
            prev_k = lambda kv: prev_ref[:, kv * HD:(kv + 1) * HD]
            prev_v = lambda kv: prev_ref[:, kvw + kv * HD:kvw + (kv + 1) * HD]
            mask = band & (kj >= jnp.where(c == 0, blk, 0))
        else:
            before = slice((sub - 1) * blk, sub * blk)
            prev_k = lambda kv, r=before: blk_ref[r, off["ka"] + kv * HD:off["ka"] + (kv + 1) * HD]
            prev_v = lambda kv, r=before: blk_ref[r, off["va"] + kv * HD:off["va"] + (kv + 1) * HD]
            mask = band

        scores, values = [], []
        for kv in kvs:
            qs = jnp.concatenate([seg("qa", kv * GROUP + g, HD) for g in range(GROUP)], axis=0)
            k2 = jnp.concatenate([prev_k(kv), seg("ka", kv, HD)], axis=0)
            v2 = jnp.concatenate([prev_v(kv), seg("va", kv, HD)], axis=0)
            scores.append(_dot_nt(qs, k2))
            values.append(jnp.concatenate([v2, ones], axis=1))
        att = [_dot_nt(seg("q", h), seg("k", h)) for h in heads]
        inter = [_dot(seg("q", h), s_ref[h].astype(BF16)) for h in heads]
        upd = [_dot_tn((seg("k", h).astype(F32) * kdec_ref[h]).astype(BF16), seg("v", h)) for h in heads]
        for h in heads:
            s_ref[h] = s_ref[h] * sdec_ref[h] + upd[h]
        att = [(att[h] * dintra_ref[h]).astype(BF16) for h in heads]
        o_ret = [_dot(att[h], seg("v", h)) + inter[h] * qdec_ref[h] for h in heads]
        for h in heads:
            macc_ref[rows, h * DK:(h + 1) * DK] = _retention_gate(o_ret[h], seg("g", h), seg("mr", h))

        probs, sink_terms = [], []
        for kv in kvs:
            p_rows = []
            for g in range(GROUP):
                sink = sink_ref[kv * GROUP + g] * (1.0 / scale)
                s = jnp.where(mask, scores[kv][g * blk:(g + 1) * blk], NEG_INF)
                m = jnp.maximum(jnp.max(s, axis=-1, keepdims=True), sink)
                p_rows.append(jnp.exp2((s - m) * (scale * LOG2E)).astype(BF16))
                sink_terms.append(jnp.exp2((sink - m) * (scale * LOG2E)))
            probs.append(jnp.concatenate(p_rows, axis=0))
        pv = [_dot(probs[kv], values[kv]) for kv in kvs]
        for kv in kvs:
            for g in range(GROUP):
                head = kv * GROUP + g
                hs = slice(head * HD, (head + 1) * HD)
                o = pv[kv][g * blk:(g + 1) * blk, :HD]
                denom = pv[kv][g * blk:(g + 1) * blk, HD:] + sink_terms[head]
                out_ref[rows, hs] = (macc_ref[rows, hs]
                                     + seg("ma", head, HD).astype(F32) * (o / denom)).astype(BF16)


def _mix_prompt(proj, sinks, tabs, layer, depth, prev_state, b, seq, d):
    off = _segment_offsets(d)
    n_ret = d // DK
    kvw = d // 4
    nblk = seq // WINDOW
    proj3 = proj.reshape(b, seq, off["width"])
    kv_col_block = off["ka"] // (2 * kvw)
    smem = pl.BlockSpec(memory_space=pltpu.SMEM)
    const3 = lambda shape: pl.BlockSpec(shape, lambda i, c: (0, 0, 0))
    n_sub = PROMPT_BLOCKS_PER_STEP if nblk % PROMPT_BLOCKS_PER_STEP == 0 else 1
    rows = n_sub * WINDOW
    in_specs = [smem, smem,
                pl.BlockSpec((None, rows, off["width"]), lambda i, c: (i, c, 0)),
                pl.BlockSpec((None, WINDOW, 2 * kvw),
                             lambda i, c: (i, jnp.maximum(c * n_sub - 1, 0), kv_col_block)),
                const3((n_ret, WINDOW, WINDOW)), const3((n_ret, WINDOW, DK)),
                const3((n_ret, WINDOW, DK))]
    args = [sinks, tabs["sdec"], proj3, proj3, tabs["dintra"], tabs["qdec"], tabs["kdec"]]
    aliases = {}
    if prev_state is not None:
        aliases[len(args)] = 1
        in_specs.append(pl.BlockSpec(memory_space=pl.ANY))
        args.append(prev_state)
    merged, state = pl.pallas_call(
        functools.partial(_mix_prompt_kernel, d=d),
        grid=(b, nblk // n_sub),
        in_specs=in_specs,
        out_specs=[pl.BlockSpec((None, rows, d), lambda i, c: (i, c, 0)),
                   pl.BlockSpec((None, None, n_ret, DK, DK), lambda i, c: (layer, i, 0, 0, 0))],
        out_shape=[jax.ShapeDtypeStruct((b, seq, d), BF16),
                   jax.ShapeDtypeStruct((depth, b, n_ret, DK, DK), F32)],
        scratch_shapes=[pltpu.VMEM((rows, d), F32)],
        input_output_aliases=aliases,
        compiler_params=_cparams(2),
        name="mix_prompt",
    )(*args)
    return merged.reshape(b * seq, d), state


def _mix_sample_kernel(sink_ref, sdec_ref, blk_ref, kvnew_ref, s0_ref, kc_ref, vc_ref, dmask_ref,
                       qdec_ref, kdec_ref, *rest, d, dec_seq, n_alias):
    out_ref, s_out_ref, kout_ref, vout_ref, pad_ref, macc_ref = rest[n_alias:]
    off = _segment_offsets(d)
    n_ret = d // DK
    n_kv = d // (HD * GROUP)
    kvw = n_kv * HD
    nb = SAMPLE_BATCHES_PER_STEP
    rows = nb * dec_seq
    pad = pad_ref.shape[0]
    p_k, p_v, p_ka, p_va = 0, d, 2 * d, 2 * d + kvw

    @pl.when(pl.program_id(0) == 0)
    def _init():
        pad_ref[...] = jnp.zeros_like(pad_ref)

    pad_ref[0:rows, p_k:p_k + 2 * d] = blk_ref[:, off["k"]:off["k"] + 2 * d]
    pad_ref[0:rows, p_ka:p_ka + 2 * kvw] = blk_ref[:, off["ka"]:off["ka"] + 2 * kvw]

    tok_bits = dec_seq.bit_length() - 1
    row_batch = lax.broadcasted_iota(jnp.int32, (rows, 1), 0) >> tok_bits
    pad_batch = lax.broadcasted_iota(jnp.int32, (pad, 1), 0) >> tok_bits

    def seg(name, h, width=DK):
        return blk_ref[:, off[name] + h * width:off[name] + (h + 1) * width]

    def padded(col0, h, width=DK):
        return pad_ref[:, col0 + h * width:col0 + (h + 1) * width]

    heads = range(n_ret)
    bis = range(nb)
    att = [_dot_nt(seg("q", h), padded(p_k, h)) for h in heads]
    inter = [[_dot(seg("q", h), s0_ref[bi, h].astype(BF16)) for bi in bis] for h in heads]
    kd = [(padded(p_k, h).astype(F32) * kdec_ref[h]).astype(BF16) for h in heads]
    v_b = [[jnp.where(pad_batch == bi, padded(p_v, h), jnp.zeros((pad, DK), BF16)) for bi in bis]
           for h in heads]
    upd = [[_dot_tn(kd[h], v_b[h][bi]) for bi in bis] for h in heads]
    for h in heads:
        for bi in bis:
            s_out_ref[bi, h] = s0_ref[bi, h] * sdec_ref[h] + upd[h][bi]
    att = [(att[h] * dmask_ref[h]).astype(BF16) for h in heads]
    for h in heads:
        own = inter[h][0]
        for bi in bis[1:]:
            own = jnp.where(row_batch == bi, inter[h][bi], own)
        o = _dot(att[h], padded(p_v, h)) + own * qdec_ref[h]
        macc_ref[:, h * DK:(h + 1) * DK] = _retention_gate(o, seg("g", h), seg("mr", h))

    qrows = GROUP * rows
    q_tok = lax.broadcasted_iota(jnp.int32, (qrows, 1), 0) & (dec_seq - 1)
    q_batch = (lax.broadcasted_iota(jnp.int32, (qrows, 1), 0) & (rows - 1)) >> tok_bits
    cache_j = lax.broadcasted_iota(jnp.int32, (qrows, WINDOW), 1)
    new_j = lax.broadcasted_iota(jnp.int32, (qrows, pad), 1)
    mask_cache = cache_j >= q_tok
    mask_new = [((new_j >> tok_bits) == bi) & ((new_j & (dec_seq - 1)) <= q_tok) for bi in bis]
    scale = HD ** -0.5
    kvs = range(n_kv)
    ones = jnp.ones((WINDOW, HD), BF16)
    qs = [jnp.concatenate([seg("qa", kv * GROUP + g, HD) for g in range(GROUP)], axis=0) for kv in kvs]
    s_new = [_dot_nt(qs[kv], padded(p_ka, kv, HD)) for kv in kvs]
    s_cache = [[_dot_nt(qs[kv], kc_ref[bi, :, kv, :].astype(BF16)) for bi in bis] for kv in kvs]
    v_new = [jnp.concatenate([padded(p_va, kv, HD), ones], axis=1) for kv in kvs]
    v_cache = [[jnp.concatenate([vc_ref[bi, :, kv, :].astype(BF16), ones], axis=1) for bi in bis]
               for kv in kvs]
    for kv in kvs:
        sink_col = jnp.concatenate(
            [jnp.full((rows, 1), sink_ref[kv * GROUP + g], F32) for g in range(GROUP)], axis=0)
        o_kv = None
        for bi in bis:
            s_c = jnp.where(mask_cache, s_cache[kv][bi] * scale, NEG_INF)
            s_n = jnp.where(mask_new[bi], s_new[kv] * scale, NEG_INF)
            m = jnp.maximum(jnp.maximum(jnp.max(s_c, axis=-1, keepdims=True),
                                        jnp.max(s_n, axis=-1, keepdims=True)), sink_col)
            pv = (_dot(jnp.exp(s_c - m).astype(BF16), v_cache[kv][bi])
                  + _dot(jnp.exp(s_n - m).astype(BF16), v_new[kv]))
            o = pv[:, :HD] / (pv[:, HD:] + jnp.exp(sink_col - m))
            o_kv = o if o_kv is None else jnp.where(q_batch == bi, o, o_kv)
        for g in range(GROUP):
            head = kv * GROUP + g
            hs = slice(head * HD, (head + 1) * HD)
            out_ref[:, hs] = (macc_ref[:, hs] + seg("ma", head, HD).astype(F32)
                              * o_kv[g * rows:(g + 1) * rows]).astype(BF16)

    keep = WINDOW - dec_seq
    for bi in range(nb):
        kout_ref[bi, 0:keep] = kc_ref[bi, dec_seq:WINDOW]
        vout_ref[bi, 0:keep] = vc_ref[bi, dec_seq:WINDOW]
        for t in range(dec_seq):
            r = bi * dec_seq + t
            for kv in range(n_kv):
                kout_ref[bi, keep + t, kv:kv + 1, :] = kvnew_ref[r:r + 1, kv * HD:(kv + 1) * HD]
                vout_ref[bi, keep + t, kv:kv + 1, :] = kvnew_ref[r:r + 1, kvw + kv * HD:kvw + (kv + 1) * HD]


def _mix_sample(proj, kvnew, sinks, tabs, layer, state_all, k_all, v_all, prev_outs, dec_seq, d):
    off = _segment_offsets(d)
    n_ret = d // DK
    n_kv = d // (HD * GROUP)
    kvw = n_kv * HD
    nb = SAMPLE_BATCHES_PER_STEP
    b = state_all.shape[1]
    rows = nb * dec_seq
    smem = pl.BlockSpec(memory_space=pltpu.SMEM)
    const3 = lambda shape: pl.BlockSpec(shape, lambda i: (0, 0, 0))
    state_spec = lambda: pl.BlockSpec((None, nb, n_ret, DK, DK), lambda i: (layer, i, 0, 0, 0))
    win_spec = lambda: pl.BlockSpec((None, nb, WINDOW, n_kv, HD), lambda i: (layer, i, 0, 0, 0))
    in_specs = [smem, smem,
                pl.BlockSpec((rows, off["width"]), lambda i: (i, 0)),
                pl.BlockSpec((rows, 2 * kvw), lambda i: (i, 0)),
                state_spec(), win_spec(), win_spec(),
                const3((n_ret, rows, WINDOW)), const3((n_ret, rows, DK)), const3((n_ret, WINDOW, DK))]
    args = [sinks, tabs["sdec"], proj, kvnew, state_all, k_all, v_all,
            tabs["dmask"], tabs["qdec"], tabs["kdec"]]
    aliases = {}
    if prev_outs is not None:
        for j, arr in enumerate(prev_outs):
            aliases[len(args)] = 1 + j
            in_specs.append(pl.BlockSpec(memory_space=pl.ANY))
            args.append(arr)
    n_alias = len(aliases)
    merged, state, kwin, vwin = pl.pallas_call(
        functools.partial(_mix_sample_kernel, d=d, dec_seq=dec_seq, n_alias=n_alias),
        grid=(b // nb,),
        in_specs=in_specs,
        out_specs=[pl.BlockSpec((rows, d), lambda i: (i, 0)), state_spec(), win_spec(), win_spec()],
        out_shape=[jax.ShapeDtypeStruct((b * dec_seq, d), BF16),
                   jax.ShapeDtypeStruct(state_all.shape, F32),
                   jax.ShapeDtypeStruct(k_all.shape, F32),
                   jax.ShapeDtypeStruct(v_all.shape, F32)],
        scratch_shapes=[pltpu.VMEM((WINDOW, 2 * d + 2 * kvw), BF16),
                        pltpu.VMEM((rows, d), F32)],
        input_output_aliases=aliases,
        compiler_params=_cparams(1),
        name="mix_sample",
    )(*args)
    return merged, (state, kwin, vwin)


def _proj_residual_kernel(a_ref, w_ref, x_ref, g_ref, *rest, with_norm):
    if with_norm:
        lnw_ref, sc_ref, sh_ref, xo_ref, ho_ref = rest
    else:
        (xo_ref,) = rest
    d = xo_ref.shape[1]
    half = DK // 2
    slabs = [slice(j * DK, (j + 1) * DK) for j in range(d // DK)]
    sq = None
    for cs in slabs:
        x = x_ref[:, cs] + g_ref[:, cs] * _dot(a_ref[...], w_ref[:, cs])
        xo_ref[:, cs] = x
        if with_norm:
            x2 = x * x
            part = x2[:, :half] + x2[:, half:]
            sq = part if sq is None else sq + part
    if with_norm:
        r = lax.rsqrt(jnp.sum(sq, axis=-1, keepdims=True) / d + EPS)
        for cs in slabs:
            y = (xo_ref[:, cs] * r) * lnw_ref[:, cs]
            ho_ref[:, cs] = (y * (1.0 + sc_ref[:, cs]) + sh_ref[:, cs]).astype(BF16)


def _proj_residual(a, w, layer, x, gate, norm, tm, name):
    g, r, d = x.shape
    kdim = w.shape[1]
    tok = lambda: pl.BlockSpec((None, tm, d), lambda i, j: (i, j, 0))
    in_specs = [pl.BlockSpec((None, tm, kdim), lambda i, j: (i, j, 0)),
                pl.BlockSpec((None, kdim, d), lambda i, j: (layer, 0, 0), pipeline_mode=pl.Buffered(1)),
                tok(), _mod_spec(gate, tm)]
    args = [a.reshape(g, r, kdim), w, x, gate.arr]
    out_specs = [tok()]
    out_shape = [jax.ShapeDtypeStruct((g, r, d), F32)]
    if norm is not None:
        lnw, sc, sh = norm
        in_specs += [pl.BlockSpec((1, d), lambda i, j: (0, 0)), _mod_spec(sc, tm), _mod_spec(sh, tm)]
        args += [lnw, sc.arr, sh.arr]
        out_specs.append(tok())
        out_shape.append(jax.ShapeDtypeStruct((g, r, d), BF16))
    out = pl.pallas_call(
        functools.partial(_proj_residual_kernel, with_norm=norm is not None),
        grid=(g, r // tm),
        in_specs=in_specs, out_specs=out_specs, out_shape=out_shape,
        compiler_params=_cparams(2),
        name=name,
    )(*args)
    return (out[0], out[1]) if norm is not None else (out[0], None)


def _up_kernel(h_ref, w_ref, *rest, cast_passes):
    o_ref = _side_casts(rest, cast_passes)
    r = jnp.maximum(_dot(h_ref[...], w_ref[...]), 0.0)
    o_ref[...] = (r * r).astype(BF16)


def _up(h, w, layer, tm, casts=()):
    m, d = h.shape
    f = w.shape[2]
    tn = min(2048, f)
    c_in, c_out, c_shape, c_arrays, c_passes = _side_cast_specs(casts, f // tn, m // tm)
    out = pl.pallas_call(
        functools.partial(_up_kernel, cast_passes=c_passes),
        grid=(f // tn, m // tm),
        in_specs=[pl.BlockSpec((tm, d), lambda n, i: (i, 0)),
                  pl.BlockSpec((None, d, tn), lambda n, i: (layer, 0, n))] + c_in,
        out_specs=[pl.BlockSpec((tm, tn), lambda n, i: (i, n))] + c_out,
        out_shape=[jax.ShapeDtypeStruct((m, f), BF16)] + c_shape,
        compiler_params=_cparams(2),
        name="mlp_up",
    )(h, w, *c_arrays)
    return out[0], list(out[1:])


def _rope_tables(pos):
    half = DK // 2
    inv = 1.0 / (ROPE_BASE ** (jnp.arange(half, dtype=F32) / half))
    ang = pos.astype(F32)[:, None] * inv[None, :]
    return jnp.cos(ang), jnp.sin(ang)


def _decay_tables(n_ret, chunk, reps, pad_cols):
    log_g = jnp.log1p(-jnp.exp2(-5.0 - jnp.arange(n_ret, dtype=F32)))
    rows = reps * chunk
    r = jnp.arange(rows)
    cidx = jnp.arange(pad_cols)
    idx = (r % chunk).astype(F32)
    diff = idx[:, None] - (cidx % chunk).astype(F32)[None, :]
    ok = (diff >= 0) & ((r // chunk)[:, None] == (cidx // chunk)[None, :]) & (cidx < rows)[None, :]
    dintra = jnp.where(ok, jnp.exp(jnp.where(ok, diff, 0.0) * log_g[:, None, None]), 0.0)
    qdec = jnp.exp((idx + 1.0) * log_g[:, None])[..., None]
    kidx = (jnp.arange(max(rows, pad_cols)) % chunk).astype(F32)
    kdec = jnp.exp((chunk - 1.0 - kidx) * log_g[:, None])[..., None]
    sdec = jnp.exp(chunk * log_g)
    return dict(dintra=dintra, dmask=dintra,
                qdec=jnp.broadcast_to(qdec, (n_ret, rows, DK)),
                kdec=jnp.broadcast_to(kdec, (n_ret, kidx.shape[0], DK)), sdec=sdec)


def kernel(x_prompt, x_sample, c_prompt, c_sample, state_ret, cache_k_win, cache_v_win, norm1_w,
           norm2_w, w_ada, b_ada, w_in, q_norm_w, k_norm_w, sinks, w_out, w_up, w_down):
    bp, seq, d = x_prompt.shape
    bs, dec_seq, _ = x_sample.shape
    depth = w_in.shape[0]
    n_ret = d // DK
    kvw = d // 4
    n_kv = kvw // HD
    win = cache_k_win.shape[2]
    assert win == WINDOW and seq % WINDOW == 0 and bs % SAMPLE_BATCHES_PER_STEP == 0
    assert dec_seq & (dec_seq - 1) == 0

    ms_rows = bs * dec_seq
    n_c = ms_rows + bp
    c_rows = -(-n_c // 16) * 16
    c_all = jnp.concatenate([jnp.repeat(c_sample, dec_seq, axis=0), c_prompt,
                             jnp.zeros((c_rows - n_c, d), F32)], axis=0)
    mods = _ada(c_all, w_ada, b_ada)

    mods_prompt = mods[:, ms_rows:n_c].reshape(depth * bp * 6, 1, d)

    def group_mods(l):
        return ([_Mod(mods_prompt, l, j, bp) for j in range(6)],
                [_Mod(mods, l, j, 1) for j in range(6)])

    wi = w_in[0:1].astype(BF16)

    cos_p, sin_p = _rope_tables(jnp.arange(seq, dtype=jnp.int32))
    cos_s, sin_s = _rope_tables(PAST_LEN + jnp.arange(dec_seq, dtype=jnp.int32))
    cos_s = jnp.tile(cos_s, (bs, 1))
    sin_s = jnp.tile(sin_s, (bs, 1))
    tabs_p = _decay_tables(n_ret, WINDOW, 1, WINDOW)
    tabs_s = _decay_tables(n_ret, dec_seq, SAMPLE_BATCHES_PER_STEP, WINDOW)

    tiles = _tile_plan(seq, ms_rows)
    tp, ts = tiles["prompt"], tiles["decode"]
    nblk = seq // WINDOW
    xp = x_prompt
    xs = x_sample.reshape(1, ms_rows, d)

    mods_p, mods_s = group_mods(0)
    hp = _prenorm(xp, norm1_w[0:1], mods_p[1], mods_p[0], tp["outproj"])
    hs = _prenorm(xs, norm1_w[0:1], mods_s[1], mods_s[0], ts["outproj"])

    outs = {k: [] for k in ("kp", "vp")}
    prompt_state = None
    sample_state = None
    for l in range(depth):
        qn = q_norm_w[l:l + 1]
        kn = k_norm_w[l:l + 1]
        if l + 1 < depth:
            nxt_p, nxt_s = group_mods(l + 1)
            norm_p = (norm1_w[l + 1:l + 2], nxt_p[1], nxt_p[0])
            norm_s = (norm1_w[l + 1:l + 2], nxt_s[1], nxt_s[0])
        else:
            norm_p = norm_s = None

        hp2 = hp.reshape(bp * seq, d)
        proj, (wu, wo) = _inproj(hp2, wi, 0, cos_p, sin_p, qn, kn, tp["wide"],
                                 casts=[(w_up, l), (w_out, l)])
        kv_tail = _kvproj(hp2, wi, 0, kn, WINDOW, bp, lambda i: i * nblk + nblk - 1)
        merged, prompt_state = _mix_prompt(proj, sinks[l], tabs_p, l, depth, prompt_state, bp, seq, d)
        outs["kp"].append(kv_tail[:, :kvw].reshape(bp, WINDOW, n_kv, HD))
        outs["vp"].append(kv_tail[:, kvw:].reshape(bp, WINDOW, n_kv, HD))
        xp, h2 = _proj_residual(merged, wo, 0, xp, mods_p[2], (norm2_w[l:l + 1], mods_p[4], mods_p[3]),
                                tp["outproj"], "outproj")
        next_w_in = [(w_in, l + 1)] if l + 1 < depth else []
        u, (wd, *wi_next) = _up(h2.reshape(bp * seq, d), wu, 0, tp["wide"],
                                casts=[(w_down, l)] + next_w_in)
        xp, hp = _proj_residual(u, wd, 0, xp, mods_p[5], norm_p, tp["down"], "mlp_down")

        hs2 = hs.reshape(ms_rows, d)
        proj, _ = _inproj(hs2, wi, 0, cos_s, sin_s, qn, kn, ts["wide"])
        kv_new = _kvproj(hs2, wi, 0, kn, ms_rows, 1, lambda i: i)
        merged, sample_state = _mix_sample(proj, kv_new, sinks[l], tabs_s, l, state_ret, cache_k_win,
                                           cache_v_win, sample_state, dec_seq, d)
        xs, h2 = _proj_residual(merged, wo, 0, xs, mods_s[2], (norm2_w[l:l + 1], mods_s[4], mods_s[3]),
                                ts["outproj"], "outproj")
        u, _ = _up(h2.reshape(ms_rows, d), wu, 0, ts["wide"])
        xs, hs = _proj_residual(u, wd, 0, xs, mods_s[5], norm_s, ts["down"], "mlp_down")

        if l + 1 < depth:
            mods_p, mods_s = nxt_p, nxt_s
            wi = wi_next[0]

    return (xp, xs.reshape(bs, dec_seq, d), prompt_state, jnp.stack(outs["kp"]),
            jnp.stack(outs["vp"])) + tuple(sample_state)
```
